```python
import math
import jax, jax.numpy as jnp
from jax import lax
import numpy as np

D_MODEL = 1024
BATCH = 1
SEQ = 16384
DEPTH = 2
DEC_BATCH = 32
DEC_SEQ = 8
PAST_LEN = 16384
PAGE_SIZE = 128

D_MIX = D_MODEL
N_MIXERS = 4
W_GROUP = D_MIX // N_MIXERS
HEAD_DIM = 64
N_H = W_GROUP // HEAD_DIM
IDX_HEADS = 4
IDX_DIM = 64
TOPK_MAX = 256
Q_BLOCK = 128
GMLP_CHUNK = 128
CONV_W = 4
GDN_CHUNK = 64
D_FF = 4 * D_MODEL
DEEPNORM_ALPHA = (2 * DEPTH) ** 0.25
DEEPNORM_BETA = (8 * DEPTH) ** -0.25
LN_EPS = 1e-5
NORM_EPS = 1e-6
PROJ_SPLITS = (
    W_GROUP, W_GROUP, W_GROUP,
    IDX_HEADS * IDX_DIM, IDX_DIM, IDX_HEADS,
    W_GROUP, W_GROUP,
    W_GROUP, W_GROUP, W_GROUP,
    3 * W_GROUP, N_H, N_H, W_GROUP,
)
N_IN = sum(PROJ_SPLITS)

kernel_name = 'hybrid_dsa_gmlp_stickbreak_gdn_step'


def _ln_f32(x):
    xf = x.astype(jnp.float32)
    mu = jnp.mean(xf, -1, keepdims=True)
    var = jnp.mean(jnp.square(xf - mu), -1, keepdims=True)
    return (xf - mu) * lax.rsqrt(var + LN_EPS)


def layer_norm(x, g, b):
    return (_ln_f32(x) * g + b).astype(x.dtype)


def layer_norm_plain(x):
    return _ln_f32(x).astype(x.dtype)


def modulate(x, shift, scale):
    return x * (1 + scale) + shift


def alibi_slopes():
    return jnp.exp2(-8.0 * jnp.arange(1, N_H + 1, dtype=jnp.float32) / N_H)


def to_blocks(a):
    B, L = a.shape[:2]
    return jnp.moveaxis(a.reshape((B, L // Q_BLOCK, Q_BLOCK) + a.shape[2:]), 1, 0)


def from_blocks(a):
    nb, B, qb = a.shape[:3]
    return jnp.moveaxis(a, 0, 1).reshape((B, nb * qb) + a.shape[3:])


def take_rows(a, idx):
    return jax.vmap(lambda r, i: r[i])(a, idx)


def project(h, w_in):
    p = h @ w_in
    out, start = [], 0
    for n in PROJ_SPLITS:
        out.append(p[..., start:start + n])
        start += n
    return out


def indexer_topk(qi, wi, q_pos, ki, k_pos, topk):
    f32 = jnp.float32
    s = jnp.einsum('bqhd,bld->bqhl', qi.astype(f32), ki.astype(f32)) * (IDX_DIM ** -0.5)
    score = jnp.einsum('bqh,bqhl->bql', wi.astype(f32) * (IDX_HEADS ** -0.5), jax.nn.relu(s))
    score = jnp.where(k_pos[None, None, :] <= q_pos[None, :, None], score, -jnp.inf)
    return lax.top_k(score, topk)[1]


def dsa_attend(q, q_pos, k_sel, v_sel, sel_pos):
    f32 = jnp.float32
    s = jnp.einsum('bqhd,bqkhd->bhqk', q.astype(f32), k_sel.astype(f32)) * (HEAD_DIM ** -0.5)
    dist = (q_pos[None, :, None] - sel_pos).astype(f32)
    s = s - alibi_slopes()[None, :, None, None] * dist[:, None]
    s = jnp.where((dist >= 0)[:, None], s, -jnp.inf)
    p = jax.nn.softmax(s, axis=-1)
    return jnp.einsum('bhqk,bqkhd->bqhd', p, v_sel.astype(f32)).astype(q.dtype)


def gmlp_mix(u, v, w_s, b_s):
    B, L, _ = u.shape
    C = min(L, GMLP_CHUNK)
    vn = layer_norm_plain(v)
    vc = vn.reshape(B, L // C, C, N_H, HEAD_DIM)
    w = jnp.tril(w_s[:, :C, :C])
    mixed = jnp.einsum('gij,bnjgc->bnigc', w, vc) + b_s[:, :C].T[None, None, :, :, None]
    return u * mixed.reshape(B, L, W_GROUP), vn


def stick_breaking(q, q_pos, k, v, k_pos):
    f32 = jnp.float32
    z = jnp.einsum('bqhd,blhd->bhql', q.astype(f32), k.astype(f32)) * (HEAD_DIM ** -0.5)
    visible = k_pos[None, :] < q_pos[:, None]
    log_1m = jnp.where(visible, jax.nn.log_sigmoid(-z), 0.0)
    tail = lax.cumsum(log_1m, axis=3, reverse=True) - log_1m
    w = jnp.where(visible, jnp.exp(jax.nn.log_sigmoid(z) + tail), 0.0)
    return jnp.einsum('bhql,blhd->bqhd', w, v.astype(f32)).astype(q.dtype)


def short_conv(x, buf, w):
    L = x.shape[1]
    xp = jnp.concatenate([buf, x], axis=1)
    y = xp[:, 0:L] * w[0]
    for j in range(1, CONV_W):
        y = y + xp[:, j:j + L] * w[j]
    return jax.nn.silu(y), xp[:, -(CONV_W - 1):]


def l2norm(x):
    return x * lax.rsqrt(jnp.sum(x * x, -1, keepdims=True) + NORM_EPS)


def gated_delta(q, k, v, g, beta, S0):
    f32 = jnp.float32
    B, L, H, DK = q.shape
    C = math.gcd(L, GDN_CHUNK)
    n = L // C

    def chunks(a):
        a = a.astype(f32).reshape((B, n, C, H) + a.shape[3:])
        return jnp.moveaxis(a, (1, 3), (0, 2))

    qc = chunks(q) * (DK ** -0.5)
    kc, vc, gc, bc = chunks(k), chunks(v), chunks(g), chunks(beta)
    gcum = jnp.cumsum(gc, axis=-1)
    incl = jnp.tril(jnp.ones((C, C), bool))
    strict = jnp.tril(jnp.ones((C, C), bool), -1)
    diff = gcum[..., :, None] - gcum[..., None, :]
    decay = jnp.where(incl, jnp.exp(jnp.where(incl, diff, 0.0)), 0.0)
    kb = kc * bc[..., None]
    M = jnp.where(strict, jnp.einsum('nbhik,nbhjk->nbhij', kb, kc) * decay, 0.0)
    eye = jnp.eye(C, dtype=f32)
    T = lax.linalg.triangular_solve(eye + M, jnp.broadcast_to(eye, M.shape), left_side=True, lower=True)
    u = T @ (vc * bc[..., None])
    w = T @ (kb * jnp.exp(gcum)[..., None])
    attn = jnp.einsum('nbhik,nbhjk->nbhij', qc, kc) * decay

    def step(S, xs):
        qi, ki, ui, wi, gi, ai = xs
        vnew = ui - wi @ S
        o = (qi * jnp.exp(gi)[..., None]) @ S + ai @ vnew
        glast = gi[..., -1:]
        S = S * jnp.exp(glast)[..., None] + jnp.einsum('bhck,bhcv->bhkv', ki * jnp.exp(glast - gi)[..., None], vnew)
        return S, o

    S, o = lax.scan(step, S0.astype(f32), (qc, kc, u, w, gcum, attn))
    o = jnp.moveaxis(o, (0, 2), (1, 3)).reshape(B, L, H, v.shape[-1])
    return o, S


def gdn_branch(qkv, beta_raw, a_raw, gate, buf, S0, conv_w, a_log, dt_bias, norm_g):
    B, L, _ = qkv.shape
    f32 = jnp.float32
    qkv_c, new_buf = short_conv(qkv, buf, conv_w)
    q, k, v = jnp.split(qkv_c.astype(f32), 3, axis=-1)
    q = l2norm(q.reshape(B, L, N_H, HEAD_DIM))
    k = l2norm(k.reshape(B, L, N_H, HEAD_DIM))
    v = v.reshape(B, L, N_H, HEAD_DIM)
    beta = jax.nn.sigmoid(beta_raw.astype(f32))
    g = -jnp.exp(a_log.astype(f32)) * jax.nn.softplus(a_raw.astype(f32) + dt_bias.astype(f32))
    o, S = gated_delta(q, k, v, g, beta, S0)
    o = o * lax.rsqrt(jnp.mean(o * o, -1, keepdims=True) + NORM_EPS) * norm_g.astype(f32)
    o = o * jax.nn.silu(gate.astype(f32).reshape(B, L, N_H, HEAD_DIM))
    return o.reshape(B, L, W_GROUP).astype(qkv.dtype), S.astype(qkv.dtype), new_buf


def mixers_prompt(h, w_in, conv_w, a_log, dt_bias, gdn_g, w_s, b_s):
    B, L, _ = h.shape
    (aq, ak, av, iq, ik, iw, bu, bv, cq, ck, cv, dqkv, dbeta, da, dgate) = project(h, w_in)
    aq, ak, av, cq, ck, cv = (t.reshape(B, L, N_H, HEAD_DIM) for t in (aq, ak, av, cq, ck, cv))
    iq = iq.reshape(B, L, IDX_HEADS, IDX_DIM)
    pos = jnp.arange(L)
    pos_blocks = pos.reshape(L // Q_BLOCK, Q_BLOCK)
    topk = min(TOPK_MAX, L // 4)

    def dsa_block(args):
        qb, qib, wib, pb = args
        idx = indexer_topk(qib, wib, pb, ik, pos, topk)
        return dsa_attend(qb, pb, take_rows(ak, idx), take_rows(av, idx), idx)

    out_a = from_blocks(lax.map(dsa_block, (to_blocks(aq), to_blocks(iq), to_blocks(iw), pos_blocks)))
    out_b, _ = gmlp_mix(jax.nn.gelu(bu), jax.nn.gelu(bv), w_s, b_s)
    out_c = from_blocks(lax.map(lambda a: stick_breaking(a[0], a[1], ck, cv, pos), (to_blocks(cq), pos_blocks)))
    out_d, S_new, buf_new = gdn_branch(
        dqkv, dbeta, da, dgate,
        jnp.zeros((B, CONV_W - 1, 3 * W_GROUP), h.dtype),
        jnp.zeros((B, N_H, HEAD_DIM, HEAD_DIM), jnp.float32),
        conv_w, a_log, dt_bias, gdn_g)
    mix = jnp.concatenate([out_a.reshape(B, L, W_GROUP), out_b, out_c.reshape(B, L, W_GROUP), out_d], axis=-1)
    return mix, (ak, av, ik, ck, cv, S_new, buf_new)


def mixers_sample(h, w_in, conv_w, a_log, dt_bias, gdn_g, w_s, b_s,
                  cache_dsa_k, cache_dsa_v, cache_dsa_kidx, cache_sb_k, cache_sb_v,
                  state_gdn_S, state_gdn_conv, page_table, l):
    B, S, _ = h.shape
    past = page_table.shape[1] * PAGE_SIZE
    L = past + S
    (aq, ak, av, iq, ik, iw, bu, bv, cq, ck, cv, dqkv, dbeta, da, dgate) = project(h, w_in)
    aq, ak, av, cq, ck, cv = (t.reshape(B, S, N_H, HEAD_DIM) for t in (aq, ak, av, cq, ck, cv))
    iq = iq.reshape(B, S, IDX_HEADS, IDX_DIM)
    q_pos = past + jnp.arange(S)
    k_pos = jnp.arange(L)
    ki_all = jnp.concatenate([cache_dsa_kidx[l, page_table].reshape(B, past, IDX_DIM), ik], axis=1)
    idx = indexer_topk(iq, iw, q_pos, ki_all, k_pos, min(TOPK_MAX, L // 4))
    in_past = (idx < past)[..., None, None]
    pidx = jnp.minimum(idx, past - 1)
    phys = jax.vmap(lambda pt, i: pt[i])(page_table, pidx // PAGE_SIZE)
    off = pidx % PAGE_SIZE
    nidx = jnp.clip(idx - past, 0, S - 1)
    k_sel = jnp.where(in_past, cache_dsa_k[l, phys, off], take_rows(ak, nidx))
    v_sel = jnp.where(in_past, cache_dsa_v[l, phys, off], take_rows(av, nidx))
    out_a = dsa_attend(aq, q_pos, k_sel, v_sel, idx)
    out_b, v_rows = gmlp_mix(jax.nn.gelu(bu), jax.nn.gelu(bv), w_s, b_s)
    k_all = jnp.concatenate([cache_sb_k[l, page_table].reshape(B, past, N_H, HEAD_DIM), ck], axis=1)
    v_all = jnp.concatenate([cache_sb_v[l, page_table].reshape(B, past, N_H, HEAD_DIM), cv], axis=1)
    out_c = stick_breaking(cq, q_pos, k_all, v_all, k_pos)
    out_d, S_new, buf_new = gdn_branch(dqkv, dbeta, da, dgate, state_gdn_conv[l], state_gdn_S[l],
                                       conv_w, a_log, dt_bias, gdn_g)
    mix = jnp.concatenate([out_a.reshape(B, S, W_GROUP), out_b, out_c.reshape(B, S, W_GROUP), out_d], axis=-1)
    return mix, (ak, av, ik, ck, cv, S_new, buf_new, v_rows)


def sqrelu_ffn(h, w1, w2):
    return jnp.square(jax.nn.relu(h @ w1)) @ w2


def run_layer(x, c, mix_fn, w_cond, b_cond, w_out, ln_g, ln_b, w_ff1, w_ff2):
    mod = (jax.nn.silu(c) @ w_cond + b_cond)[:, None, :]
    sh_m, sc_m, g_m, sh_f, sc_f, g_f = jnp.split(mod, 6, axis=-1)
    mix, states = mix_fn(modulate(x, sh_m, sc_m))
    x = layer_norm(DEEPNORM_ALPHA * x + g_m * (mix @ w_out), ln_g[0], ln_b[0])
    ff = sqrelu_ffn(modulate(x, sh_f, sc_f), w_ff1, w_ff2)
    x = layer_norm(DEEPNORM_ALPHA * x + g_f * ff, ln_g[1], ln_b[1])
    return x, states


def setup_inputs(seed: int = 0) -> dict:
    key = jax.random.key(seed)
    ks = iter(jax.random.split(key, 40))
    f32 = jnp.float32

    def nrm(shape, s=1.0):
        return s * jax.random.normal(next(ks), shape, f32)

    n_pages = PAST_LEN // PAGE_SIZE
    n_used = DEC_BATCH * n_pages
    n_pool = n_used + max(1, n_used // 4)
    x_prompt = nrm((BATCH, SEQ, D_MODEL))
    x_sample = nrm((DEC_BATCH, DEC_SEQ, D_MODEL))
    cache_dsa_k = nrm((DEPTH, n_pool, PAGE_SIZE, N_H, HEAD_DIM))
    cache_dsa_v = nrm((DEPTH, n_pool, PAGE_SIZE, N_H, HEAD_DIM))
    cache_dsa_kidx = nrm((DEPTH, n_pool, PAGE_SIZE, IDX_DIM))
    cache_sb_k = nrm((DEPTH, n_pool, PAGE_SIZE, N_H, HEAD_DIM))
    cache_sb_v = nrm((DEPTH, n_pool, PAGE_SIZE, N_H, HEAD_DIM))
    state_gdn_S = nrm((DEPTH, DEC_BATCH, N_H, HEAD_DIM, HEAD_DIM), 0.1)
    state_gdn_conv = nrm((DEPTH, DEC_BATCH, CONV_W - 1, 3 * W_GROUP))
    page_table = jax.random.permutation(next(ks), n_pool)[:n_used].reshape(DEC_BATCH, n_pages).astype(jnp.int32)
    c_prompt = nrm((BATCH, D_MODEL))
    c_sample = nrm((DEC_BATCH, D_MODEL))
    w_cond = nrm((DEPTH, D_MODEL, 6 * D_MODEL), 0.5 * D_MODEL ** -0.5)
    b_cond = nrm((DEPTH, 6 * D_MODEL), 0.02)
    w_in = nrm((DEPTH, D_MODEL, N_IN), D_MODEL ** -0.5)
    w_out = nrm((DEPTH, D_MIX, D_MODEL), DEEPNORM_BETA * D_MIX ** -0.5)
    ln_g = 1.0 + nrm((DEPTH, 2, D_MODEL), 0.02)
    ln_b = nrm((DEPTH, 2, D_MODEL), 0.02)
    conv_w = nrm((DEPTH, CONV_W, 3 * W_GROUP), CONV_W ** -0.5)
    a_log = jnp.log(jax.random.uniform(next(ks), (DEPTH, N_H), f32, 1.0, 16.0))
    dt = jnp.exp(jax.random.uniform(next(ks), (DEPTH, N_H), f32, math.log(1e-3), math.log(0.1)))
    dt_bias = dt + jnp.log(-jnp.expm1(-dt))
    gdn_norm_g = 1.0 + nrm((DEPTH, HEAD_DIM), 0.02)
    gmlp_w_s = nrm((DEPTH, N_H, GMLP_CHUNK, GMLP_CHUNK), 0.5 * GMLP_CHUNK ** -0.5)
    gmlp_b_s = 1.0 + nrm((DEPTH, N_H, GMLP_CHUNK), 0.02)
    w_ff1 = nrm((DEPTH, D_MODEL, D_FF), D_MODEL ** -0.5)
    w_ff2 = nrm((DEPTH, D_FF, D_MODEL), DEEPNORM_BETA * D_FF ** -0.5)
    return {
        'x_prompt': x_prompt, 'x_sample': x_sample,
        'cache_dsa_k': cache_dsa_k, 'cache_dsa_v': cache_dsa_v, 'cache_dsa_kidx': cache_dsa_kidx,
        'cache_sb_k': cache_sb_k, 'cache_sb_v': cache_sb_v,
        'state_gdn_S': state_gdn_S, 'state_gdn_conv': state_gdn_conv,
        'page_table': page_table, 'c_prompt': c_prompt, 'c_sample': c_sample,
        'w_cond': w_cond, 'b_cond': b_cond, 'w_in': w_in, 'w_out': w_out,
        'ln_g': ln_g, 'ln_b': ln_b, 'conv_w': conv_w, 'a_log': a_log, 'dt_bias': dt_bias,
        'gdn_norm_g': gdn_norm_g, 'gmlp_w_s': gmlp_w_s, 'gmlp_b_s': gmlp_b_s,
        'w_ff1': w_ff1, 'w_ff2': w_ff2,
    }


def reference(x_prompt, x_sample, cache_dsa_k, cache_dsa_v, cache_dsa_kidx, cache_sb_k, cache_sb_v,
              state_gdn_S, state_gdn_conv, page_table, c_prompt, c_sample,
              w_cond, b_cond, w_in, w_out, ln_g, ln_b, conv_w, a_log, dt_bias, gdn_norm_g,
              gmlp_w_s, gmlp_b_s, w_ff1, w_ff2):
    xp = layer_norm_plain(x_prompt)
    xs = layer_norm_plain(x_sample)
    st_p, st_s = [], []
    for l in range(DEPTH):
        shared = (w_cond[l], b_cond[l], w_out[l], ln_g[l], ln_b[l], w_ff1[l], w_ff2[l])
        mix_w = (w_in[l], conv_w[l], a_log[l], dt_bias[l], gdn_norm_g[l], gmlp_w_s[l], gmlp_b_s[l])
        xp, s_p = run_layer(xp, c_prompt, lambda h: mixers_prompt(h, *mix_w), *shared)
        st_p.append(s_p)
        xs, s_s = run_layer(xs, c_sample, lambda h: mixers_sample(
            h, *mix_w, cache_dsa_k, cache_dsa_v, cache_dsa_kidx, cache_sb_k, cache_sb_v,
            state_gdn_S, state_gdn_conv, page_table, l), *shared)
        st_s.append(s_s)
    dsa_k_p, dsa_v_p, dsa_kidx_p, sb_k_p, sb_v_p, gdn_S_p, gdn_conv_p = (jnp.stack(t) for t in zip(*st_p))
    dsa_k_s, dsa_v_s, dsa_kidx_s, sb_k_s, sb_v_s, gdn_S_s, gdn_conv_s, gmlp_v_s = (jnp.stack(t) for t in zip(*st_s))
    return (xp, xs, dsa_k_p, dsa_v_p, dsa_kidx_p, sb_k_p, sb_v_p, gdn_S_p, gdn_conv_p,
            dsa_k_s, dsa_v_s, dsa_kidx_s, sb_k_s, sb_v_s, gdn_S_s, gdn_conv_s, gmlp_v_s)
```

```python
import functools
import math

import jax
import jax.numpy as jnp
from jax import lax
from jax.experimental import pallas as pl
from jax.experimental.pallas import tpu as pltpu

F32 = jnp.float32
BF16 = jnp.bfloat16
I32 = jnp.int32

LN_EPS = 1e-5
NORM_EPS = 1e-6
N_H = 4
HEAD_DIM = 64
W_GROUP = N_H * HEAD_DIM
IDX_HEADS = 4
IDX_DIM = 64
TOPK_MAX = 256
GMLP_CHUNK = 128
GDN_CHUNK = 64
CONV_W = 4
BLK = 128
LANES = 128
VMEM_LIMIT = 56 * 1024 * 1024
SB_ZERO_TAIL = -110.0
INT_MIN = -2 ** 31

PIECES = (("dqkv", 768, 768), ("aq", 256, 256), ("ak", 256, 256), ("av", 256, 256), ("iq", 256, 256),
          ("bu", 256, 256), ("bv", 256, 256), ("cq", 256, 256), ("ck", 256, 256), ("cv", 256, 256),
          ("dgate", 256, 256), ("ik", 64, 128), ("small", 12, 128))
N_PAD = sum(p[2] for p in PIECES)
SMALL_IW, SMALL_BETA, SMALL_A = 0, 4, 8


def _cparams(sem, vmem=VMEM_LIMIT):
    return pltpu.CompilerParams(dimension_semantics=sem, vmem_limit_bytes=vmem)


def _ln_plain(x):
    mu = jnp.mean(x, axis=-1, keepdims=True)
    xc = x - mu
    var = jnp.mean(xc * xc, axis=-1, keepdims=True)
    return xc * lax.rsqrt(var + LN_EPS)


def _sigmoid(x):
    return 1.0 / (1.0 + jnp.exp(-x))


def _silu(x):
    return x * _sigmoid(x)


def _softplus(x):
    return jnp.maximum(x, 0.0) + jnp.log1p(jnp.exp(-jnp.abs(x)))


def _gelu_tanh(x):
    c = math.sqrt(2.0 / math.pi)
    return x * (0.5 * (1.0 + jnp.tanh(c * (x + 0.044715 * (x * x * x)))))


def _dot(a, b):
    return jnp.dot(a.astype(BF16), b.astype(BF16), preferred_element_type=F32)


def _dot_nt(a, b):
    return lax.dot_general(a.astype(BF16), b.astype(BF16), (((1,), (1,)), ((), ())),
                           preferred_element_type=F32)


def _dot_tn(a, b):
    return lax.dot_general(a.astype(BF16), b.astype(BF16), (((0,), (0,)), ((), ())),
                           preferred_element_type=F32)


def _split(x):
    hi = x.astype(BF16)
    lo = (x - hi.astype(F32)).astype(BF16)
    return hi, lo


def _dot_precise_lhs(a, b01):
    hi, lo = _split(a)
    return jnp.dot(hi, b01, preferred_element_type=F32) + jnp.dot(lo, b01, preferred_element_type=F32)


def _dot_precise_rhs(a01, b):
    hi, lo = _split(b)
    return jnp.dot(a01, hi, preferred_element_type=F32) + jnp.dot(a01, lo, preferred_element_type=F32)


def _dot3(a, b):
    ah, al = _split(a)
    bh, bl = _split(b)
    return (jnp.dot(ah, bh, preferred_element_type=F32) + jnp.dot(al, bh, preferred_element_type=F32)
            + jnp.dot(ah, bl, preferred_element_type=F32))


def _iota(shape, dim):
    return lax.broadcasted_iota(I32, shape, dim)


def _sort_key(score):
    bits = pltpu.bitcast(score, I32)
    return bits ^ ((bits >> 31) & jnp.int32(0x7FFFFFFF))


def _alibi_slope(h):
    return 2.0 ** (-8.0 * (h + 1) / N_H)


def _cond_kernel(c_ref, w_ref, b_ref, o_ref):
    s = _silu(c_ref[...])
    o_ref[0] = _dot3(s, w_ref[0]) + b_ref[0]


def _cond(c_all, w_cond, b_cond):
    depth, d, n6 = w_cond.shape
    rc = c_all.shape[0]
    tn = 1536 if n6 % 1536 == 0 else n6
    return pl.pallas_call(
        _cond_kernel,
        out_shape=jax.ShapeDtypeStruct((depth, rc, n6), F32),
        grid=(depth, n6 // tn),
        in_specs=[pl.BlockSpec((rc, d), lambda l, j: (0, 0)),
                  pl.BlockSpec((1, d, tn), lambda l, j: (l, 0, j)),
                  pl.BlockSpec((1, 1, tn), lambda l, j: (l, 0, j))],
        out_specs=pl.BlockSpec((1, rc, tn), lambda l, j: (l, 0, j)),
        compiler_params=_cparams(("arbitrary", "arbitrary")),
        name="cond",
    )(c_all, w_cond, b_cond.reshape(depth, 1, n6))


def _proj_kernel(x_ref, sh_ref, sc_ref, w_ref, *o_refs, pre_ln):
    x = x_ref[...]
    if pre_ln:
        x = _ln_plain(x)
    hb = (x * (1.0 + sc_ref[...]) + sh_ref[...]).astype(BF16)
    col = 0
    for (name, width, padded), o_ref in zip(PIECES, o_refs[:len(PIECES)]):
        res = jnp.dot(hb, w_ref[:, col:col + padded], preferred_element_type=F32)
        o_ref[...] = res if o_ref.shape[-1] == padded else res[:, :o_ref.shape[-1]]
        if name in ("ak", "av", "ck", "cv"):
            o_refs[len(PIECES) + ("ak", "av", "ck", "cv").index(name)][...] = res.astype(BF16)
        if name == "ik":
            o_refs[len(PIECES) + 4][...] = res.astype(BF16)
        col += padded


def _mod_spec(m, tm, d):
    if m.shape[0] == 1:
        return pl.BlockSpec((1, d), lambda i: (0, 0))
    return pl.BlockSpec((tm, d), lambda i: (i, 0))


def _proj(x, shift, scale, w_pad, pre_ln, tm):
    r, d = x.shape
    out_shapes, out_specs = [], []
    for name, width, padded in PIECES:
        w_out = width if name == "ik" else padded
        out_shapes.append(jax.ShapeDtypeStruct((r, w_out), F32))
        out_specs.append(pl.BlockSpec((tm, w_out), lambda i: (i, 0)))
    for name in ("ak", "av", "ck", "cv"):
        out_shapes.append(jax.ShapeDtypeStruct((r, 256), BF16))
        out_specs.append(pl.BlockSpec((tm, 256), lambda i: (i, 0)))
    out_shapes.append(jax.ShapeDtypeStruct((r, 128), BF16))
    out_specs.append(pl.BlockSpec((tm, 128), lambda i: (i, 0)))
    outs = pl.pallas_call(
        functools.partial(_proj_kernel, pre_ln=pre_ln),
        out_shape=out_shapes,
        grid=(r // tm,),
        in_specs=[pl.BlockSpec((tm, d), lambda i: (i, 0)), _mod_spec(shift, tm, d), _mod_spec(scale, tm, d),
                  pl.BlockSpec((d, N_PAD), lambda i: (0, 0))],
        out_specs=out_specs,
        compiler_params=_cparams(("parallel",)),
        name="proj",
    )(x, shift, scale, w_pad)
    res = {name: o for (name, _, _), o in zip(PIECES, outs)}
    for i, name in enumerate(("ak", "av", "ck", "cv", "ik")):
        res[name + "_b"] = outs[len(PIECES) + i]
    return res


def _pad_w_in(w_in_l):
    off = {}
    start = 0
    for name, n in (("aq", 256), ("ak", 256), ("av", 256), ("iq", 256), ("ik", 64), ("iw", 4), ("bu", 256),
                    ("bv", 256), ("cq", 256), ("ck", 256), ("cv", 256), ("dqkv", 768), ("dbeta", 4), ("da", 4),
                    ("dgate", 256)):
        off[name] = (start, n)
        start += n
    d = w_in_l.shape[0]
    cols = []
    for name, width, padded in PIECES:
        if name == "small":
            parts = [w_in_l[:, off[k][0]:off[k][0] + off[k][1]] for k in ("iw", "dbeta", "da")]
            piece = jnp.concatenate(parts, axis=1)
        else:
            piece = w_in_l[:, off[name][0]:off[name][0] + off[name][1]]
        if padded > piece.shape[1]:
            piece = jnp.concatenate([piece, jnp.zeros((d, padded - piece.shape[1]), piece.dtype)], axis=1)
        cols.append(piece)
    return jnp.concatenate(cols, axis=1).astype(BF16)


def _outffn_kernel(x_ref, oa_ref, ob_ref, oc_ref, od_ref, gm_ref, shf_ref, scf_ref, gf_ref,
                   wout_ref, lng_ref, lnb_ref, w1_ref, w2_ref, o_ref, *, pre_ln, alpha):
    x = x_ref[...]
    if pre_ln:
        x = _ln_plain(x)
    mo = None
    for g, r in enumerate((oa_ref, ob_ref, oc_ref, od_ref)):
        part = _dot(r[...], wout_ref[g * W_GROUP:(g + 1) * W_GROUP, :])
        mo = part if mo is None else mo + part
    x1 = _ln_plain(alpha * x + gm_ref[...] * mo) * lng_ref[0:1, :] + lnb_ref[0:1, :]
    hf = x1 * (1.0 + scf_ref[...]) + shf_ref[...]
    a = jnp.maximum(_dot(hf, w1_ref[...]), 0.0)
    ff = _dot(a * a, w2_ref[...])
    o_ref[...] = _ln_plain(alpha * x1 + gf_ref[...] * ff) * lng_ref[1:2, :] + lnb_ref[1:2, :]


def _outffn(x, branches, gm, shf, scf, gf, wout_b, ln_g, ln_b, w1_b, w2_b, pre_ln, alpha, tm):
    r, d = x.shape
    dff = w1_b.shape[1]
    row = lambda w: pl.BlockSpec((tm, w), lambda i: (i, 0))
    const = lambda shp: pl.BlockSpec(shp, lambda i: (0, 0), pipeline_mode=pl.Buffered(1))
    return pl.pallas_call(
        functools.partial(_outffn_kernel, pre_ln=pre_ln, alpha=alpha),
        out_shape=jax.ShapeDtypeStruct((r, d), F32),
        grid=(r // tm,),
        in_specs=[row(d)] + [row(W_GROUP)] * 4 + [_mod_spec(m, tm, d) for m in (gm, shf, scf, gf)]
        + [const((d, d)), const((2, d)), const((2, d)), const((d, dff)), const((dff, d))],
        out_specs=row(d),
        compiler_params=_cparams(("parallel",)),
        name="outffn",
    )(x, *branches, gm, shf, scf, gf, wout_b, ln_g, ln_b, w1_b, w2_b)


def _gmlp_kernel(u_ref, v_ref, w_ref, bt_ref, ob_ref, *vn_refs, chunk):
    u = _gelu_tanh(u_ref[...])
    vn = _ln_plain(_gelu_tanh(v_ref[...]))
    if vn_refs:
        vn_refs[0][...] = vn
    rows = u.shape[0]
    ri, ci = _iota((rows, rows), 0), _iota((rows, rows), 1)
    mask = ci <= ri
    if rows != chunk:
        sh = int(math.log2(chunk))
        mask = mask & ((ri >> sh) == (ci >> sh))
    vb = vn.astype(BF16)
    for g in range(N_H):
        w = jnp.where(mask, w_ref[g], 0.0).astype(BF16)
        mixed = jnp.dot(w, vb[:, g * HEAD_DIM:(g + 1) * HEAD_DIM], preferred_element_type=F32)
        mixed = mixed + bt_ref[:, g:g + 1]
        ob_ref[:, g * HEAD_DIM:(g + 1) * HEAD_DIM] = u[:, g * HEAD_DIM:(g + 1) * HEAD_DIM] * mixed


def _gmlp(bu, bv, w_tiled, bt_tiled, rows, chunk, want_vn):
    r = bu.shape[0]
    row = pl.BlockSpec((rows, W_GROUP), lambda i: (i, 0))
    out_shape = [jax.ShapeDtypeStruct((r, W_GROUP), F32)]
    out_specs = [row]
    if want_vn:
        out_shape.append(jax.ShapeDtypeStruct((r, W_GROUP), F32))
        out_specs.append(row)
    outs = pl.pallas_call(
        functools.partial(_gmlp_kernel, chunk=chunk),
        out_shape=out_shape,
        grid=(r // rows,),
        in_specs=[row, row, pl.BlockSpec((N_H, rows, rows), lambda i: (0, 0, 0)),
                  pl.BlockSpec((rows, N_H), lambda i: (0, 0))],
        out_specs=out_specs,
        compiler_params=_cparams(("parallel",)),
        name="gmlp",
    )(bu, bv, w_tiled, bt_tiled)
    return outs if want_vn else (outs[0], None)


def _sb_block(qb16, k, v, vis, carry, acc, upper):
    new_carry, new_acc = [], []
    for h in range(N_H):
        sl = slice(h * HEAD_DIM, (h + 1) * HEAD_DIM)
        z = _dot_nt(qb16[:, sl], k[h])
        l_raw = -_softplus(z)
        l_vis = l_raw if vis is None else jnp.where(vis, l_raw, 0.0)
        tail = _dot_precise_lhs(l_vis, upper) + carry[h]
        w = jnp.exp(z + l_raw + tail)
        if vis is not None:
            w = jnp.where(vis, w, 0.0)
        new_acc.append(acc[h] + _dot(w, v[h]))
        new_carry.append(carry[h] + jnp.sum(l_vis, axis=1, keepdims=True))
    return new_carry, new_acc


def _sb_live(carry):
    m = carry[0]
    for c in carry[1:]:
        m = jnp.maximum(m, c)
    return (jnp.max(m) > SB_ZERO_TAIL).astype(I32)


def _heads(x):
    return [x[:, h * HEAD_DIM:(h + 1) * HEAD_DIM] for h in range(N_H)]


def _upper_ones():
    return (_iota((BLK, BLK), 0) > _iota((BLK, BLK), 1)).astype(BF16)


def _sbp_kernel(q_ref, k_ref, v_ref, o_ref):
    qb = pl.program_id(1)
    q = (q_ref[...] * (HEAD_DIM ** -0.5)).astype(BF16)
    upper = _upper_ones()
    ri, ci = _iota((BLK, BLK), 0), _iota((BLK, BLK), 1)
    zero_c = [jnp.zeros((BLK, 1), F32)] * N_H
    zero_a = [jnp.zeros((BLK, HEAD_DIM), F32)] * N_H
    off = pl.multiple_of(qb * BLK, BLK)
    carry, acc = _sb_block(q, _heads(k_ref[pl.ds(off, BLK), :]), _heads(v_ref[pl.ds(off, BLK), :]), ci < ri,
                           zero_c, zero_a, upper)

    def cond(st):
        return jnp.logical_and(st[0] >= 0, st[1] > 0)

    def body(st):
        kb = st[0]
        o = pl.multiple_of(kb * BLK, BLK)
        c, a = _sb_block(q, _heads(k_ref[pl.ds(o, BLK), :]), _heads(v_ref[pl.ds(o, BLK), :]), None,
                         list(st[2:2 + N_H]), list(st[2 + N_H:]), upper)
        return (kb - 1, _sb_live(c), *c, *a)

    st = lax.while_loop(cond, body, (qb - 1, _sb_live(carry), *carry, *acc))
    for h in range(N_H):
        o_ref[:, h * HEAD_DIM:(h + 1) * HEAD_DIM] = st[2 + N_H + h]


def _sb_prompt(cq, ck_b, cv_b, batch, seq):
    nq = seq // BLK
    full = pl.BlockSpec((seq, W_GROUP), lambda b, i: (b, 0))
    return pl.pallas_call(
        _sbp_kernel,
        out_shape=jax.ShapeDtypeStruct(cq.shape, F32),
        grid=(batch, nq),
        in_specs=[pl.BlockSpec((BLK, W_GROUP), lambda b, i: (b * nq + i, 0)), full, full],
        out_specs=pl.BlockSpec((BLK, W_GROUP), lambda b, i: (b * nq + i, 0)),
        compiler_params=_cparams(("parallel", "arbitrary")),
        name="sb_prompt",
    )(cq, ck_b, cv_b)


def _sbs_kernel(pt_ref, q_ref, kn_ref, vn_ref, kc_ref, vc_ref, o_ref, kbuf, vbuf, sem, *, layer, n_pages):
    b = pl.program_id(0)
    rows = q_ref.shape[0]
    q = (q_ref[...] * (HEAD_DIM ** -0.5)).astype(BF16)
    upper = _upper_ones()
    ri, ci = _iota((rows, BLK), 0), _iota((rows, BLK), 1)
    zero_c = [jnp.zeros((rows, 1), F32)] * N_H
    zero_a = [jnp.zeros((rows, HEAD_DIM), F32)] * N_H
    carry, acc = _sb_block(q, _heads(kn_ref[0]), _heads(vn_ref[0]), ci < ri, zero_c, zero_a, upper)

    def copies(p):
        page = pt_ref[b, p]
        return (pltpu.make_async_copy(kc_ref.at[layer, page], kbuf, sem.at[0]),
                pltpu.make_async_copy(vc_ref.at[layer, page], vbuf, sem.at[1]))

    def cond(st):
        return jnp.logical_and(st[0] >= 0, st[1] > 0)

    def body(st):
        p = st[0]
        ck, cv = copies(p)
        ck.start()
        cv.start()
        ck.wait()
        cv.wait()
        c, a = _sb_block(q, [kbuf[:, h, :] for h in range(N_H)], [vbuf[:, h, :] for h in range(N_H)], None,
                         list(st[2:2 + N_H]), list(st[2 + N_H:]), upper)
        return (p - 1, _sb_live(c), *c, *a)

    st = lax.while_loop(cond, body, (jnp.int32(n_pages - 1), _sb_live(carry), *carry, *acc))
    for h in range(N_H):
        o_ref[:, h * HEAD_DIM:(h + 1) * HEAD_DIM] = st[2 + N_H + h]


def _sb_sample(page_table, cq, ck_new_pad, cv_new_pad, cache_k, cache_v, layer, bs, s):
    n_pages = page_table.shape[1]
    page = cache_k.shape[2]
    return pl.pallas_call(
        functools.partial(_sbs_kernel, layer=layer, n_pages=n_pages),
        out_shape=jax.ShapeDtypeStruct(cq.shape, F32),
        grid_spec=pltpu.PrefetchScalarGridSpec(
            num_scalar_prefetch=1,
            grid=(bs,),
            in_specs=[pl.BlockSpec((s, W_GROUP), lambda b, pt: (b, 0)),
                      pl.BlockSpec((1, BLK, W_GROUP), lambda b, pt: (b, 0, 0)),
                      pl.BlockSpec((1, BLK, W_GROUP), lambda b, pt: (b, 0, 0)),
                      pl.BlockSpec(memory_space=pl.ANY), pl.BlockSpec(memory_space=pl.ANY)],
            out_specs=pl.BlockSpec((s, W_GROUP), lambda b, pt: (b, 0)),
            scratch_shapes=[pltpu.VMEM((page, N_H, HEAD_DIM), F32), pltpu.VMEM((page, N_H, HEAD_DIM), F32),
                            pltpu.SemaphoreType.DMA((2,))]),
        compiler_params=_cparams(("arbitrary",)),
        name="sb_sample",
    )(page_table, cq, ck_new_pad, cv_new_pad, cache_k, cache_v)


def _kth_largest_key(count_ge, shape, k):
    kf = jnp.float32(k)
    t = jnp.where(count_ge(jnp.zeros(shape, I32)) >= kf, jnp.int32(0), jnp.int32(INT_MIN))

    def body(i, t):
        cand = t + (jnp.int32(1) << (30 - i))
        return jnp.where(count_ge(cand) >= kf, cand, t)

    return lax.fori_loop(0, 31, body, t)


def _dsap_kernel(aq_ref, iq_ref, sm_ref, ik_ref, k_ref, v_ref, o_ref, key_ref, *, topk):
    qb = pl.program_id(1)
    nkb = qb + 1
    ri, ci = _iota((BLK, BLK), 0), _iota((BLK, BLK), 1)
    iqb = (iq_ref[...] * (IDX_DIM ** -0.5)).astype(BF16)
    iw = sm_ref[...][:, SMALL_IW:SMALL_IW + IDX_HEADS] * (IDX_HEADS ** -0.5)

    def score_block(kb, _):
        o = pl.multiple_of(kb * BLK, BLK)
        kib = ik_ref[pl.ds(o, BLK), :][:, :IDX_DIM]
        score = jnp.zeros((BLK, BLK), F32)
        for h in range(IDX_HEADS):
            s = _dot_nt(iqb[:, h * IDX_DIM:(h + 1) * IDX_DIM], kib)
            score = score + iw[:, h:h + 1] * jnp.maximum(s, 0.0)
        score = jnp.where(jnp.logical_or(kb < qb, ci <= ri), score, -jnp.inf)
        key_ref[kb] = _sort_key(score)
        return 0

    lax.fori_loop(0, nkb, score_block, 0)

    def count(pred):
        def body(kb, cnt):
            return cnt + jnp.where(pred(key_ref[kb]), 1, 0)
        cnt = lax.fori_loop(0, nkb, body, jnp.zeros((BLK, BLK), I32))
        return jnp.sum(cnt.astype(F32), axis=1, keepdims=True)

    thr = _kth_largest_key(lambda cand: count(lambda key: key >= cand), (BLK, 1), topk)
    n_gt = count(lambda key: key > thr)
    n_eq = count(lambda key: key == thr)
    room = jnp.float32(topk) - n_gt
    tie_break = jnp.max(n_eq - room) > 0.0

    q = (aq_ref[...] * (HEAD_DIM ** -0.5)).astype(BF16)
    lower = (_iota((BLK, BLK), 0) < _iota((BLK, BLK), 1)).astype(BF16)
    neg = jnp.float32(-1e30)

    def attend(kb, st):
        seen = st[0]
        o = pl.multiple_of(kb * BLK, BLK)
        key = key_ref[kb]
        causal = jnp.logical_or(kb < qb, ci <= ri)

        def with_ties(_):
            eq = key == thr
            rank = seen + jnp.dot(eq.astype(BF16), lower, preferred_element_type=F32)
            sel = jnp.logical_or(key > thr, jnp.logical_and(eq, rank < room))
            return sel.astype(F32), seen + jnp.sum(eq.astype(F32), axis=1, keepdims=True)

        def no_ties(_):
            return (key >= thr).astype(F32), seen

        self_f, seen = lax.cond(tie_break, with_ties, no_ties, 0)
        sel = jnp.logical_and(self_f > 0.5, causal)
        dist = ((qb - kb) * BLK + ri - ci).astype(F32)
        kblk, vblk = k_ref[pl.ds(o, BLK), :], v_ref[pl.ds(o, BLK), :]
        out = [seen]
        for h in range(N_H):
            sl = slice(h * HEAD_DIM, (h + 1) * HEAD_DIM)
            m_old, l_old, a_old = st[1 + 3 * h], st[2 + 3 * h], st[3 + 3 * h]
            s = _dot_nt(q[:, sl], kblk[:, sl]) - _alibi_slope(h) * dist
            s = jnp.where(sel, s, neg)
            m_new = jnp.maximum(m_old, jnp.max(s, axis=1, keepdims=True))
            p = jnp.where(sel, jnp.exp(s - m_new), 0.0)
            alpha = jnp.exp(m_old - m_new)
            out += [m_new, alpha * l_old + jnp.sum(p, axis=1, keepdims=True),
                    alpha * a_old + jnp.dot(p.astype(BF16), vblk[:, sl], preferred_element_type=F32)]
        return tuple(out)

    init = [jnp.zeros((BLK, 1), F32)]
    for h in range(N_H):
        init += [jnp.full((BLK, 1), neg, F32), jnp.zeros((BLK, 1), F32), jnp.zeros((BLK, HEAD_DIM), F32)]
    st = lax.fori_loop(0, nkb, attend, tuple(init))
    for h in range(N_H):
        o_ref[:, h * HEAD_DIM:(h + 1) * HEAD_DIM] = st[3 + 3 * h] / st[2 + 3 * h]


def _dsa_prompt(aq, iq, small, ik_b, ak_b, av_b, batch, seq):
    nq = seq // BLK
    topk = min(TOPK_MAX, seq // 4)
    blk = lambda w: pl.BlockSpec((BLK, w), lambda b, i: (b * nq + i, 0))
    full = lambda w: pl.BlockSpec((seq, w), lambda b, i: (b, 0), pipeline_mode=pl.Buffered(1))
    return pl.pallas_call(
        functools.partial(_dsap_kernel, topk=topk),
        out_shape=jax.ShapeDtypeStruct(aq.shape, F32),
        grid=(batch, nq),
        in_specs=[blk(W_GROUP), blk(W_GROUP), blk(LANES), full(LANES), full(W_GROUP), full(W_GROUP)],
        out_specs=blk(W_GROUP),
        scratch_shapes=[pltpu.VMEM((nq, BLK, BLK), I32)],
        compiler_params=_cparams(("parallel", "arbitrary")),
        name="dsa_prompt",
    )(aq, iq, small, ik_b, ak_b, av_b)


def _dsas_kernel(pt_ref, aq_ref, iq_ref, sm_ref, ikn_ref, akn_ref, avn_ref, ic_ref, kc_ref, vc_ref, o_ref,
                 key_ref, fold_ref, ibuf, kbuf, vbuf, sem, *, layer, n_pages, topk):
    b = pl.program_id(0)
    s_rows = aq_ref.shape[0]
    reps = LANES // s_rows
    past = n_pages * BLK
    sh_s = int(math.log2(s_rows))
    lane = _iota((1, LANES), 1)
    lane_q = lane & (s_rows - 1)
    lane_h = (lane >> sh_s) & (N_H - 1)
    ri, ci = _iota((BLK, LANES), 0), _iota((BLK, LANES), 1)
    cq_full = ci & (s_rows - 1)

    def tile_rows(x):
        return jnp.concatenate([x] * reps, axis=0)

    iq = iq_ref[...] * (IDX_DIM ** -0.5)
    iq_heads = jnp.concatenate([iq[:, h * IDX_DIM:(h + 1) * IDX_DIM] for h in range(IDX_HEADS)], axis=0)
    iq_t = jnp.concatenate([iq_heads] * (LANES // (s_rows * IDX_HEADS)), axis=0).astype(BF16)
    iw = sm_ref[...][:, SMALL_IW:SMALL_IW + IDX_HEADS] * (IDX_HEADS ** -0.5)
    wq = jnp.zeros((s_rows, LANES), F32)
    for h in range(IDX_HEADS):
        wq = wq + jnp.where(lane_h == h, iw[:, h:h + 1], 0.0)
    w_lane = jnp.sum(jnp.where(_iota((s_rows, LANES), 0) == lane_q, wq, 0.0), axis=0, keepdims=True)

    def idx_copy(p, slot):
        return pltpu.make_async_copy(ic_ref.at[layer, pt_ref[b, p]], ibuf.at[slot], sem.at[0, slot])

    def kv_copies(p, slot):
        page = pt_ref[b, p]
        return (pltpu.make_async_copy(kc_ref.at[layer, page], kbuf.at[slot], sem.at[1, slot]),
                pltpu.make_async_copy(vc_ref.at[layer, page], vbuf.at[slot], sem.at[2, slot]))

    group_sel = (ri >> sh_s) == (ci >> sh_s)

    def score_block(p, kidx, valid):
        s = _dot_nt(kidx, iq_t)
        x = jnp.maximum(s, 0.0) * w_lane
        x = x + pltpu.roll(x, 2 * s_rows, 1)
        x = x + pltpu.roll(x, s_rows, 1)
        if valid is not None:
            x = jnp.where(valid, x, -jnp.inf)
        key = _sort_key(x)
        key_ref[p] = key
        picked = jnp.where(group_sel, key, 0)
        folded = picked[0:s_rows]
        for g in range(1, BLK // s_rows):
            folded = folded + picked[g * s_rows:(g + 1) * s_rows]
        fold_ref[p] = folded

    idx_copy(0, 0).start()

    def p1(p, _):
        slot = p % 2

        @pl.when(p + 1 < n_pages)
        def _():
            idx_copy(p + 1, 1 - slot).start()

        idx_copy(p, slot).wait()
        score_block(p, ibuf[slot], None)
        return 0

    lax.fori_loop(0, n_pages, p1, 0)
    new_valid = jnp.logical_and(ri <= cq_full, ri < s_rows)
    score_block(n_pages, ikn_ref[0], new_valid)

    def lane_total(cnt):
        tot = jnp.sum(cnt.astype(F32), axis=0, keepdims=True)
        sh = s_rows
        while sh < LANES:
            tot = tot + pltpu.roll(tot, sh, 1)
            sh *= 2
        return tot

    def count(pred):
        def body(p, cnt):
            return cnt + jnp.where(pred(fold_ref[p]), 1, 0)
        return lane_total(lax.fori_loop(0, n_pages + 1, body, jnp.zeros((s_rows, LANES), I32)))

    thr = _kth_largest_key(lambda cand: count(lambda key: key >= cand), (1, LANES), topk)
    n_gt = count(lambda key: key > thr)
    n_eq = count(lambda key: key == thr)
    room = jnp.float32(topk) - n_gt
    tie_break = jnp.max(n_eq - room) > 0.0

    q_heads = [tile_rows(x).astype(BF16) for x in _heads(aq_ref[...] * (HEAD_DIM ** -0.5))]
    slope = jnp.zeros((1, LANES), F32)
    for h in range(N_H):
        slope = jnp.where(lane_h == h, _alibi_slope(h), slope)
    strict_lower = (_iota((BLK, BLK), 1) < _iota((BLK, BLK), 0)).astype(BF16)
    neg = jnp.float32(-1e30)

    def attend_block(p, kpage, vpage, valid, kpos0, st):
        seen, m_old, l_old, acc = st
        key = key_ref[p]

        def with_ties(_):
            eq = key == thr
            rank = seen + jnp.dot(strict_lower, eq.astype(BF16), preferred_element_type=F32)
            sel = jnp.logical_or(key > thr, jnp.logical_and(eq, rank < room))
            return sel.astype(F32), seen + jnp.sum(eq.astype(F32), axis=0, keepdims=True)

        def no_ties(_):
            return (key >= thr).astype(F32), seen

        sel_f, seen = lax.cond(tie_break, with_ties, no_ties, 0)
        sel = sel_f > 0.5
        if valid is not None:
            sel = jnp.logical_and(sel, valid)
        dist = ((past - kpos0) + cq_full - ri).astype(F32)
        s = _dot_nt(kpage[0], q_heads[0])
        for h in range(1, N_H):
            s = jnp.where(lane_h == h, _dot_nt(kpage[h], q_heads[h]), s)
        s = jnp.where(sel, s - slope * dist, neg)
        m_new = jnp.maximum(m_old, jnp.max(s, axis=0, keepdims=True))
        pr = jnp.where(sel, jnp.exp(s - m_new), 0.0)
        alpha = jnp.exp(m_old - m_new)
        l_new = alpha * l_old + jnp.sum(pr, axis=0, keepdims=True)
        pv = jnp.concatenate([_dot_tn(vpage[h], pr) for h in range(N_H)], axis=0)
        return seen, m_new, l_new, alpha * acc + pv

    for c in kv_copies(0, 0):
        c.start()

    def p3(p, st):
        slot = p % 2

        @pl.when(p + 1 < n_pages)
        def _():
            for c in kv_copies(p + 1, 1 - slot):
                c.start()

        for c in kv_copies(p, slot):
            c.wait()
        return attend_block(p, [kbuf[slot, :, h, :] for h in range(N_H)],
                            [vbuf[slot, :, h, :] for h in range(N_H)], None, p * BLK, st)

    st0 = (jnp.zeros((1, LANES), F32), jnp.full((1, LANES), neg, F32), jnp.zeros((1, LANES), F32),
           jnp.zeros((W_GROUP, LANES), F32))
    st = lax.fori_loop(0, n_pages, p3, st0)
    _, _, l_fin, acc = attend_block(n_pages, _heads(akn_ref[0]), _heads(avn_ref[0]), new_valid, past, st)
    out_t = (acc / l_fin).T
    for h in range(N_H):
        o_ref[:, h * HEAD_DIM:(h + 1) * HEAD_DIM] = out_t[h * s_rows:(h + 1) * s_rows, h * HEAD_DIM:(h + 1) * HEAD_DIM]


def _dsa_sample(page_table, aq, iq, small, ik_new_pad, ak_new_pad, av_new_pad, cache_i, cache_k, cache_v,
                layer, bs, s):
    n_pages = page_table.shape[1]
    page = cache_k.shape[2]
    topk = min(TOPK_MAX, (n_pages * page + s) // 4)
    row = lambda w: pl.BlockSpec((s, w), lambda b, pt: (b, 0))
    new = lambda w: pl.BlockSpec((1, BLK, w), lambda b, pt: (b, 0, 0))
    hbm = pl.BlockSpec(memory_space=pl.ANY)
    return pl.pallas_call(
        functools.partial(_dsas_kernel, layer=layer, n_pages=n_pages, topk=topk),
        out_shape=jax.ShapeDtypeStruct(aq.shape, F32),
        grid_spec=pltpu.PrefetchScalarGridSpec(
            num_scalar_prefetch=1,
            grid=(bs,),
            in_specs=[row(W_GROUP), row(W_GROUP), row(LANES), new(IDX_DIM), new(W_GROUP), new(W_GROUP),
                      hbm, hbm, hbm],
            out_specs=row(W_GROUP),
            scratch_shapes=[pltpu.VMEM((n_pages + 1, BLK, LANES), I32), pltpu.VMEM((n_pages + 1, s, LANES), I32),
                            pltpu.VMEM((2, page, IDX_DIM), F32), pltpu.VMEM((2, page, N_H, HEAD_DIM), F32),
                            pltpu.VMEM((2, page, N_H, HEAD_DIM), F32), pltpu.SemaphoreType.DMA((3, 2))]),
        compiler_params=_cparams(("arbitrary",)),
        name="dsa_sample",
    )(page_table, aq, iq, small, ik_new_pad, ak_new_pad, av_new_pad, cache_i, cache_k, cache_v)


def _gdn_kernel(x_ref, sm_ref, gate_ref, tail0_ref, s0_ref, cw_ref, alog_ref, dtb_ref, ng_ref,
                o_ref, sout_ref, s_scr, tail_scr, *, chunk):
    step = pl.program_id(1)
    rb = x_ref.shape[0]
    n_chunks = rb // chunk

    @pl.when(step == 0)
    def _():
        s_scr[...] = s0_ref[0]
        tail_scr[...] = tail0_ref[0]

    x = x_ref[...]
    xfull = jnp.concatenate([tail_scr[...], x], axis=0)
    y = xfull[8:8 + rb] * cw_ref[CONV_W - 1:CONV_W, :]
    for j in range(CONV_W - 1):
        y = y + pltpu.roll(xfull, CONV_W - 1 - j, 0)[8:8 + rb] * cw_ref[j:j + 1, :]
    tail_scr[...] = xfull[rb:rb + 8]
    c = _silu(y)

    sm = sm_ref[...]
    beta_all = _sigmoid(sm)
    g_all = -jnp.exp(alog_ref[...]) * _softplus(sm + dtb_ref[...])
    rbp = ((rb + LANES - 1) // LANES) * LANES
    g_pad = g_all if rbp == rb else jnp.concatenate([g_all, jnp.zeros((rbp - rb, LANES), F32)], axis=0)
    g_t = g_pad.T

    ri, ci = _iota((chunk, chunk), 0), _iota((chunk, chunk), 1)
    incl, strict = ci <= ri, ci < ri
    lower_incl = incl.astype(BF16)
    upper_incl = (ri <= ci).astype(BF16)
    eye = (ri == ci).astype(F32)
    gate = gate_ref[...]
    n_sq = int(math.log2(chunk))
    scale = HEAD_DIM ** -0.5

    for cidx in range(n_chunks):
        r0 = cidx * chunk
        rows = slice(r0, r0 + chunk)
        gcol_all = _dot_precise_rhs(lower_incl, g_all[rows])
        grow_all = _dot_precise_lhs(g_t[:, rows], upper_incl)
        for h in range(N_H):
            hs = slice(h * HEAD_DIM, (h + 1) * HEAD_DIM)
            qh = c[rows, h * HEAD_DIM:(h + 1) * HEAD_DIM]
            kh = c[rows, W_GROUP + h * HEAD_DIM:W_GROUP + (h + 1) * HEAD_DIM]
            vh = c[rows, 2 * W_GROUP + h * HEAD_DIM:2 * W_GROUP + (h + 1) * HEAD_DIM]
            qh = qh * lax.rsqrt(jnp.sum(qh * qh, axis=1, keepdims=True) + NORM_EPS) * scale
            kh = kh * lax.rsqrt(jnp.sum(kh * kh, axis=1, keepdims=True) + NORM_EPS)
            bh = beta_all[rows, SMALL_BETA + h:SMALL_BETA + h + 1]
            gcol = gcol_all[:, SMALL_A + h:SMALL_A + h + 1]
            grow = grow_all[SMALL_A + h:SMALL_A + h + 1, :]
            decay = jnp.where(incl, jnp.exp(jnp.where(incl, gcol - grow, 0.0)), 0.0)
            kb = kh * bh
            m = jnp.where(strict, _dot_nt(kb, kh) * decay, 0.0)
            pw = -m
            t = eye + pw
            for _ in range(n_sq - 1):
                pw = _dot(pw, pw)
                t = t + _dot(t, pw)
            u = _dot(t, vh * bh)
            w = _dot(t, kb * jnp.exp(gcol))
            attn = jnp.where(incl, _dot_nt(qh, kh) * decay, 0.0)
            s_h = s_scr[h]
            vnew = u - _dot(w, s_h)
            o = _dot(qh * jnp.exp(gcol), s_h) + _dot(attn, vnew)
            glast = gcol[chunk - 1:chunk, :]
            s_scr[h] = s_h * jnp.exp(glast) + _dot_tn(kh * jnp.exp(glast - gcol), vnew)
            o = o * lax.rsqrt(jnp.mean(o * o, axis=1, keepdims=True) + NORM_EPS) * ng_ref[:, hs]
            o_ref[rows, hs] = o * _silu(gate[rows, hs])

    @pl.when(step == pl.num_programs(1) - 1)
    def _():
        sout_ref[0] = s_scr[...]


def _gdn(dqkv, small, dgate, tail0, s0, conv_w, alog_row, dtb_row, ng_row, batch, seq, chunk, rb):
    steps = seq // rb
    row = lambda w: pl.BlockSpec((rb, w), lambda b, i: (b * steps + i, 0))
    const = lambda shp: pl.BlockSpec(shp, lambda b, i: (0,) * len(shp))
    return pl.pallas_call(
        functools.partial(_gdn_kernel, chunk=chunk),
        out_shape=[jax.ShapeDtypeStruct((dqkv.shape[0], W_GROUP), F32),
                   jax.ShapeDtypeStruct((batch, N_H, HEAD_DIM, HEAD_DIM), F32)],
        grid=(batch, steps),
        in_specs=[row(3 * W_GROUP), row(LANES), row(W_GROUP),
                  pl.BlockSpec((1, 8, 3 * W_GROUP), lambda b, i: (b, 0, 0)),
                  pl.BlockSpec((1, N_H, HEAD_DIM, HEAD_DIM), lambda b, i: (b, 0, 0, 0)),
                  const((CONV_W, 3 * W_GROUP)), const((1, LANES)), const((1, LANES)), const((1, W_GROUP))],
        out_specs=[row(W_GROUP), pl.BlockSpec((1, N_H, HEAD_DIM, HEAD_DIM), lambda b, i: (b, 0, 0, 0))],
        scratch_shapes=[pltpu.VMEM((N_H, HEAD_DIM, HEAD_DIM), F32), pltpu.VMEM((8, 3 * W_GROUP), F32)],
        compiler_params=_cparams(("parallel", "arbitrary")),
        name="gdn",
    )(dqkv, small, dgate, tail0, s0, conv_w, alog_row, dtb_row, ng_row)


def _lane_row(vec, offset):
    return jnp.zeros((1, LANES), F32).at[0, offset:offset + vec.shape[0]].set(vec)


def _pad_rows(x, bs, s, rows):
    w = x.shape[-1]
    return jnp.concatenate([x.reshape(bs, s, w), jnp.zeros((bs, rows - s, w), x.dtype)], axis=1)


def kernel(x_prompt, x_sample, cache_dsa_k, cache_dsa_v, cache_dsa_kidx, cache_sb_k, cache_sb_v, state_gdn_S,
           state_gdn_conv, page_table, c_prompt, c_sample, w_cond, b_cond, w_in, w_out, ln_g, ln_b, conv_w, a_log,
           dt_bias, gdn_norm_g, gmlp_w_s, gmlp_b_s, w_ff1, w_ff2):
    bp, seq, d = x_prompt.shape
    bs, s, _ = x_sample.shape
    depth = w_in.shape[0]
    n_pool, page = cache_dsa_k.shape[1], cache_dsa_k.shape[2]
    alpha = (2 * depth) ** 0.25
    assert seq % BLK == 0 and page == BLK and LANES % (s * N_H) == 0 and s >= CONV_W - 1 and s & (s - 1) == 0

    c_all = jnp.concatenate([c_prompt, c_sample], axis=0)
    rc = ((c_all.shape[0] + 7) // 8) * 8
    c_all = jnp.concatenate([c_all, jnp.zeros((rc - c_all.shape[0], d), F32)], axis=0)
    mod = _cond(c_all, w_cond, b_cond)

    cache_i = cache_dsa_kidx
    cache_ak, cache_av, cache_ck, cache_cv = cache_dsa_k, cache_dsa_v, cache_sb_k, cache_sb_v

    xp = x_prompt.reshape(bp * seq, d)
    xs = x_sample.reshape(bs * s, d)
    c_gmlp = min(seq, GMLP_CHUNK)
    cs_gmlp = min(s, GMLP_CHUNK)
    st_p, st_s = [], []
    for l in range(depth):
        w_pad = _pad_w_in(w_in[l])
        wout_b, w1_b, w2_b = w_out[l].astype(BF16), w_ff1[l].astype(BF16), w_ff2[l].astype(BF16)
        alog_row, dtb_row = _lane_row(a_log[l], SMALL_A), _lane_row(dt_bias[l], SMALL_A)
        ng_row = jnp.tile(gdn_norm_g[l], N_H).reshape(1, W_GROUP)
        pre_ln = l == 0

        assert bp == 1
        mods = [mod[l, 0:1, i * d:(i + 1) * d] for i in range(6)]
        pr = _proj(xp, mods[0], mods[1], w_pad, pre_ln, tm=512 if (bp * seq) % 512 == 0 else BLK)
        out_a = _dsa_prompt(pr["aq"], pr["iq"], pr["small"], pr["ik_b"], pr["ak_b"], pr["av_b"], bp, seq)
        out_b, _ = _gmlp(pr["bu"], pr["bv"], gmlp_w_s[l][:, :c_gmlp, :c_gmlp], gmlp_b_s[l][:, :c_gmlp].T,
                         c_gmlp, c_gmlp, False)
        out_c = _sb_prompt(pr["cq"], pr["ck_b"], pr["cv_b"], bp, seq)
        gchunk = math.gcd(seq, GDN_CHUNK)
        out_d, s_new = _gdn(pr["dqkv"], pr["small"], pr["dgate"], jnp.zeros((bp, 8, 3 * W_GROUP), F32),
                            jnp.zeros((bp, N_H, HEAD_DIM, HEAD_DIM), F32), conv_w[l], alog_row, dtb_row, ng_row,
                            bp, seq, gchunk, 256 if seq % 256 == 0 else gchunk)
        xp = _outffn(xp, (out_a, out_b, out_c, out_d), mods[2], mods[3], mods[4], mods[5], wout_b, ln_g[l], ln_b[l],
                     w1_b, w2_b, pre_ln, alpha, tm=256 if (bp * seq) % 256 == 0 else BLK)
        buf_new = pr["dqkv"].reshape(bp, seq, 3 * W_GROUP)[:, seq - (CONV_W - 1):]
        st_p.append((pr["ak"].reshape(bp, seq, N_H, HEAD_DIM), pr["av"].reshape(bp, seq, N_H, HEAD_DIM),
                     pr["ik"].reshape(bp, seq, IDX_DIM), pr["ck"].reshape(bp, seq, N_H, HEAD_DIM),
                     pr["cv"].reshape(bp, seq, N_H, HEAD_DIM), s_new, buf_new))

        mods = [jnp.repeat(mod[l, bp:bp + bs, i * d:(i + 1) * d], s, axis=0) for i in range(6)]
        ps = _proj(xs, mods[0], mods[1], w_pad, pre_ln, tm=bs * s)
        out_a = _dsa_sample(page_table, ps["aq"], ps["iq"], ps["small"], _pad_rows(ps["ik"], bs, s, BLK),
                            _pad_rows(ps["ak"], bs, s, BLK), _pad_rows(ps["av"], bs, s, BLK),
                            cache_i, cache_ak, cache_av, l, bs, s)
        out_b, v_rows = _gmlp(ps["bu"], ps["bv"], jnp.tile(gmlp_w_s[l][:, :cs_gmlp, :cs_gmlp], (1, bs, bs)),
                              jnp.tile(gmlp_b_s[l][:, :cs_gmlp].T, (bs, 1)), bs * s, cs_gmlp, True)
        out_c = _sb_sample(page_table, ps["cq"], _pad_rows(ps["ck"], bs, s, BLK), _pad_rows(ps["cv"], bs, s, BLK),
                           cache_ck, cache_cv, l, bs, s)
        tail0 = jnp.concatenate([jnp.zeros((bs, 8 - (CONV_W - 1), 3 * W_GROUP), F32), state_gdn_conv[l]], axis=1)
        out_d, s_new = _gdn(ps["dqkv"], ps["small"], ps["dgate"], tail0, state_gdn_S[l], conv_w[l], alog_row,
                            dtb_row, ng_row, bs, s, math.gcd(s, GDN_CHUNK), s)
        xs = _outffn(xs, (out_a, out_b, out_c, out_d), mods[2], mods[3], mods[4], mods[5], wout_b, ln_g[l], ln_b[l],
                     w1_b, w2_b, pre_ln, alpha, tm=bs * s)
        buf_new = ps["dqkv"].reshape(bs, s, 3 * W_GROUP)[:, s - (CONV_W - 1):]
        st_s.append((ps["ak"].reshape(bs, s, N_H, HEAD_DIM), ps["av"].reshape(bs, s, N_H, HEAD_DIM),
                     ps["ik"].reshape(bs, s, IDX_DIM), ps["ck"].reshape(bs, s, N_H, HEAD_DIM),
                     ps["cv"].reshape(bs, s, N_H, HEAD_DIM), s_new, buf_new, v_rows.reshape(bs, s, W_GROUP)))

    outs_p = tuple(jnp.stack(t) for t in zip(*st_p))
    outs_s = tuple(jnp.stack(t) for t in zip(*st_s))
    return (xp.reshape(bp, seq, d), xs.reshape(bs, s, d)) + outs_p + outs_s
```

```python
import functools
import math

import jax
import jax.numpy as jnp
from jax import lax
from jax.experimental import pallas as pl
from jax.experimental.pallas import tpu as pltpu

F32 = jnp.float32
BF16 = jnp.bfloat16
I32 = jnp.int32
I16 = jnp.int16
I16_MIN = -2 ** 15

LN_EPS = 1e-5
NORM_EPS = 1e-6
N_H = 4
HEAD_DIM = 64
W_GROUP = N_H * HEAD_DIM
IDX_HEADS = 4
IDX_DIM = 64
TOPK_MAX = 256
GMLP_CHUNK = 128
GDN_CHUNK = 64
CONV_W = 4
BLK = 128
LANES = 128
VMEM_LIMIT = 56 * 1024 * 1024
SB_ZERO_TAIL = -110.0
INT_MIN = -2 ** 31
LOG2E = 1.4426950408889634
MASKED = -1e30
M_INIT = -1e20
UNROLL = 4
PAGE_UNROLL = 4
COUNT_UNROLL = 4
GDN_BASE = 8

PIECES = (("dqkv", 768, 768), ("aq", 256, 256), ("ak", 256, 256), ("av", 256, 256), ("iq", 256, 256),
          ("bu", 256, 256), ("bv", 256, 256), ("cq", 256, 256), ("ck", 256, 256), ("cv", 256, 256),
          ("dgate", 256, 256), ("ik", 64, 128), ("small", 12, 128))
N_PAD = sum(p[2] for p in PIECES)
SMALL_IW, SMALL_BETA, SMALL_A = 0, 4, 8
BF16_COPIES = ("ak", "ck", "cv", "ik")


def _cparams(sem, vmem=VMEM_LIMIT):
    return pltpu.CompilerParams(dimension_semantics=sem, vmem_limit_bytes=vmem)


def _ln_plain(x):
    mu = jnp.mean(x, axis=-1, keepdims=True)
    xc = x - mu
    var = jnp.mean(xc * xc, axis=-1, keepdims=True)
    return xc * lax.rsqrt(var + LN_EPS)


def _sigmoid(x):
    return 1.0 / (1.0 + jnp.exp(-x))


def _silu(x):
    return x * _sigmoid(x)


def _softplus(x):
    return jnp.maximum(x, 0.0) + jnp.log1p(jnp.exp(-jnp.abs(x)))


def _gelu_tanh(x):
    c = math.sqrt(2.0 / math.pi)
    return x * (0.5 * (1.0 + jnp.tanh(c * (x + 0.044715 * (x * x * x)))))


def _dot(a, b):
    return jnp.dot(a.astype(BF16), b.astype(BF16), preferred_element_type=F32)


def _dot_nt(a, b):
    return lax.dot_general(a.astype(BF16), b.astype(BF16), (((1,), (1,)), ((), ())),
                           preferred_element_type=F32)


def _dot_tn(a, b):
    return lax.dot_general(a.astype(BF16), b.astype(BF16), (((0,), (0,)), ((), ())),
                           preferred_element_type=F32)


def _split(x):
    hi = x.astype(BF16)
    lo = (x - hi.astype(F32)).astype(BF16)
    return hi, lo


def _dot_precise_lhs(a, b01):
    hi, lo = _split(a)
    return jnp.dot(hi, b01, preferred_element_type=F32) + jnp.dot(lo, b01, preferred_element_type=F32)


def _dot_precise_rhs(a01, b):
    hi, lo = _split(b)
    return jnp.dot(a01, hi, preferred_element_type=F32) + jnp.dot(a01, lo, preferred_element_type=F32)


def _dot3(a, b):
    ah, al = _split(a)
    bh, bl = _split(b)
    return (jnp.dot(ah, bh, preferred_element_type=F32) + jnp.dot(al, bh, preferred_element_type=F32)
            + jnp.dot(ah, bl, preferred_element_type=F32))


def _iota(shape, dim):
    return lax.broadcasted_iota(I32, shape, dim)


def _sort_key(score):
    bits = pltpu.bitcast(score, I32)
    return bits ^ ((bits >> 31) & jnp.int32(0x7FFFFFFF))


def _alibi_slope(h):
    return 2.0 ** (-8.0 * (h + 1) / N_H)


def _heads(x):
    return [x[:, h * HEAD_DIM:(h + 1) * HEAD_DIM] for h in range(N_H)]


def _cond_kernel(c_ref, w_ref, b_ref, o_ref):
    s = _silu(c_ref[...])
    o_ref[0] = _dot3(s, w_ref[0]) + b_ref[0]


def _cond(c_all, w_cond, b_cond):
    depth, d, n6 = w_cond.shape
    rc = c_all.shape[0]
    tn = 1536 if n6 % 1536 == 0 else n6
    return pl.pallas_call(
        _cond_kernel,
        out_shape=jax.ShapeDtypeStruct((depth, rc, n6), F32),
        grid=(depth, n6 // tn),
        in_specs=[pl.BlockSpec((rc, d), lambda l, j: (0, 0)),
                  pl.BlockSpec((1, d, tn), lambda l, j: (l, 0, j)),
                  pl.BlockSpec((1, 1, tn), lambda l, j: (l, 0, j))],
        out_specs=pl.BlockSpec((1, rc, tn), lambda l, j: (l, 0, j)),
        compiler_params=_cparams(("arbitrary", "arbitrary")),
        name="cond",
    )(c_all, w_cond, b_cond.reshape(depth, 1, n6))


def _proj_kernel(x_ref, sh_ref, sc_ref, w_ref, *o_refs, pre_ln, extras):
    x = x_ref[...]
    if pre_ln:
        x = _ln_plain(x)
    hb = (x * (1.0 + sc_ref[...]) + sh_ref[...]).astype(BF16)
    col = 0
    for (name, width, padded), o_ref in zip(PIECES, o_refs[:len(PIECES)]):
        res = jnp.dot(hb, w_ref[:, col:col + padded], preferred_element_type=F32)
        o_ref[...] = res if o_ref.shape[-1] == padded else res[:, :o_ref.shape[-1]]
        if extras and name in BF16_COPIES:
            o_refs[len(PIECES) + BF16_COPIES.index(name)][...] = res.astype(BF16)
        if extras and name == "av":
            vt_ref = o_refs[len(PIECES) + len(BF16_COPIES)]
            for j in range(vt_ref.shape[0]):
                vt_ref[j] = res[j * BLK:(j + 1) * BLK, :].T.astype(BF16)
        col += padded


def _mod_spec(m, tm, d):
    if m.shape[0] == 1:
        return pl.BlockSpec((1, d), lambda i: (0, 0))
    return pl.BlockSpec((tm, d), lambda i: (i, 0))


def _proj(x, shift, scale, w_pad, pre_ln, tm, extras):
    r, d = x.shape
    out_shapes, out_specs = [], []
    for name, width, padded in PIECES:
        w_out = width if name == "ik" else padded
        out_shapes.append(jax.ShapeDtypeStruct((r, w_out), F32))
        out_specs.append(pl.BlockSpec((tm, w_out), lambda i: (i, 0)))
    if extras:
        for name in BF16_COPIES:
            w_out = LANES if name == "ik" else W_GROUP
            out_shapes.append(jax.ShapeDtypeStruct((r, w_out), BF16))
            out_specs.append(pl.BlockSpec((tm, w_out), lambda i: (i, 0)))
        out_shapes.append(jax.ShapeDtypeStruct((r // BLK, W_GROUP, BLK), BF16))
        out_specs.append(pl.BlockSpec((tm // BLK, W_GROUP, BLK), lambda i: (i, 0, 0)))
    outs = pl.pallas_call(
        functools.partial(_proj_kernel, pre_ln=pre_ln, extras=extras),
        out_shape=out_shapes,
        grid=(r // tm,),
        in_specs=[pl.BlockSpec((tm, d), lambda i: (i, 0)), _mod_spec(shift, tm, d), _mod_spec(scale, tm, d),
                  pl.BlockSpec((d, N_PAD), lambda i: (0, 0))],
        out_specs=out_specs,
        compiler_params=_cparams(("parallel",)),
        name="proj",
    )(x, shift, scale, w_pad)
    res = {name: o for (name, _, _), o in zip(PIECES, outs)}
    if extras:
        for i, name in enumerate(BF16_COPIES):
            res[name + "_b"] = outs[len(PIECES) + i]
        res["av_t"] = outs[len(PIECES) + len(BF16_COPIES)]
    return res


def _pad_w_in(w_in_l):
    off = {}
    start = 0
    for name, n in (("aq", 256), ("ak", 256), ("av", 256), ("iq", 256), ("ik", 64), ("iw", 4), ("bu", 256),
                    ("bv", 256), ("cq", 256), ("ck", 256), ("cv", 256), ("dqkv", 768), ("dbeta", 4), ("da", 4),
                    ("dgate", 256)):
        off[name] = (start, n)
        start += n
    d = w_in_l.shape[0]
    cols = []
    for name, width, padded in PIECES:
        if name == "small":
            parts = [w_in_l[:, off[k][0]:off[k][0] + off[k][1]] for k in ("iw", "dbeta", "da")]
            piece = jnp.concatenate(parts, axis=1)
        else:
            piece = w_in_l[:, off[name][0]:off[name][0] + off[name][1]]
        if padded > piece.shape[1]:
            piece = jnp.concatenate([piece, jnp.zeros((d, padded - piece.shape[1]), piece.dtype)], axis=1)
        cols.append(piece)
    return jnp.concatenate(cols, axis=1).astype(BF16)


def _outffn_kernel(x_ref, oa_ref, ob_ref, oc_ref, od_ref, gm_ref, shf_ref, scf_ref, gf_ref,
                   wout_ref, lng_ref, lnb_ref, w1_ref, w2_ref, o_ref, *, pre_ln, alpha):
    x = x_ref[...]
    if pre_ln:
        x = _ln_plain(x)
    mo = None
    for g, r in enumerate((oa_ref, ob_ref, oc_ref, od_ref)):
        part = _dot(r[...], wout_ref[g * W_GROUP:(g + 1) * W_GROUP, :])
        mo = part if mo is None else mo + part
    x1 = _ln_plain(alpha * x + gm_ref[...] * mo) * lng_ref[0:1, :] + lnb_ref[0:1, :]
    hf = x1 * (1.0 + scf_ref[...]) + shf_ref[...]
    a = jnp.maximum(_dot(hf, w1_ref[...]), 0.0)
    ff = _dot(a * a, w2_ref[...])
    o_ref[...] = _ln_plain(alpha * x1 + gf_ref[...] * ff) * lng_ref[1:2, :] + lnb_ref[1:2, :]


def _outffn(x, branches, gm, shf, scf, gf, wout_b, ln_g, ln_b, w1_b, w2_b, pre_ln, alpha, tm):
    r, d = x.shape
    dff = w1_b.shape[1]
    row = lambda w: pl.BlockSpec((tm, w), lambda i: (i, 0))
    const = lambda shp: pl.BlockSpec(shp, lambda i: (0, 0), pipeline_mode=pl.Buffered(1))
    return pl.pallas_call(
        functools.partial(_outffn_kernel, pre_ln=pre_ln, alpha=alpha),
        out_shape=jax.ShapeDtypeStruct((r, d), F32),
        grid=(r // tm,),
        in_specs=[row(d)] + [row(W_GROUP)] * 4 + [_mod_spec(m, tm, d) for m in (gm, shf, scf, gf)]
        + [const((d, d)), const((2, d)), const((2, d)), const((d, dff)), const((dff, d))],
        out_specs=row(d),
        compiler_params=_cparams(("parallel",)),
        name="outffn",
    )(x, *branches, gm, shf, scf, gf, wout_b, ln_g, ln_b, w1_b, w2_b)


def _gmlp_kernel(u_ref, v_ref, w_ref, bt_ref, ob_ref, *vn_refs, chunk):
    u = _gelu_tanh(u_ref[...])
    vn = _ln_plain(_gelu_tanh(v_ref[...]))
    if vn_refs:
        vn_refs[0][...] = vn
    rows = u.shape[0]
    ri, ci = _iota((rows, rows), 0), _iota((rows, rows), 1)
    mask = ci <= ri
    if rows != chunk:
        sh = int(math.log2(chunk))
        mask = mask & ((ri >> sh) == (ci >> sh))
    vb = vn.astype(BF16)
    for g in range(N_H):
        w = jnp.where(mask, w_ref[g], 0.0).astype(BF16)
        mixed = jnp.dot(w, vb[:, g * HEAD_DIM:(g + 1) * HEAD_DIM], preferred_element_type=F32)
        mixed = mixed + bt_ref[:, g:g + 1]
        ob_ref[:, g * HEAD_DIM:(g + 1) * HEAD_DIM] = u[:, g * HEAD_DIM:(g + 1) * HEAD_DIM] * mixed


def _gmlp(bu, bv, w_tiled, bt_tiled, rows, chunk, want_vn):
    r = bu.shape[0]
    row = pl.BlockSpec((rows, W_GROUP), lambda i: (i, 0))
    out_shape = [jax.ShapeDtypeStruct((r, W_GROUP), F32)]
    out_specs = [row]
    if want_vn:
        out_shape.append(jax.ShapeDtypeStruct((r, W_GROUP), F32))
        out_specs.append(row)
    outs = pl.pallas_call(
        functools.partial(_gmlp_kernel, chunk=chunk),
        out_shape=out_shape,
        grid=(r // rows,),
        in_specs=[row, row, pl.BlockSpec((N_H, rows, rows), lambda i: (0, 0, 0)),
                  pl.BlockSpec((rows, N_H), lambda i: (0, 0))],
        out_specs=out_specs,
        compiler_params=_cparams(("parallel",)),
        name="gmlp",
    )(bu, bv, w_tiled, bt_tiled)
    return outs if want_vn else (outs[0], None)


def _sb_block(qk, pv, vis, carry, acc, upper):
    new_carry, new_acc = [], []
    for h in range(N_H):
        z = qk(h)
        l_raw = -_softplus(z)
        l_vis = l_raw if vis is None else jnp.where(vis, l_raw, 0.0)
        tail = _dot_precise_lhs(l_vis, upper) + carry[h]
        w = jnp.exp(z + l_raw + tail)
        if vis is not None:
            w = jnp.where(vis, w, 0.0)
        new_acc.append(acc[h] + pv(h, w))
        new_carry.append(carry[h] + jnp.sum(l_vis, axis=1, keepdims=True))
    return new_carry, new_acc


def _sb_live(carry):
    m = carry[0]
    for c in carry[1:]:
        m = jnp.maximum(m, c)
    return (jnp.max(m) > SB_ZERO_TAIL).astype(I32)


def _upper_ones():
    return (_iota((BLK, BLK), 0) > _iota((BLK, BLK), 1)).astype(BF16)


def _natural_kv(q_heads, k, v):
    kh, vh = _heads(k), _heads(v)
    return (lambda h: _dot_nt(q_heads[h], kh[h])), (lambda h, w: _dot(w, vh[h]))


def _sbp_kernel(q_ref, k_ref, v_ref, o_ref):
    qb = pl.program_id(1)
    q_heads = _heads((q_ref[...] * (HEAD_DIM ** -0.5)).astype(BF16))
    upper = _upper_ones()
    ri, ci = _iota((BLK, BLK), 0), _iota((BLK, BLK), 1)
    zero_c = [jnp.zeros((BLK, 1), F32)] * N_H
    zero_a = [jnp.zeros((BLK, HEAD_DIM), F32)] * N_H
    off = pl.multiple_of(qb * BLK, BLK)
    qk, pv = _natural_kv(q_heads, k_ref[pl.ds(off, BLK), :], v_ref[pl.ds(off, BLK), :])
    carry, acc = _sb_block(qk, pv, ci < ri, zero_c, zero_a, upper)

    def cond(st):
        return jnp.logical_and(st[0] >= 0, st[1] > 0)

    def body(st):
        kb = st[0]
        o = pl.multiple_of(kb * BLK, BLK)
        qk, pv = _natural_kv(q_heads, k_ref[pl.ds(o, BLK), :], v_ref[pl.ds(o, BLK), :])
        c, a = _sb_block(qk, pv, None, list(st[2:2 + N_H]), list(st[2 + N_H:]), upper)
        return (kb - 1, _sb_live(c), *c, *a)

    st = lax.while_loop(cond, body, (qb - 1, _sb_live(carry), *carry, *acc))
    for h in range(N_H):
        o_ref[:, h * HEAD_DIM:(h + 1) * HEAD_DIM] = st[2 + N_H + h]


def _sb_prompt(cq, ck_b, cv_b, batch, seq):
    nq = seq // BLK
    full = pl.BlockSpec((seq, W_GROUP), lambda b, i: (b, 0), pipeline_mode=pl.Buffered(1))
    return pl.pallas_call(
        _sbp_kernel,
        out_shape=jax.ShapeDtypeStruct(cq.shape, F32),
        grid=(batch, nq),
        in_specs=[pl.BlockSpec((BLK, W_GROUP), lambda b, i: (b * nq + i, 0)), full, full],
        out_specs=pl.BlockSpec((BLK, W_GROUP), lambda b, i: (b * nq + i, 0)),
        compiler_params=_cparams(("parallel", "arbitrary")),
        name="sb_prompt",
    )(cq, ck_b, cv_b)


def _sbs_kernel(pt_ref, q_ref, kn_ref, vn_ref, kc_ref, vc_ref, o_ref, kbuf, vbuf, sem, *, layer, n_pages):
    b = pl.program_id(0)
    rows = q_ref.shape[0]
    q_heads = _heads((q_ref[...] * (HEAD_DIM ** -0.5)).astype(BF16))
    upper = _upper_ones()
    ri, ci = _iota((rows, BLK), 0), _iota((rows, BLK), 1)
    zero_c = [jnp.zeros((rows, 1), F32)] * N_H
    zero_a = [jnp.zeros((rows, HEAD_DIM), F32)] * N_H
    qk, pv = _natural_kv(q_heads, kn_ref[0], vn_ref[0])
    carry, acc = _sb_block(qk, pv, ci < ri, zero_c, zero_a, upper)

    def copies(p):
        page = pt_ref[b, p]
        return (pltpu.make_async_copy(kc_ref.at[layer, page], kbuf, sem.at[0]),
                pltpu.make_async_copy(vc_ref.at[layer, page], vbuf, sem.at[1]))

    def cond(st):
        return jnp.logical_and(st[0] >= 0, st[1] > 0)

    def body(st):
        p = st[0]
        ck, cv = copies(p)
        ck.start()
        cv.start()
        ck.wait()
        cv.wait()
        c, a = _sb_block(lambda h: _dot(q_heads[h], kbuf[h]), lambda h, w: _dot_nt(w, vbuf[h]), None,
                         list(st[2:2 + N_H]), list(st[2 + N_H:]), upper)
        return (p - 1, _sb_live(c), *c, *a)

    st = lax.while_loop(cond, body, (jnp.int32(n_pages - 1), _sb_live(carry), *carry, *acc))
    for h in range(N_H):
        o_ref[:, h * HEAD_DIM:(h + 1) * HEAD_DIM] = st[2 + N_H + h]


def _sb_sample(page_table, cq, ck_new_pad, cv_new_pad, cache_kt, cache_vt, layer, bs, s):
    n_pages = page_table.shape[1]
    page = cache_kt.shape[-1]
    return pl.pallas_call(
        functools.partial(_sbs_kernel, layer=layer, n_pages=n_pages),
        out_shape=jax.ShapeDtypeStruct(cq.shape, F32),
        grid_spec=pltpu.PrefetchScalarGridSpec(
            num_scalar_prefetch=1,
            grid=(bs,),
            in_specs=[pl.BlockSpec((s, W_GROUP), lambda b, pt: (b, 0)),
                      pl.BlockSpec((1, BLK, W_GROUP), lambda b, pt: (b, 0, 0)),
                      pl.BlockSpec((1, BLK, W_GROUP), lambda b, pt: (b, 0, 0)),
                      pl.BlockSpec(memory_space=pl.ANY), pl.BlockSpec(memory_space=pl.ANY)],
            out_specs=pl.BlockSpec((s, W_GROUP), lambda b, pt: (b, 0)),
            scratch_shapes=[pltpu.VMEM((N_H, HEAD_DIM, page), F32), pltpu.VMEM((N_H, HEAD_DIM, page), F32),
                            pltpu.SemaphoreType.DMA((2,))]),
        compiler_params=_cparams(("arbitrary",)),
        name="sb_sample",
    )(page_table, cq, ck_new_pad, cv_new_pad, cache_kt, cache_vt)


def _kth_largest_key(count_ge, shape, k):
    kf = jnp.float32(k)
    t = jnp.where(count_ge(jnp.zeros(shape, I32)) >= kf, jnp.int32(0), jnp.int32(INT_MIN))

    def body(i, t):
        cand = t + (jnp.int32(1) << (30 - i))
        return jnp.where(count_ge(cand) >= kf, cand, t)

    return lax.fori_loop(0, 31, body, t)


def _kth_largest_i16(count_ge, shape, k):
    t = jnp.where(count_ge(jnp.zeros(shape, I32)) >= k, jnp.int32(0), jnp.int32(I16_MIN))

    def body(i, t):
        cand = t + (jnp.int32(1) << (14 - i))
        return jnp.where(count_ge(cand) >= k, cand, t)

    return lax.fori_loop(0, 15, body, t)


def _dsap_kernel(aq_ref, iq_ref, sm_ref, ik_ref, k_ref, vt_ref, o_ref, key_ref, hi_ref, lo_ref, *, topk):
    qb = pl.program_id(1)
    nkb = qb + 1
    ri, ci = _iota((BLK, BLK), 0), _iota((BLK, BLK), 1)
    diag_ok = ri <= ci

    iq_t = (iq_ref[...] * (IDX_DIM ** -0.5)).T
    iq_all = jnp.concatenate([iq_t[h * IDX_DIM:(h + 1) * IDX_DIM] for h in range(IDX_HEADS)],
                             axis=1).astype(BF16)
    sm_t = sm_ref[...].T
    iw = [sm_t[SMALL_IW + h:SMALL_IW + h + 1, :] * (IDX_HEADS ** -0.5) for h in range(IDX_HEADS)]

    def score_block(kb, diagonal):
        o = pl.multiple_of(kb * BLK, BLK)
        kidx = ik_ref[pl.ds(o, BLK), :][:, :IDX_DIM]
        s_all = jnp.dot(kidx, iq_all, preferred_element_type=F32)
        score = iw[0] * jnp.maximum(s_all[:, 0:BLK], 0.0)
        for h in range(1, IDX_HEADS):
            score = score + iw[h] * jnp.maximum(s_all[:, h * BLK:(h + 1) * BLK], 0.0)
        if diagonal:
            score = jnp.where(diag_ok, score, -jnp.inf)
        key = _sort_key(score)
        key_ref[kb] = key
        hi_ref[kb] = (key >> 16).astype(I16)
        lo_ref[kb] = ((key & 0xFFFF) + I16_MIN).astype(I16)

    def score_one(kb, _):
        score_block(kb, False)
        return 0

    def score_group(g, _):
        for u in range(UNROLL):
            score_block(g * UNROLL + u, False)
        return 0

    lax.fori_loop(0, qb // UNROLL, score_group, 0)
    lax.fori_loop((qb // UNROLL) * UNROLL, qb, score_one, 0)
    score_block(qb, True)

    def count_blocks(ref, pred, dtype):
        def one(kb, cnt):
            return cnt + jnp.where(pred(ref[kb]), jnp.ones((), dtype), jnp.zeros((), dtype))

        def group(g, cnt):
            for u in range(COUNT_UNROLL):
                cnt = one(g * COUNT_UNROLL + u, cnt)
            return cnt

        groups = nkb // COUNT_UNROLL
        cnt = lax.fori_loop(0, groups, group, jnp.zeros((BLK, BLK), dtype))
        cnt = lax.fori_loop(groups * COUNT_UNROLL, nkb, one, cnt)
        return jnp.sum(cnt.astype(F32), axis=0, keepdims=True)

    def count(pred):
        return count_blocks(key_ref, pred, I32)

    def count16(ref, pred):
        return count_blocks(ref, pred, I16)

    kf = jnp.float32(topk)
    t_hi = _kth_largest_i16(lambda c: count16(hi_ref, lambda v: v >= c.astype(I16)), (1, BLK), kf)
    t_hi16 = t_hi.astype(I16)
    k_lo = kf - count16(hi_ref, lambda v: v > t_hi16)

    def keep_low(kb, _):
        lo_ref[kb] = jnp.where(hi_ref[kb] == t_hi16, lo_ref[kb], jnp.int16(I16_MIN))
        return 0

    lax.fori_loop(0, nkb, keep_low, 0)
    t_lo = _kth_largest_i16(lambda c: count16(lo_ref, lambda v: v >= c.astype(I16)), (1, BLK), k_lo)
    thr = (t_hi << 16) + (t_lo - I16_MIN)
    n_gt = count(lambda key: key > thr)
    n_eq = count(lambda key: key == thr)
    room = jnp.float32(topk) - n_gt
    tie_break = jnp.max(n_eq - room) > 0.0

    aq_t = (aq_ref[...] * (HEAD_DIM ** -0.5 * LOG2E)).T
    bias = [(_alibi_slope(h) * LOG2E) * ri.astype(F32) for h in range(N_H)]
    zeros = jnp.zeros((HEAD_DIM, BLK), F32)
    q_bd = jnp.concatenate(
        [jnp.concatenate([aq_t[g * HEAD_DIM:(g + 1) * HEAD_DIM] if g == h else zeros for g in range(N_H)], axis=0)
         for h in range(N_H)], axis=1).astype(BF16)
    strict_lower = (ci < ri).astype(BF16)

    def attend(ties, kb, diagonal, st):
        seen = st[0]
        key = key_ref[kb]
        if ties:
            eq = key == thr
            rank = seen + jnp.dot(strict_lower, eq.astype(BF16), preferred_element_type=F32)
            sel = jnp.logical_or(key > thr, jnp.logical_and(eq, rank < room))
            seen = seen + jnp.sum(eq.astype(F32), axis=0, keepdims=True)
        else:
            sel = key >= thr
        if diagonal:
            sel = jnp.logical_and(sel, diag_ok)
        blk_off = ((kb - qb) * BLK).astype(F32)
        o = pl.multiple_of(kb * BLK, BLK)
        s_all = jnp.dot(k_ref[pl.ds(o, BLK), :], q_bd, preferred_element_type=F32)
        vt = vt_ref[kb]
        out = [seen]
        for h in range(N_H):
            m_old, l_old, a_old = st[1 + 3 * h], st[2 + 3 * h], st[3 + 3 * h]
            c_blk = (_alibi_slope(h) * LOG2E) * blk_off
            s = jnp.where(sel, s_all[:, h * BLK:(h + 1) * BLK] + bias[h], MASKED)
            m_new = jnp.maximum(m_old, jnp.max(s, axis=0, keepdims=True) + c_blk)
            p = jnp.exp2(s - (m_new - c_blk))
            alpha = jnp.exp2(m_old - m_new)
            pv = jnp.dot(vt[h * HEAD_DIM:(h + 1) * HEAD_DIM, :], p.astype(BF16), preferred_element_type=F32)
            out += [m_new, alpha * l_old + jnp.sum(p, axis=0, keepdims=True), alpha * a_old + pv]
        return tuple(out)

    init = [jnp.zeros((1, BLK), F32)]
    for h in range(N_H):
        init += [jnp.full((1, BLK), M_INIT, F32), jnp.zeros((1, BLK), F32), jnp.zeros((HEAD_DIM, BLK), F32)]
    def sweep(ties):
        def one(kb, st):
            return attend(ties, kb, False, st)

        def group(g, st):
            for u in range(UNROLL):
                st = one(g * UNROLL + u, st)
            return st

        groups = qb // UNROLL
        st = lax.fori_loop(0, groups, group, tuple(init))
        st = lax.fori_loop(groups * UNROLL, qb, one, st)
        return attend(ties, qb, True, st)

    st = lax.cond(tie_break, lambda: sweep(True), lambda: sweep(False))
    out_t = jnp.concatenate([st[3 + 3 * h] / st[2 + 3 * h] for h in range(N_H)], axis=0)
    o_ref[...] = out_t.T


def _dsa_prompt(aq, iq, small, ik_b, ak_b, av_t, batch, seq):
    nq = seq // BLK
    topk = min(TOPK_MAX, seq // 4)
    blk = lambda w: pl.BlockSpec((BLK, w), lambda b, i: (b * nq + i, 0))
    full = lambda w: pl.BlockSpec((seq, w), lambda b, i: (b, 0), pipeline_mode=pl.Buffered(1))
    return pl.pallas_call(
        functools.partial(_dsap_kernel, topk=topk),
        out_shape=jax.ShapeDtypeStruct(aq.shape, F32),
        grid=(batch, nq),
        in_specs=[blk(W_GROUP), blk(W_GROUP), blk(LANES), full(LANES), full(W_GROUP),
                  pl.BlockSpec((nq, W_GROUP, BLK), lambda b, i: (b, 0, 0), pipeline_mode=pl.Buffered(1))],
        out_specs=blk(W_GROUP),
        scratch_shapes=[pltpu.VMEM((nq, BLK, BLK), I32), pltpu.VMEM((nq, BLK, BLK), I16),
                        pltpu.VMEM((nq, BLK, BLK), I16)],
        compiler_params=_cparams(("parallel", "arbitrary")),
        name="dsa_prompt",
    )(aq, iq, small, ik_b, ak_b, av_t)


def _dsas_kernel(pt_ref, aq_ref, iq_ref, sm_ref, ikn_ref, akn_ref, avn_ref, ic_ref, kc_ref, vc_ref, o_ref,
                 key_ref, s_ref, ibuf, kbuf, vbuf, sem, *, layer, n_pages, topk):
    b = pl.program_id(0)
    s_rows = aq_ref.shape[0]
    past = n_pages * BLK
    ri, ci = _iota((s_rows, BLK), 0), _iota((s_rows, BLK), 1)
    new_valid = jnp.logical_and(ci <= ri, ci < s_rows)

    iq = iq_ref[...] * (IDX_DIM ** -0.5)
    iq_all = jnp.concatenate(_heads(iq), axis=0).astype(BF16)
    iw = sm_ref[...][:, SMALL_IW:SMALL_IW + IDX_HEADS] * (IDX_HEADS ** -0.5)

    def idx_copy(p, slot):
        return pltpu.make_async_copy(ic_ref.at[layer, pt_ref[b, p]], ibuf.at[slot], sem.at[0, slot])

    def k_copy(p, slot):
        return pltpu.make_async_copy(kc_ref.at[layer, pt_ref[b, p]], kbuf.at[slot], sem.at[1, slot])

    def v_copy(p, slot):
        return pltpu.make_async_copy(vc_ref.at[layer, pt_ref[b, p]], vbuf.at[slot], sem.at[2, slot])

    def score_keys(s_all, valid):
        score = iw[:, 0:1] * jnp.maximum(s_all[0:s_rows], 0.0)
        for h in range(1, IDX_HEADS):
            score = score + iw[:, h:h + 1] * jnp.maximum(s_all[h * s_rows:(h + 1) * s_rows], 0.0)
        if valid is not None:
            score = jnp.where(valid, score, -jnp.inf)
        return _sort_key(score)

    def stream(copies, body, carry):
        u_n = PAGE_UNROLL
        trips = n_pages // u_n
        for u in range(u_n):
            for c in copies(u, u):
                c.start()

        def trip(g, carry):
            half = (g % 2) * u_n

            @pl.when(g + 1 < trips)
            def _():
                for u in range(u_n):
                    for c in copies((g + 1) * u_n + u, u_n - half + u):
                        c.start()

            for u in range(u_n):
                for c in copies(g * u_n + u, half + u):
                    c.wait()
            for u in range(u_n):
                carry = body(g * u_n + u, half + u, carry)
            return carry

        return lax.fori_loop(0, trips, trip, carry)

    def p1(p, slot, _):
        key_ref[p] = score_keys(_dot(iq_all, ibuf[slot]), None)
        return 0

    stream(lambda p, slot: [idx_copy(p, slot)], p1, 0)
    key_ref[n_pages] = score_keys(_dot_nt(iq_all, ikn_ref[0]), new_valid)

    def count(pred):
        per_trip = math.gcd(n_pages, 16)

        def body(g, cnt):
            for u in range(per_trip):
                cnt = cnt + jnp.where(pred(key_ref[g * per_trip + u]), 1, 0)
            return cnt
        cnt = lax.fori_loop(0, n_pages // per_trip, body, jnp.zeros((s_rows, BLK), I32))
        cnt = cnt + jnp.where(pred(key_ref[n_pages]), 1, 0)
        return jnp.sum(cnt.astype(F32), axis=1, keepdims=True)

    thr = _kth_largest_key(lambda cand: count(lambda key: key >= cand), (s_rows, 1), topk)
    n_gt = count(lambda key: key > thr)
    n_eq = count(lambda key: key == thr)
    room = jnp.float32(topk) - n_gt
    tie_break = jnp.max(n_eq - room) > 0.0

    q_heads = _heads((aq_ref[...] * (HEAD_DIM ** -0.5)).astype(BF16))
    lower = (_iota((BLK, BLK), 0) < _iota((BLK, BLK), 1)).astype(BF16)

    def logits_block(ties, p, qk, valid, kpos0, st):
        seen = st[0]
        key = key_ref[p]
        if ties:
            eq = key == thr
            rank = seen + jnp.dot(eq.astype(BF16), lower, preferred_element_type=F32)
            sel = jnp.logical_or(key > thr, jnp.logical_and(eq, rank < room))
            seen = seen + jnp.sum(eq.astype(F32), axis=1, keepdims=True)
        else:
            sel = key >= thr
        if valid is not None:
            sel = jnp.logical_and(sel, valid)
        dist = ((past - kpos0) + ri - ci).astype(F32)
        out = [seen]
        for h in range(N_H):
            s = jnp.where(sel, qk(h) - _alibi_slope(h) * dist, MASKED)
            s_ref[p, h] = s
            out.append(jnp.maximum(st[1 + h], s))
        return tuple(out)

    qk_new, pv_new = _natural_kv(q_heads, akn_ref[0], avn_ref[0])

    def pass_a(ties):
        st = stream(lambda p, slot: [k_copy(p, slot)],
                    lambda p, slot, st: logits_block(ties, p, lambda h: _dot(q_heads[h], kbuf[slot, h]), None,
                                                     p * BLK, st),
                    (jnp.zeros((s_rows, 1), F32),) + (jnp.full((s_rows, BLK), MASKED, F32),) * N_H)
        return logits_block(ties, n_pages, qk_new, new_valid, past, st)[1:]

    run_max = lax.cond(tie_break, lambda: pass_a(True), lambda: pass_a(False))
    m_fin = [jnp.max(run_max[h], axis=1, keepdims=True) for h in range(N_H)]

    def values_block(p, pv, st):
        out_l, out_a = [], []
        for h in range(N_H):
            pr = jnp.exp(s_ref[p, h] - m_fin[h])
            out_l.append(st[h] + pr)
            out_a.append(st[N_H + h] + pv(h, pr))
        return tuple(out_l + out_a)

    st = stream(lambda p, slot: [v_copy(p, slot)],
                lambda p, slot, st: values_block(p, lambda h, pr: _dot_nt(pr, vbuf[slot, h]), st),
                (jnp.zeros((s_rows, BLK), F32),) * N_H + (jnp.zeros((s_rows, HEAD_DIM), F32),) * N_H)
    st = values_block(n_pages, pv_new, st)
    for h in range(N_H):
        o_ref[:, h * HEAD_DIM:(h + 1) * HEAD_DIM] = st[N_H + h] / jnp.sum(st[h], axis=1, keepdims=True)


def _dsa_sample(page_table, aq, iq, small, ik_new_pad, ak_new_pad, av_new_pad, cache_it, cache_kt, cache_vt,
                layer, bs, s):
    n_pages = page_table.shape[1]
    page = cache_kt.shape[-1]
    topk = min(TOPK_MAX, (n_pages * page + s) // 4)
    row = lambda w: pl.BlockSpec((s, w), lambda b, pt: (b, 0))
    new = lambda w: pl.BlockSpec((1, BLK, w), lambda b, pt: (b, 0, 0))
    hbm = pl.BlockSpec(memory_space=pl.ANY)
    return pl.pallas_call(
        functools.partial(_dsas_kernel, layer=layer, n_pages=n_pages, topk=topk),
        out_shape=jax.ShapeDtypeStruct(aq.shape, F32),
        grid_spec=pltpu.PrefetchScalarGridSpec(
            num_scalar_prefetch=1,
            grid=(bs,),
            in_specs=[row(W_GROUP), row(W_GROUP), row(LANES), new(IDX_DIM), new(W_GROUP), new(W_GROUP),
                      hbm, hbm, hbm],
            out_specs=row(W_GROUP),
            scratch_shapes=[pltpu.VMEM((n_pages + 1, s, BLK), I32), pltpu.VMEM((n_pages + 1, N_H, s, BLK), F32),
                            pltpu.VMEM((2 * PAGE_UNROLL, IDX_DIM, page), F32),
                            pltpu.VMEM((2 * PAGE_UNROLL, N_H, HEAD_DIM, page), F32),
                            pltpu.VMEM((2 * PAGE_UNROLL, N_H, HEAD_DIM, page), F32),
                            pltpu.SemaphoreType.DMA((3, 2 * PAGE_UNROLL))]),
        compiler_params=_cparams(("arbitrary",)),
        name="dsa_sample",
    )(page_table, aq, iq, small, ik_new_pad, ak_new_pad, av_new_pad, cache_it, cache_kt, cache_vt)


def _gdn_kernel(x_ref, sm_ref, gate_ref, tail0_ref, s0_ref, cw_ref, alog_ref, dtb_ref, ng_ref,
                o_ref, sout_ref, s_scr, tail_scr, *, chunk):
    step = pl.program_id(1)
    rb = x_ref.shape[0]
    n_chunks = rb // chunk

    @pl.when(step == 0)
    def _():
        s_scr[...] = s0_ref[0]
        tail_scr[...] = tail0_ref[0]

    x = x_ref[...]
    xfull = jnp.concatenate([tail_scr[...], x], axis=0)
    y = xfull[8:8 + rb] * cw_ref[CONV_W - 1:CONV_W, :]
    for j in range(CONV_W - 1):
        y = y + pltpu.roll(xfull, CONV_W - 1 - j, 0)[8:8 + rb] * cw_ref[j:j + 1, :]
    tail_scr[...] = xfull[rb:rb + 8]
    c = _silu(y)

    sm = sm_ref[...]
    beta_all = _sigmoid(sm)
    g_all = -jnp.exp(alog_ref[...]) * _softplus(sm + dtb_ref[...])
    rbp = ((rb + LANES - 1) // LANES) * LANES
    g_pad = g_all if rbp == rb else jnp.concatenate([g_all, jnp.zeros((rbp - rb, LANES), F32)], axis=0)
    g_t = g_pad.T[:, :rb]

    ri, ci = _iota((rb, rb), 0), _iota((rb, rb), 1)
    incl, strict, upper = ci <= ri, ci < ri, ri <= ci
    if n_chunks > 1:
        sh = int(math.log2(chunk))
        same = (ri >> sh) == (ci >> sh)
        incl, strict, upper = incl & same, strict & same, upper & same
    eye = (ri == ci).astype(F32)

    def same_blk(size):
        sh_b = int(math.log2(size))
        return (ri >> sh_b) == (ci >> sh_b)

    gcol_all = _dot_precise_rhs(incl.astype(BF16), g_all)
    grow_all = _dot_precise_lhs(g_t, upper.astype(BF16))
    gate = gate_ref[...]
    n_sq = int(math.log2(chunk))
    scale = HEAD_DIM ** -0.5

    for h in range(N_H):
        hs = slice(h * HEAD_DIM, (h + 1) * HEAD_DIM)
        qh = c[:, h * HEAD_DIM:(h + 1) * HEAD_DIM]
        kh = c[:, W_GROUP + h * HEAD_DIM:W_GROUP + (h + 1) * HEAD_DIM]
        vh = c[:, 2 * W_GROUP + h * HEAD_DIM:2 * W_GROUP + (h + 1) * HEAD_DIM]
        qh = qh * lax.rsqrt(jnp.sum(qh * qh, axis=1, keepdims=True) + NORM_EPS) * scale
        kh = kh * lax.rsqrt(jnp.sum(kh * kh, axis=1, keepdims=True) + NORM_EPS)
        bh = beta_all[:, SMALL_BETA + h:SMALL_BETA + h + 1]
        gcol = gcol_all[:, SMALL_A + h:SMALL_A + h + 1]
        grow = grow_all[SMALL_A + h:SMALL_A + h + 1, :]
        decay = jnp.where(incl, jnp.exp(jnp.where(incl, gcol - grow, 0.0)), 0.0)
        kb = kh * bh
        m = jnp.where(strict, _dot_nt(kb, kh) * decay, 0.0)
        size = min(chunk, GDN_BASE)
        pw = -jnp.where(same_blk(size), m, 0.0)
        t = eye + pw
        for _ in range(int(math.log2(size)) - 1):
            pw = _dot(pw, pw)
            t = t + _dot(t, pw)
        while size < chunk:
            off = jnp.where(jnp.logical_and(same_blk(2 * size), jnp.logical_not(same_blk(size))), m, 0.0)
            t = t - _dot(_dot(t, off), t)
            size *= 2
        u = _dot(t, vh * bh)
        w = _dot(t, kb * jnp.exp(gcol))
        attn = jnp.where(incl, _dot_nt(qh, kh) * decay, 0.0)
        qg = qh * jnp.exp(gcol)
        s_h = s_scr[h]
        vnews, inters = [], []
        for cidx in range(n_chunks):
            rows = slice(cidx * chunk, (cidx + 1) * chunk)
            vnew = u[rows] - _dot(w[rows], s_h)
            inters.append(_dot(qg[rows], s_h))
            glast = gcol[(cidx + 1) * chunk - 1:(cidx + 1) * chunk, :]
            s_h = s_h * jnp.exp(glast) + _dot_tn(kh[rows] * jnp.exp(glast - gcol[rows]), vnew)
            vnews.append(vnew)
        s_scr[h] = s_h
        vnew_all = vnews[0] if n_chunks == 1 else jnp.concatenate(vnews, axis=0)
        inter = inters[0] if n_chunks == 1 else jnp.concatenate(inters, axis=0)
        o = inter + _dot(attn, vnew_all)
        o = o * lax.rsqrt(jnp.mean(o * o, axis=1, keepdims=True) + NORM_EPS) * ng_ref[:, hs]
        o_ref[:, hs] = o * _silu(gate[:, hs])

    @pl.when(step == pl.num_programs(1) - 1)
    def _():
        sout_ref[0] = s_scr[...]


def _gdn(dqkv, small, dgate, tail0, s0, conv_w, alog_row, dtb_row, ng_row, batch, seq, chunk, rb):
    steps = seq // rb
    row = lambda w: pl.BlockSpec((rb, w), lambda b, i: (b * steps + i, 0))
    const = lambda shp: pl.BlockSpec(shp, lambda b, i: (0,) * len(shp))
    return pl.pallas_call(
        functools.partial(_gdn_kernel, chunk=chunk),
        out_shape=[jax.ShapeDtypeStruct((dqkv.shape[0], W_GROUP), F32),
                   jax.ShapeDtypeStruct((batch, N_H, HEAD_DIM, HEAD_DIM), F32)],
        grid=(batch, steps),
        in_specs=[row(3 * W_GROUP), row(LANES), row(W_GROUP),
                  pl.BlockSpec((1, 8, 3 * W_GROUP), lambda b, i: (b, 0, 0)),
                  pl.BlockSpec((1, N_H, HEAD_DIM, HEAD_DIM), lambda b, i: (b, 0, 0, 0)),
                  const((CONV_W, 3 * W_GROUP)), const((1, LANES)), const((1, LANES)), const((1, W_GROUP))],
        out_specs=[row(W_GROUP), pl.BlockSpec((1, N_H, HEAD_DIM, HEAD_DIM), lambda b, i: (b, 0, 0, 0))],
        scratch_shapes=[pltpu.VMEM((N_H, HEAD_DIM, HEAD_DIM), F32), pltpu.VMEM((8, 3 * W_GROUP), F32)],
        compiler_params=_cparams(("parallel", "arbitrary")),
        name="gdn",
    )(dqkv, small, dgate, tail0, s0, conv_w, alog_row, dtb_row, ng_row)


def _lane_row(vec, offset):
    return jnp.zeros((1, LANES), F32).at[0, offset:offset + vec.shape[0]].set(vec)


def _pad_rows(x, bs, s, rows):
    w = x.shape[-1]
    return jnp.concatenate([x.reshape(bs, s, w), jnp.zeros((bs, rows - s, w), x.dtype)], axis=1)


def kernel(x_prompt, x_sample, cache_dsa_k, cache_dsa_v, cache_dsa_kidx, cache_sb_k, cache_sb_v, state_gdn_S,
           state_gdn_conv, page_table, c_prompt, c_sample, w_cond, b_cond, w_in, w_out, ln_g, ln_b, conv_w, a_log,
           dt_bias, gdn_norm_g, gmlp_w_s, gmlp_b_s, w_ff1, w_ff2):
    bp, seq, d = x_prompt.shape
    bs, s, _ = x_sample.shape
    depth = w_in.shape[0]
    page = cache_dsa_k.shape[2]
    alpha = (2 * depth) ** 0.25
    assert seq % BLK == 0 and page == BLK and s >= CONV_W - 1 and s % 8 == 0 and s <= BLK

    c_all = jnp.concatenate([c_prompt, c_sample], axis=0)
    rc = ((c_all.shape[0] + 7) // 8) * 8
    c_all = jnp.concatenate([c_all, jnp.zeros((rc - c_all.shape[0], d), F32)], axis=0)
    mod = _cond(c_all, w_cond, b_cond)

    cache_it = jnp.transpose(cache_dsa_kidx, (0, 1, 3, 2))
    cache_akt, cache_avt, cache_ckt, cache_cvt = (jnp.transpose(t, (0, 1, 3, 4, 2))
                                                  for t in (cache_dsa_k, cache_dsa_v, cache_sb_k, cache_sb_v))

    xp = x_prompt.reshape(bp * seq, d)
    xs = x_sample.reshape(bs * s, d)
    c_gmlp = min(seq, GMLP_CHUNK)
    cs_gmlp = min(s, GMLP_CHUNK)
    st_p, st_s = [], []
    for l in range(depth):
        w_pad = _pad_w_in(w_in[l])
        wout_b, w1_b, w2_b = w_out[l].astype(BF16), w_ff1[l].astype(BF16), w_ff2[l].astype(BF16)
        alog_row, dtb_row = _lane_row(a_log[l], SMALL_A), _lane_row(dt_bias[l], SMALL_A)
        ng_row = jnp.tile(gdn_norm_g[l], N_H).reshape(1, W_GROUP)
        pre_ln = l == 0

        assert bp == 1
        mods = [mod[l, 0:1, i * d:(i + 1) * d] for i in range(6)]
        pr = _proj(xp, mods[0], mods[1], w_pad, pre_ln, 512 if (bp * seq) % 512 == 0 else BLK, True)
        out_a = _dsa_prompt(pr["aq"], pr["iq"], pr["small"], pr["ik_b"], pr["ak_b"], pr["av_t"], bp, seq)
        out_b, _ = _gmlp(pr["bu"], pr["bv"], gmlp_w_s[l][:, :c_gmlp, :c_gmlp], gmlp_b_s[l][:, :c_gmlp].T,
                         c_gmlp, c_gmlp, False)
        out_c = _sb_prompt(pr["cq"], pr["ck_b"], pr["cv_b"], bp, seq)
        gchunk = math.gcd(seq, GDN_CHUNK)
        out_d, s_new = _gdn(pr["dqkv"], pr["small"], pr["dgate"], jnp.zeros((bp, 8, 3 * W_GROUP), F32),
                            jnp.zeros((bp, N_H, HEAD_DIM, HEAD_DIM), F32), conv_w[l], alog_row, dtb_row, ng_row,
                            bp, seq, gchunk, 256 if seq % 256 == 0 else gchunk)
        xp = _outffn(xp, (out_a, out_b, out_c, out_d), mods[2], mods[3], mods[4], mods[5], wout_b, ln_g[l], ln_b[l],
                     w1_b, w2_b, pre_ln, alpha, tm=256 if (bp * seq) % 256 == 0 else BLK)
        buf_new = pr["dqkv"].reshape(bp, seq, 3 * W_GROUP)[:, seq - (CONV_W - 1):]
        st_p.append((pr["ak"].reshape(bp, seq, N_H, HEAD_DIM), pr["av"].reshape(bp, seq, N_H, HEAD_DIM),
                     pr["ik"].reshape(bp, seq, IDX_DIM), pr["ck"].reshape(bp, seq, N_H, HEAD_DIM),
                     pr["cv"].reshape(bp, seq, N_H, HEAD_DIM), s_new, buf_new))

        mods = [jnp.repeat(mod[l, bp:bp + bs, i * d:(i + 1) * d], s, axis=0) for i in range(6)]
        ps = _proj(xs, mods[0], mods[1], w_pad, pre_ln, bs * s, False)
        out_a = _dsa_sample(page_table, ps["aq"], ps["iq"], ps["small"], _pad_rows(ps["ik"], bs, s, BLK),
                            _pad_rows(ps["ak"], bs, s, BLK), _pad_rows(ps["av"], bs, s, BLK),
                            cache_it, cache_akt, cache_avt, l, bs, s)
        out_b, v_rows = _gmlp(ps["bu"], ps["bv"], jnp.tile(gmlp_w_s[l][:, :cs_gmlp, :cs_gmlp], (1, bs, bs)),
                              jnp.tile(gmlp_b_s[l][:, :cs_gmlp].T, (bs, 1)), bs * s, cs_gmlp, True)
        out_c = _sb_sample(page_table, ps["cq"], _pad_rows(ps["ck"], bs, s, BLK), _pad_rows(ps["cv"], bs, s, BLK),
                           cache_ckt, cache_cvt, l, bs, s)
        tail0 = jnp.concatenate([jnp.zeros((bs, 8 - (CONV_W - 1), 3 * W_GROUP), F32), state_gdn_conv[l]], axis=1)
        out_d, s_new = _gdn(ps["dqkv"], ps["small"], ps["dgate"], tail0, state_gdn_S[l], conv_w[l], alog_row,
                            dtb_row, ng_row, bs, s, math.gcd(s, GDN_CHUNK), s)
        xs = _outffn(xs, (out_a, out_b, out_c, out_d), mods[2], mods[3], mods[4], mods[5], wout_b, ln_g[l], ln_b[l],
                     w1_b, w2_b, pre_ln, alpha, tm=bs * s)
        buf_new = ps["dqkv"].reshape(bs, s, 3 * W_GROUP)[:, s - (CONV_W - 1):]
        st_s.append((ps["ak"].reshape(bs, s, N_H, HEAD_DIM), ps["av"].reshape(bs, s, N_H, HEAD_DIM),
                     ps["ik"].reshape(bs, s, IDX_DIM), ps["ck"].reshape(bs, s, N_H, HEAD_DIM),
                     ps["cv"].reshape(bs, s, N_H, HEAD_DIM), s_new, buf_new, v_rows.reshape(bs, s, W_GROUP)))

    outs_p = tuple(jnp.stack(t) for t in zip(*st_p))
    outs_s = tuple(jnp.stack(t) for t in zip(*st_s))
    return (xp.reshape(bp, seq, d), xs.reshape(bs, s, d)) + outs_p + outs_s
```

```python
import functools
import math

import jax
import jax.numpy as jnp
from jax import lax
from jax.experimental import pallas as pl
from jax.experimental.pallas import tpu as pltpu

F32 = jnp.float32
BF16 = jnp.bfloat16
I32 = jnp.int32
I16 = jnp.int16
I16_MIN = -2 ** 15

LN_EPS = 1e-5
NORM_EPS = 1e-6
N_H = 4
HEAD_DIM = 64
W_GROUP = N_H * HEAD_DIM
IDX_HEADS = 4
IDX_DIM = 64
TOPK_MAX = 256
GMLP_CHUNK = 128
GDN_CHUNK = 64
CONV_W = 4
BLK = 128
LANES = 128
VMEM_LIMIT = 56 * 1024 * 1024
SB_ZERO_TAIL = -110.0
INT_MIN = -2 ** 31
LOG2E = 1.4426950408889634
MASKED = -1e30
M_INIT = -1e20
UNROLL = 4
PAGE_UNROLL = 4
SB_RING = 3
PAGE_RING = 8
COUNT_UNROLL = 4
GDN_BASE = 8

PIECES = (("dqkv", 768, 768), ("aq", 256, 256), ("ak", 256, 256), ("av", 256, 256), ("iq", 256, 256),
          ("bu", 256, 256), ("bv", 256, 256), ("cq", 256, 256), ("ck", 256, 256), ("cv", 256, 256),
          ("dgate", 256, 256), ("ik", 64, 128), ("small", 12, 128))
N_PAD = sum(p[2] for p in PIECES)
SMALL_IW, SMALL_BETA, SMALL_A = 0, 4, 8
BF16_COPIES = ("ak", "ck", "cv", "ik")


def _cparams(sem, vmem=VMEM_LIMIT):
    return pltpu.CompilerParams(dimension_semantics=sem, vmem_limit_bytes=vmem)


def _ln_plain(x):
    mu = jnp.mean(x, axis=-1, keepdims=True)
    xc = x - mu
    var = jnp.mean(xc * xc, axis=-1, keepdims=True)
    return xc * lax.rsqrt(var + LN_EPS)


def _sigmoid(x):
    return 1.0 / (1.0 + jnp.exp(-x))


def _silu(x):
    return x * _sigmoid(x)


def _softplus(x):
    return jnp.maximum(x, 0.0) + jnp.log1p(jnp.exp(-jnp.abs(x)))


def _gelu_tanh(x):
    c = math.sqrt(2.0 / math.pi)
    return x * (0.5 * (1.0 + jnp.tanh(c * (x + 0.044715 * (x * x * x)))))


def _dot(a, b):
    return jnp.dot(a.astype(BF16), b.astype(BF16), preferred_element_type=F32)


def _dot_nt(a, b):
    return lax.dot_general(a.astype(BF16), b.astype(BF16), (((1,), (1,)), ((), ())),
                           preferred_element_type=F32)


def _dot_tn(a, b):
    return lax.dot_general(a.astype(BF16), b.astype(BF16), (((0,), (0,)), ((), ())),
                           preferred_element_type=F32)


def _split(x):
    hi = x.astype(BF16)
    lo = (x - hi.astype(F32)).astype(BF16)
    return hi, lo


def _dot_precise_lhs(a, b01):
    hi, lo = _split(a)
    return jnp.dot(hi, b01, preferred_element_type=F32) + jnp.dot(lo, b01, preferred_element_type=F32)


def _dot_precise_rhs(a01, b):
    hi, lo = _split(b)
    return jnp.dot(a01, hi, preferred_element_type=F32) + jnp.dot(a01, lo, preferred_element_type=F32)


def _dot3(a, b):
    ah, al = _split(a)
    bh, bl = _split(b)
    return (jnp.dot(ah, bh, preferred_element_type=F32) + jnp.dot(al, bh, preferred_element_type=F32)
            + jnp.dot(ah, bl, preferred_element_type=F32))


def _iota(shape, dim):
    return lax.broadcasted_iota(I32, shape, dim)


def _sort_key(score):
    bits = pltpu.bitcast(score, I32)
    return bits ^ ((bits >> 31) & jnp.int32(0x7FFFFFFF))


def _alibi_slope(h):
    return 2.0 ** (-8.0 * (h + 1) / N_H)


def _heads(x):
    return [x[:, h * HEAD_DIM:(h + 1) * HEAD_DIM] for h in range(N_H)]


def _cond_kernel(c_ref, w_ref, b_ref, o_ref):
    s = _silu(c_ref[...])
    o_ref[0] = _dot3(s, w_ref[0]) + b_ref[0]


def _cond(c_all, w_cond, b_cond):
    depth, d, n6 = w_cond.shape
    rc = c_all.shape[0]
    tn = 1536 if n6 % 1536 == 0 else n6
    return pl.pallas_call(
        _cond_kernel,
        out_shape=jax.ShapeDtypeStruct((depth, rc, n6), F32),
        grid=(depth, n6 // tn),
        in_specs=[pl.BlockSpec((rc, d), lambda l, j: (0, 0)),
                  pl.BlockSpec((1, d, tn), lambda l, j: (l, 0, j)),
                  pl.BlockSpec((1, 1, tn), lambda l, j: (l, 0, j))],
        out_specs=pl.BlockSpec((1, rc, tn), lambda l, j: (l, 0, j)),
        compiler_params=_cparams(("arbitrary", "arbitrary")),
        name="cond",
    )(c_all, w_cond, b_cond.reshape(depth, 1, n6))


def _proj_kernel(x_ref, sh_ref, sc_ref, w_ref, *o_refs, pre_ln, extras):
    x = x_ref[...]
    if pre_ln:
        x = _ln_plain(x)
    hb = (x * (1.0 + sc_ref[...]) + sh_ref[...]).astype(BF16)
    col = 0
    for (name, width, padded), o_ref in zip(PIECES, o_refs[:len(PIECES)]):
        res = jnp.dot(hb, w_ref[:, col:col + padded], preferred_element_type=F32)
        o_ref[...] = res if o_ref.shape[-1] == padded else res[:, :o_ref.shape[-1]]
        if extras and name in BF16_COPIES:
            o_refs[len(PIECES) + BF16_COPIES.index(name)][...] = res.astype(BF16)
        if extras and name == "av":
            vt_ref = o_refs[len(PIECES) + len(BF16_COPIES)]
            for j in range(vt_ref.shape[0]):
                vt_ref[j] = res[j * BLK:(j + 1) * BLK, :].T.astype(BF16)
        col += padded


def _mod_spec(m, tm, d):
    if m.shape[0] == 1:
        return pl.BlockSpec((1, d), lambda i: (0, 0))
    return pl.BlockSpec((tm, d), lambda i: (i, 0))


def _proj(x, shift, scale, w_pad, pre_ln, tm, extras):
    r, d = x.shape
    out_shapes, out_specs = [], []
    for name, width, padded in PIECES:
        w_out = width if name == "ik" else padded
        out_shapes.append(jax.ShapeDtypeStruct((r, w_out), F32))
        out_specs.append(pl.BlockSpec((tm, w_out), lambda i: (i, 0)))
    if extras:
        for name in BF16_COPIES:
            w_out = LANES if name == "ik" else W_GROUP
            out_shapes.append(jax.ShapeDtypeStruct((r, w_out), BF16))
            out_specs.append(pl.BlockSpec((tm, w_out), lambda i: (i, 0)))
        out_shapes.append(jax.ShapeDtypeStruct((r // BLK, W_GROUP, BLK), BF16))
        out_specs.append(pl.BlockSpec((tm // BLK, W_GROUP, BLK), lambda i: (i, 0, 0)))
    outs = pl.pallas_call(
        functools.partial(_proj_kernel, pre_ln=pre_ln, extras=extras),
        out_shape=out_shapes,
        grid=(r // tm,),
        in_specs=[pl.BlockSpec((tm, d), lambda i: (i, 0)), _mod_spec(shift, tm, d), _mod_spec(scale, tm, d),
                  pl.BlockSpec((d, N_PAD), lambda i: (0, 0))],
        out_specs=out_specs,
        compiler_params=_cparams(("parallel",)),
        name="proj",
    )(x, shift, scale, w_pad)
    res = {name: o for (name, _, _), o in zip(PIECES, outs)}
    if extras:
        for i, name in enumerate(BF16_COPIES):
            res[name + "_b"] = outs[len(PIECES) + i]
        res["av_t"] = outs[len(PIECES) + len(BF16_COPIES)]
    return res


def _pad_w_in(w_in_l):
    off = {}
    start = 0
    for name, n in (("aq", 256), ("ak", 256), ("av", 256), ("iq", 256), ("ik", 64), ("iw", 4), ("bu", 256),
                    ("bv", 256), ("cq", 256), ("ck", 256), ("cv", 256), ("dqkv", 768), ("dbeta", 4), ("da", 4),
                    ("dgate", 256)):
        off[name] = (start, n)
        start += n
    d = w_in_l.shape[0]
    cols = []
    for name, width, padded in PIECES:
        if name == "small":
            parts = [w_in_l[:, off[k][0]:off[k][0] + off[k][1]] for k in ("iw", "dbeta", "da")]
            piece = jnp.concatenate(parts, axis=1)
        else:
            piece = w_in_l[:, off[name][0]:off[name][0] + off[name][1]]
        if padded > piece.shape[1]:
            piece = jnp.concatenate([piece, jnp.zeros((d, padded - piece.shape[1]), piece.dtype)], axis=1)
        cols.append(piece)
    return jnp.concatenate(cols, axis=1).astype(BF16)


def _outffn_kernel(x_ref, oa_ref, ob_ref, oc_ref, od_ref, gm_ref, shf_ref, scf_ref, gf_ref,
                   wout_ref, lng_ref, lnb_ref, w1_ref, w2_ref, o_ref, *, pre_ln, alpha):
    x = x_ref[...]
    if pre_ln:
        x = _ln_plain(x)
    mo = None
    for g, r in enumerate((oa_ref, ob_ref, oc_ref, od_ref)):
        part = _dot(r[...], wout_ref[g * W_GROUP:(g + 1) * W_GROUP, :])
        mo = part if mo is None else mo + part
    x1 = _ln_plain(alpha * x + gm_ref[...] * mo) * lng_ref[0:1, :] + lnb_ref[0:1, :]
    hf = x1 * (1.0 + scf_ref[...]) + shf_ref[...]
    a = jnp.maximum(_dot(hf, w1_ref[...]), 0.0)
    ff = _dot(a * a, w2_ref[...])
    o_ref[...] = _ln_plain(alpha * x1 + gf_ref[...] * ff) * lng_ref[1:2, :] + lnb_ref[1:2, :]


def _outffn(x, branches, gm, shf, scf, gf, wout_b, ln_g, ln_b, w1_b, w2_b, pre_ln, alpha, tm):
    r, d = x.shape
    dff = w1_b.shape[1]
    row = lambda w: pl.BlockSpec((tm, w), lambda i: (i, 0))
    const = lambda shp: pl.BlockSpec(shp, lambda i: (0, 0), pipeline_mode=pl.Buffered(1))
    return pl.pallas_call(
        functools.partial(_outffn_kernel, pre_ln=pre_ln, alpha=alpha),
        out_shape=jax.ShapeDtypeStruct((r, d), F32),
        grid=(r // tm,),
        in_specs=[row(d)] + [row(W_GROUP)] * 4 + [_mod_spec(m, tm, d) for m in (gm, shf, scf, gf)]
        + [const((d, d)), const((2, d)), const((2, d)), const((d, dff)), const((dff, d))],
        out_specs=row(d),
        compiler_params=_cparams(("parallel",)),
        name="outffn",
    )(x, *branches, gm, shf, scf, gf, wout_b, ln_g, ln_b, w1_b, w2_b)


def _gmlp_kernel(u_ref, v_ref, w_ref, bt_ref, ob_ref, *vn_refs, chunk):
    u = _gelu_tanh(u_ref[...])
    vn = _ln_plain(_gelu_tanh(v_ref[...]))
    if vn_refs:
        vn_refs[0][...] = vn
    rows = u.shape[0]
    ri, ci = _iota((rows, rows), 0), _iota((rows, rows), 1)
    mask = ci <= ri
    if rows != chunk:
        sh = int(math.log2(chunk))
        mask = mask & ((ri >> sh) == (ci >> sh))
    vb = vn.astype(BF16)
    for g in range(N_H):
        w = jnp.where(mask, w_ref[g], 0.0).astype(BF16)
        mixed = jnp.dot(w, vb[:, g * HEAD_DIM:(g + 1) * HEAD_DIM], preferred_element_type=F32)
        mixed = mixed + bt_ref[:, g:g + 1]
        ob_ref[:, g * HEAD_DIM:(g + 1) * HEAD_DIM] = u[:, g * HEAD_DIM:(g + 1) * HEAD_DIM] * mixed


def _gmlp(bu, bv, w_tiled, bt_tiled, rows, chunk, want_vn):
    r = bu.shape[0]
    row = pl.BlockSpec((rows, W_GROUP), lambda i: (i, 0))
    out_shape = [jax.ShapeDtypeStruct((r, W_GROUP), F32)]
    out_specs = [row]
    if want_vn:
        out_shape.append(jax.ShapeDtypeStruct((r, W_GROUP), F32))
        out_specs.append(row)
    outs = pl.pallas_call(
        functools.partial(_gmlp_kernel, chunk=chunk),
        out_shape=out_shape,
        grid=(r // rows,),
        in_specs=[row, row, pl.BlockSpec((N_H, rows, rows), lambda i: (0, 0, 0)),
                  pl.BlockSpec((rows, N_H), lambda i: (0, 0))],
        out_specs=out_specs,
        compiler_params=_cparams(("parallel",)),
        name="gmlp",
    )(bu, bv, w_tiled, bt_tiled)
    return outs if want_vn else (outs[0], None)


def _sb_block(qk, pv, vis, carry, acc, upper):
    new_carry, new_acc = [], []
    for h in range(N_H):
        z = qk(h)
        l_raw = -_softplus(z)
        l_vis = l_raw if vis is None else jnp.where(vis, l_raw, 0.0)
        tail = _dot_precise_lhs(l_vis, upper) + carry[h]
        w = jnp.exp(z + l_raw + tail)
        if vis is not None:
            w = jnp.where(vis, w, 0.0)
        new_acc.append(acc[h] + pv(h, w))
        new_carry.append(carry[h] + jnp.sum(l_vis, axis=1, keepdims=True))
    return new_carry, new_acc


def _sb_live(carry):
    m = carry[0]
    for c in carry[1:]:
        m = jnp.maximum(m, c)
    return (jnp.max(m) > SB_ZERO_TAIL).astype(I32)


def _upper_ones():
    return (_iota((BLK, BLK), 0) > _iota((BLK, BLK), 1)).astype(BF16)


def _natural_kv(q_heads, k, v):
    kh, vh = _heads(k), _heads(v)
    return (lambda h: _dot_nt(q_heads[h], kh[h])), (lambda h, w: _dot(w, vh[h]))


def _sbp_kernel(q_ref, k_ref, v_ref, o_ref):
    qb = pl.program_id(1)
    q_heads = _heads((q_ref[...] * (HEAD_DIM ** -0.5)).astype(BF16))
    upper = _upper_ones()
    ri, ci = _iota((BLK, BLK), 0), _iota((BLK, BLK), 1)
    zero_c = [jnp.zeros((BLK, 1), F32)] * N_H
    zero_a = [jnp.zeros((BLK, HEAD_DIM), F32)] * N_H
    off = pl.multiple_of(qb * BLK, BLK)
    qk, pv = _natural_kv(q_heads, k_ref[pl.ds(off, BLK), :], v_ref[pl.ds(off, BLK), :])
    carry, acc = _sb_block(qk, pv, ci < ri, zero_c, zero_a, upper)

    def cond(st):
        return jnp.logical_and(st[0] >= 0, st[1] > 0)

    def body(st):
        kb = st[0]
        o = pl.multiple_of(kb * BLK, BLK)
        qk, pv = _natural_kv(q_heads, k_ref[pl.ds(o, BLK), :], v_ref[pl.ds(o, BLK), :])
        c, a = _sb_block(qk, pv, None, list(st[2:2 + N_H]), list(st[2 + N_H:]), upper)
        return (kb - 1, _sb_live(c), *c, *a)

    st = lax.while_loop(cond, body, (qb - 1, _sb_live(carry), *carry, *acc))
    for h in range(N_H):
        o_ref[:, h * HEAD_DIM:(h + 1) * HEAD_DIM] = st[2 + N_H + h]


def _sb_prompt(cq, ck_b, cv_b, batch, seq):
    nq = seq // BLK
    full = pl.BlockSpec((seq, W_GROUP), lambda b, i: (b, 0), pipeline_mode=pl.Buffered(1))
    return pl.pallas_call(
        _sbp_kernel,
        out_shape=jax.ShapeDtypeStruct(cq.shape, F32),
        grid=(batch, nq),
        in_specs=[pl.BlockSpec((BLK, W_GROUP), lambda b, i: (b * nq + i, 0)), full, full],
        out_specs=pl.BlockSpec((BLK, W_GROUP), lambda b, i: (b * nq + i, 0)),
        compiler_params=_cparams(("parallel", "arbitrary")),
        name="sb_prompt",
    )(cq, ck_b, cv_b)


def _sbs_kernel(pt_ref, q_ref, kn_ref, vn_ref, kc_ref, vc_ref, o_ref, kbuf, vbuf, sem, *, layer, n_pages):
    b = pl.program_id(0)
    rows = q_ref.shape[0]
    q_heads = _heads((q_ref[...] * (HEAD_DIM ** -0.5)).astype(BF16))
    upper = _upper_ones()
    ri, ci = _iota((rows, BLK), 0), _iota((rows, BLK), 1)
    zero_c = [jnp.zeros((rows, 1), F32)] * N_H
    zero_a = [jnp.zeros((rows, HEAD_DIM), F32)] * N_H
    def copies(p):
        page, slot = pt_ref[b, p], p % SB_RING
        return (pltpu.make_async_copy(kc_ref.at[layer, page], kbuf.at[slot], sem.at[0, slot]),
                pltpu.make_async_copy(vc_ref.at[layer, page], vbuf.at[slot], sem.at[1, slot]))

    def start(p):
        for c in copies(p):
            c.start()

    def wait(p):
        for c in copies(p):
            c.wait()

    for j in range(1, min(SB_RING, n_pages + 1)):
        start(n_pages - j)
    qk, pv = _natural_kv(q_heads, kn_ref[0], vn_ref[0])
    carry, acc = _sb_block(qk, pv, ci < ri, zero_c, zero_a, upper)

    def cond(st):
        return jnp.logical_and(st[0] >= 0, st[1] > 0)

    def body(st):
        p = st[0]
        slot = p % SB_RING
        wait(p)

        @pl.when(p - (SB_RING - 1) >= 0)
        def _():
            start(p - (SB_RING - 1))

        c, a = _sb_block(lambda h: _dot(q_heads[h], kbuf[slot, h]), lambda h, w: _dot_nt(w, vbuf[slot, h]), None,
                         list(st[2:2 + N_H]), list(st[2 + N_H:]), upper)
        return (p - 1, _sb_live(c), *c, *a)

    st = lax.while_loop(cond, body, (jnp.int32(n_pages - 1), _sb_live(carry), *carry, *acc))
    for j in range(SB_RING - 1):
        @pl.when(st[0] - j >= 0)
        def _():
            wait(st[0] - j)

    for h in range(N_H):
        o_ref[:, h * HEAD_DIM:(h + 1) * HEAD_DIM] = st[2 + N_H + h]


def _sb_sample(page_table, cq, ck_new_pad, cv_new_pad, cache_kt, cache_vt, layer, bs, s):
    n_pages = page_table.shape[1]
    page = cache_kt.shape[-1]
    return pl.pallas_call(
        functools.partial(_sbs_kernel, layer=layer, n_pages=n_pages),
        out_shape=jax.ShapeDtypeStruct(cq.shape, F32),
        grid_spec=pltpu.PrefetchScalarGridSpec(
            num_scalar_prefetch=1,
            grid=(bs,),
            in_specs=[pl.BlockSpec((s, W_GROUP), lambda b, pt: (b, 0)),
                      pl.BlockSpec((1, BLK, W_GROUP), lambda b, pt: (b, 0, 0)),
                      pl.BlockSpec((1, BLK, W_GROUP), lambda b, pt: (b, 0, 0)),
                      pl.BlockSpec(memory_space=pl.ANY), pl.BlockSpec(memory_space=pl.ANY)],
            out_specs=pl.BlockSpec((s, W_GROUP), lambda b, pt: (b, 0)),
            scratch_shapes=[pltpu.VMEM((SB_RING, N_H, HEAD_DIM, page), F32),
                            pltpu.VMEM((SB_RING, N_H, HEAD_DIM, page), F32),
                            pltpu.SemaphoreType.DMA((2, SB_RING))]),
        compiler_params=_cparams(("arbitrary",)),
        name="sb_sample",
    )(page_table, cq, ck_new_pad, cv_new_pad, cache_kt, cache_vt)


def _kth_largest_key(count_ge, shape, k):
    kf = jnp.float32(k)
    t = jnp.where(count_ge(jnp.zeros(shape, I32)) >= kf, jnp.int32(0), jnp.int32(INT_MIN))

    def body(i, t):
        cand = t + (jnp.int32(1) << (30 - i))
        return jnp.where(count_ge(cand) >= kf, cand, t)

    return lax.fori_loop(0, 31, body, t)


def _kth_largest_i16(count_ge, shape, k):
    t = jnp.where(count_ge(jnp.zeros(shape, I32)) >= k, jnp.int32(0), jnp.int32(I16_MIN))

    def body(i, t):
        cand = t + (jnp.int32(1) << (14 - i))
        return jnp.where(count_ge(cand) >= k, cand, t)

    return lax.fori_loop(0, 15, body, t)


def _dsap_kernel(aq_ref, iq_ref, sm_ref, ik_ref, k_ref, vt_ref, o_ref, key_ref, hi_ref, lo_ref, *, topk):
    qb = pl.program_id(1)
    nkb = qb + 1
    ri, ci = _iota((BLK, BLK), 0), _iota((BLK, BLK), 1)
    diag_ok = ri <= ci

    iq_t = (iq_ref[...] * (IDX_DIM ** -0.5)).T
    iq_all = jnp.concatenate([iq_t[h * IDX_DIM:(h + 1) * IDX_DIM] for h in range(IDX_HEADS)],
                             axis=1).astype(BF16)
    sm_t = sm_ref[...].T
    iw = [sm_t[SMALL_IW + h:SMALL_IW + h + 1, :] * (IDX_HEADS ** -0.5) for h in range(IDX_HEADS)]

    def score_block(kb, diagonal):
        o = pl.multiple_of(kb * BLK, BLK)
        kidx = ik_ref[pl.ds(o, BLK), :][:, :IDX_DIM]
        s_all = jnp.dot(kidx, iq_all, preferred_element_type=F32)
        score = iw[0] * jnp.maximum(s_all[:, 0:BLK], 0.0)
        for h in range(1, IDX_HEADS):
            score = score + iw[h] * jnp.maximum(s_all[:, h * BLK:(h + 1) * BLK], 0.0)
        if diagonal:
            score = jnp.where(diag_ok, score, -jnp.inf)
        key = _sort_key(score)
        key_ref[kb] = key
        hi_ref[kb] = (key >> 16).astype(I16)
        lo_ref[kb] = ((key & 0xFFFF) + I16_MIN).astype(I16)

    def score_one(kb, _):
        score_block(kb, False)
        return 0

    def score_group(g, _):
        for u in range(UNROLL):
            score_block(g * UNROLL + u, False)
        return 0

    lax.fori_loop(0, qb // UNROLL, score_group, 0)
    lax.fori_loop((qb // UNROLL) * UNROLL, qb, score_one, 0)
    score_block(qb, True)

    def count_blocks(ref, pred, dtype):
        def one(kb, cnt):
            return cnt + jnp.where(pred(ref[kb]), jnp.ones((), dtype), jnp.zeros((), dtype))

        def group(g, cnt):
            for u in range(COUNT_UNROLL):
                cnt = one(g * COUNT_UNROLL + u, cnt)
            return cnt

        groups = nkb // COUNT_UNROLL
        cnt = lax.fori_loop(0, groups, group, jnp.zeros((BLK, BLK), dtype))
        cnt = lax.fori_loop(groups * COUNT_UNROLL, nkb, one, cnt)
        return jnp.sum(cnt.astype(F32), axis=0, keepdims=True)

    def count(pred):
        return count_blocks(key_ref, pred, I32)

    def count16(ref, pred):
        return count_blocks(ref, pred, I16)

    kf = jnp.float32(topk)
    t_hi = _kth_largest_i16(lambda c: count16(hi_ref, lambda v: v >= c.astype(I16)), (1, BLK), kf)
    t_hi16 = t_hi.astype(I16)
    k_lo = kf - count16(hi_ref, lambda v: v > t_hi16)

    def keep_low(kb, _):
        lo_ref[kb] = jnp.where(hi_ref[kb] == t_hi16, lo_ref[kb], jnp.int16(I16_MIN))
        return 0

    lax.fori_loop(0, nkb, keep_low, 0)
    t_lo = _kth_largest_i16(lambda c: count16(lo_ref, lambda v: v >= c.astype(I16)), (1, BLK), k_lo)
    thr = (t_hi << 16) + (t_lo - I16_MIN)
    n_gt = count(lambda key: key > thr)
    n_eq = count(lambda key: key == thr)
    room = jnp.float32(topk) - n_gt
    tie_break = jnp.max(n_eq - room) > 0.0

    aq_t = (aq_ref[...] * (HEAD_DIM ** -0.5 * LOG2E)).T
    bias = [(_alibi_slope(h) * LOG2E) * ri.astype(F32) for h in range(N_H)]
    zeros = jnp.zeros((HEAD_DIM, BLK), F32)
    q_bd = jnp.concatenate(
        [jnp.concatenate([aq_t[g * HEAD_DIM:(g + 1) * HEAD_DIM] if g == h else zeros for g in range(N_H)], axis=0)
         for h in range(N_H)], axis=1).astype(BF16)
    strict_lower = (ci < ri).astype(BF16)

    def attend(ties, kb, diagonal, st):
        seen = st[0]
        key = key_ref[kb]
        if ties:
            eq = key == thr
            rank = seen + jnp.dot(strict_lower, eq.astype(BF16), preferred_element_type=F32)
            sel = jnp.logical_or(key > thr, jnp.logical_and(eq, rank < room))
            seen = seen + jnp.sum(eq.astype(F32), axis=0, keepdims=True)
        else:
            sel = key >= thr
        if diagonal:
            sel = jnp.logical_and(sel, diag_ok)
        blk_off = ((kb - qb) * BLK).astype(F32)
        o = pl.multiple_of(kb * BLK, BLK)
        kblk = k_ref[pl.ds(o, BLK), :]
        half = W_GROUP // 2
        s_pair = [jnp.dot(kblk[:, g * half:(g + 1) * half], q_bd[g * half:(g + 1) * half, g * 2 * BLK:(g + 1) * 2 * BLK],
                          preferred_element_type=F32) for g in range(2)]
        vt = vt_ref[kb]
        out = [seen]
        for h in range(N_H):
            m_old, l_old, a_old = st[1 + 3 * h], st[2 + 3 * h], st[3 + 3 * h]
            c_blk = (_alibi_slope(h) * LOG2E) * blk_off
            s = jnp.where(sel, s_pair[h // 2][:, (h % 2) * BLK:(h % 2 + 1) * BLK] + bias[h], MASKED)
            m_new = jnp.maximum(m_old, jnp.max(s, axis=0, keepdims=True) + c_blk)
            p = jnp.exp2(s - (m_new - c_blk))
            alpha = jnp.exp2(m_old - m_new)
            pv = jnp.dot(vt[h * HEAD_DIM:(h + 1) * HEAD_DIM, :], p.astype(BF16), preferred_element_type=F32)
            out += [m_new, alpha * l_old + jnp.sum(p, axis=0, keepdims=True), alpha * a_old + pv]
        return tuple(out)

    init = [jnp.zeros((1, BLK), F32)]
    for h in range(N_H):
        init += [jnp.full((1, BLK), M_INIT, F32), jnp.zeros((1, BLK), F32), jnp.zeros((HEAD_DIM, BLK), F32)]
    def sweep(ties):
        def one(kb, st):
            return attend(ties, kb, False, st)

        def group(g, st):
            for u in range(UNROLL):
                st = one(g * UNROLL + u, st)
            return st

        groups = qb // UNROLL
        st = lax.fori_loop(0, groups, group, tuple(init))
        st = lax.fori_loop(groups * UNROLL, qb, one, st)
        return attend(ties, qb, True, st)

    st = lax.cond(tie_break, lambda: sweep(True), lambda: sweep(False))
    out_t = jnp.concatenate([st[3 + 3 * h] / st[2 + 3 * h] for h in range(N_H)], axis=0)
    o_ref[...] = out_t.T


def _dsa_prompt(aq, iq, small, ik_b, ak_b, av_t, batch, seq):
    nq = seq // BLK
    topk = min(TOPK_MAX, seq // 4)
    blk = lambda w: pl.BlockSpec((BLK, w), lambda b, i: (b * nq + i, 0))
    full = lambda w: pl.BlockSpec((seq, w), lambda b, i: (b, 0), pipeline_mode=pl.Buffered(1))
    return pl.pallas_call(
        functools.partial(_dsap_kernel, topk=topk),
        out_shape=jax.ShapeDtypeStruct(aq.shape, F32),
        grid=(batch, nq),
        in_specs=[blk(W_GROUP), blk(W_GROUP), blk(LANES), full(LANES), full(W_GROUP),
                  pl.BlockSpec((nq, W_GROUP, BLK), lambda b, i: (b, 0, 0), pipeline_mode=pl.Buffered(1))],
        out_specs=blk(W_GROUP),
        scratch_shapes=[pltpu.VMEM((nq, BLK, BLK), I32), pltpu.VMEM((nq, BLK, BLK), I16),
                        pltpu.VMEM((nq, BLK, BLK), I16)],
        compiler_params=_cparams(("parallel", "arbitrary")),
        name="dsa_prompt",
    )(aq, iq, small, ik_b, ak_b, av_t)


def _dsas_kernel(pt_ref, aq_ref, iq_ref, sm_ref, ikn_ref, akn_ref, avn_ref, ic_ref, kc_ref, vc_ref, o_ref,
                 key_ref, s_ref, ibuf, kbuf, vbuf, sem, *, layer, n_pages, topk):
    b = pl.program_id(0)
    s_rows = aq_ref.shape[0]
    past = n_pages * BLK
    ri, ci = _iota((s_rows, BLK), 0), _iota((s_rows, BLK), 1)
    new_valid = jnp.logical_and(ci <= ri, ci < s_rows)

    iq = iq_ref[...] * (IDX_DIM ** -0.5)
    iq_all = jnp.concatenate(_heads(iq), axis=0).astype(BF16)
    iw = sm_ref[...][:, SMALL_IW:SMALL_IW + IDX_HEADS] * (IDX_HEADS ** -0.5)

    def idx_copy(p, slot):
        return pltpu.make_async_copy(ic_ref.at[layer, pt_ref[b, p]], ibuf.at[slot], sem.at[0, slot])

    def k_copy(p, slot):
        return pltpu.make_async_copy(kc_ref.at[layer, pt_ref[b, p]], kbuf.at[slot], sem.at[1, slot])

    def v_copy(p, slot):
        return pltpu.make_async_copy(vc_ref.at[layer, pt_ref[b, p]], vbuf.at[slot], sem.at[2, slot])

    def score_keys(s_all, valid):
        score = iw[:, 0:1] * jnp.maximum(s_all[0:s_rows], 0.0)
        for h in range(1, IDX_HEADS):
            score = score + iw[:, h:h + 1] * jnp.maximum(s_all[h * s_rows:(h + 1) * s_rows], 0.0)
        if valid is not None:
            score = jnp.where(valid, score, -jnp.inf)
        return _sort_key(score)

    u_n = PAGE_UNROLL
    trips = n_pages // u_n

    def prefetch(copy):
        for g in range(min(PAGE_RING - 1, trips)):
            for u in range(u_n):
                copy(g * u_n + u, g * u_n + u).start()

    def stream(copy, body, carry):
        def trip(g, carry):
            ahead = g + PAGE_RING - 1

            @pl.when(ahead < trips)
            def _():
                for u in range(u_n):
                    copy(ahead * u_n + u, (ahead % PAGE_RING) * u_n + u).start()

            base = (g % PAGE_RING) * u_n
            for u in range(u_n):
                copy(g * u_n + u, base + u).wait()
            for u in range(u_n):
                carry = body(g * u_n + u, base + u, carry)
            return carry

        return lax.fori_loop(0, trips, trip, carry)

    prefetch(idx_copy)
    prefetch(k_copy)
    prefetch(v_copy)

    def p1(p, slot, _):
        key_ref[p] = score_keys(_dot(iq_all, ibuf[slot]), None)
        return 0

    stream(idx_copy, p1, 0)
    key_ref[n_pages] = score_keys(_dot_nt(iq_all, ikn_ref[0]), new_valid)

    def count(pred):
        per_trip = math.gcd(n_pages, 16)

        def body(g, cnt):
            for u in range(per_trip):
                cnt = cnt + jnp.where(pred(key_ref[g * per_trip + u]), 1, 0)
            return cnt
        cnt = lax.fori_loop(0, n_pages // per_trip, body, jnp.zeros((s_rows, BLK), I32))
        cnt = cnt + jnp.where(pred(key_ref[n_pages]), 1, 0)
        return jnp.sum(cnt.astype(F32), axis=1, keepdims=True)

    thr = _kth_largest_key(lambda cand: count(lambda key: key >= cand), (s_rows, 1), topk)
    n_gt = count(lambda key: key > thr)
    n_eq = count(lambda key: key == thr)
    room = jnp.float32(topk) - n_gt
    tie_break = jnp.max(n_eq - room) > 0.0

    q_heads = _heads((aq_ref[...] * (HEAD_DIM ** -0.5)).astype(BF16))
    lower = (_iota((BLK, BLK), 0) < _iota((BLK, BLK), 1)).astype(BF16)

    def logits_block(ties, p, qk, valid, kpos0, st):
        seen = st[0]
        key = key_ref[p]
        if ties:
            eq = key == thr
            rank = seen + jnp.dot(eq.astype(BF16), lower, preferred_element_type=F32)
            sel = jnp.logical_or(key > thr, jnp.logical_and(eq, rank < room))
            seen = seen + jnp.sum(eq.astype(F32), axis=1, keepdims=True)
        else:
            sel = key >= thr
        if valid is not None:
            sel = jnp.logical_and(sel, valid)
        dist = ((past - kpos0) + ri - ci).astype(F32)
        out = [seen]
        for h in range(N_H):
            s = jnp.where(sel, qk(h) - _alibi_slope(h) * dist, MASKED)
            s_ref[p, h] = s
            out.append(jnp.maximum(st[1 + h], s))
        return tuple(out)

    qk_new, pv_new = _natural_kv(q_heads, akn_ref[0], avn_ref[0])

    def pass_a(ties):
        st = stream(k_copy,
                    lambda p, slot, st: logits_block(ties, p, lambda h: _dot(q_heads[h], kbuf[slot, h]), None,
                                                     p * BLK, st),
                    (jnp.zeros((s_rows, 1), F32),) + (jnp.full((s_rows, BLK), MASKED, F32),) * N_H)
        return logits_block(ties, n_pages, qk_new, new_valid, past, st)[1:]

    run_max = lax.cond(tie_break, lambda: pass_a(True), lambda: pass_a(False))
    m_fin = [jnp.max(run_max[h], axis=1, keepdims=True) for h in range(N_H)]

    def values_block(p, pv, st):
        out_l, out_a = [], []
        for h in range(N_H):
            pr = jnp.exp(s_ref[p, h] - m_fin[h])
            out_l.append(st[h] + pr)
            out_a.append(st[N_H + h] + pv(h, pr))
        return tuple(out_l + out_a)

    st = stream(v_copy,
                lambda p, slot, st: values_block(p, lambda h, pr: _dot_nt(pr, vbuf[slot, h]), st),
                (jnp.zeros((s_rows, BLK), F32),) * N_H + (jnp.zeros((s_rows, HEAD_DIM), F32),) * N_H)
    st = values_block(n_pages, pv_new, st)
    for h in range(N_H):
        o_ref[:, h * HEAD_DIM:(h + 1) * HEAD_DIM] = st[N_H + h] / jnp.sum(st[h], axis=1, keepdims=True)


def _dsa_sample(page_table, aq, iq, small, ik_new_pad, ak_new_pad, av_new_pad, cache_it, cache_kt, cache_vt,
                layer, bs, s):
    n_pages = page_table.shape[1]
    page = cache_kt.shape[-1]
    topk = min(TOPK_MAX, (n_pages * page + s) // 4)
    row = lambda w: pl.BlockSpec((s, w), lambda b, pt: (b, 0))
    new = lambda w: pl.BlockSpec((1, BLK, w), lambda b, pt: (b, 0, 0))
    hbm = pl.BlockSpec(memory_space=pl.ANY)
    return pl.pallas_call(
        functools.partial(_dsas_kernel, layer=layer, n_pages=n_pages, topk=topk),
        out_shape=jax.ShapeDtypeStruct(aq.shape, F32),
        grid_spec=pltpu.PrefetchScalarGridSpec(
            num_scalar_prefetch=1,
            grid=(bs,),
            in_specs=[row(W_GROUP), row(W_GROUP), row(LANES), new(IDX_DIM), new(W_GROUP), new(W_GROUP),
                      hbm, hbm, hbm],
            out_specs=row(W_GROUP),
            scratch_shapes=[pltpu.VMEM((n_pages + 1, s, BLK), I32), pltpu.VMEM((n_pages + 1, N_H, s, BLK), F32),
                            pltpu.VMEM((PAGE_RING * PAGE_UNROLL, IDX_DIM, page), F32),
                            pltpu.VMEM((PAGE_RING * PAGE_UNROLL, N_H, HEAD_DIM, page), F32),
                            pltpu.VMEM((PAGE_RING * PAGE_UNROLL, N_H, HEAD_DIM, page), F32),
                            pltpu.SemaphoreType.DMA((3, PAGE_RING * PAGE_UNROLL))]),
        compiler_params=_cparams(("arbitrary",)),
        name="dsa_sample",
    )(page_table, aq, iq, small, ik_new_pad, ak_new_pad, av_new_pad, cache_it, cache_kt, cache_vt)


def _gdn_kernel(x_ref, sm_ref, gate_ref, tail0_ref, s0_ref, cw_ref, alog_ref, dtb_ref, ng_ref,
                o_ref, sout_ref, s_scr, tail_scr, *, chunk):
    step = pl.program_id(1)
    rb = x_ref.shape[0]
    n_chunks = rb // chunk

    @pl.when(step == 0)
    def _():
        s_scr[...] = s0_ref[0]
        tail_scr[...] = tail0_ref[0]

    x = x_ref[...]
    xfull = jnp.concatenate([tail_scr[...], x], axis=0)
    y = xfull[8:8 + rb] * cw_ref[CONV_W - 1:CONV_W, :]
    for j in range(CONV_W - 1):
        y = y + pltpu.roll(xfull, CONV_W - 1 - j, 0)[8:8 + rb] * cw_ref[j:j + 1, :]
    tail_scr[...] = xfull[rb:rb + 8]
    c = _silu(y)

    sm = sm_ref[...]
    beta_all = _sigmoid(sm)
    g_all = -jnp.exp(alog_ref[...]) * _softplus(sm + dtb_ref[...])
    rbp = ((rb + LANES - 1) // LANES) * LANES
    g_pad = g_all if rbp == rb else jnp.concatenate([g_all, jnp.zeros((rbp - rb, LANES), F32)], axis=0)
    g_t = g_pad.T[:, :rb]

    ri, ci = _iota((rb, rb), 0), _iota((rb, rb), 1)
    incl, strict, upper = ci <= ri, ci < ri, ri <= ci
    if n_chunks > 1:
        sh = int(math.log2(chunk))
        same = (ri >> sh) == (ci >> sh)
        incl, strict, upper = incl & same, strict & same, upper & same
    eye = (ri == ci).astype(F32)

    def same_blk(size):
        sh_b = int(math.log2(size))
        return (ri >> sh_b) == (ci >> sh_b)

    gcol_all = _dot_precise_rhs(incl.astype(BF16), g_all)
    grow_all = _dot_precise_lhs(g_t, upper.astype(BF16))
    gate = gate_ref[...]
    n_sq = int(math.log2(chunk))
    scale = HEAD_DIM ** -0.5

    for h in range(N_H):
        hs = slice(h * HEAD_DIM, (h + 1) * HEAD_DIM)
        qh = c[:, h * HEAD_DIM:(h + 1) * HEAD_DIM]
        kh = c[:, W_GROUP + h * HEAD_DIM:W_GROUP + (h + 1) * HEAD_DIM]
        vh = c[:, 2 * W_GROUP + h * HEAD_DIM:2 * W_GROUP + (h + 1) * HEAD_DIM]
        qh = qh * lax.rsqrt(jnp.sum(qh * qh, axis=1, keepdims=True) + NORM_EPS) * scale
        kh = kh * lax.rsqrt(jnp.sum(kh * kh, axis=1, keepdims=True) + NORM_EPS)
        bh = beta_all[:, SMALL_BETA + h:SMALL_BETA + h + 1]
        gcol = gcol_all[:, SMALL_A + h:SMALL_A + h + 1]
        grow = grow_all[SMALL_A + h:SMALL_A + h + 1, :]
        decay = jnp.where(incl, jnp.exp(jnp.where(incl, gcol - grow, 0.0)), 0.0)
        kb = kh * bh
        m = jnp.where(strict, _dot_nt(kb, kh) * decay, 0.0)
        size = min(chunk, GDN_BASE)
        pw = -jnp.where(same_blk(size), m, 0.0)
        t = eye + pw
        for _ in range(int(math.log2(size)) - 1):
            pw = _dot(pw, pw)
            t = t + _dot(t, pw)
        while size < chunk:
            off = jnp.where(jnp.logical_and(same_blk(2 * size), jnp.logical_not(same_blk(size))), m, 0.0)
            t = t - _dot(_dot(t, off), t)
            size *= 2
        u = _dot(t, vh * bh)
        w = _dot(t, kb * jnp.exp(gcol))
        attn = jnp.where(incl, _dot_nt(qh, kh) * decay, 0.0)
        qg = qh * jnp.exp(gcol)
        s_h = s_scr[h]
        vnews, inters = [], []
        for cidx in range(n_chunks):
            rows = slice(cidx * chunk, (cidx + 1) * chunk)
            vnew = u[rows] - _dot(w[rows], s_h)
            inters.append(_dot(qg[rows], s_h))
            glast = gcol[(cidx + 1) * chunk - 1:(cidx + 1) * chunk, :]
            s_h = s_h * jnp.exp(glast) + _dot_tn(kh[rows] * jnp.exp(glast - gcol[rows]), vnew)
            vnews.append(vnew)
        s_scr[h] = s_h
        vnew_all = vnews[0] if n_chunks == 1 else jnp.concatenate(vnews, axis=0)
        inter = inters[0] if n_chunks == 1 else jnp.concatenate(inters, axis=0)
        o = inter + _dot(attn, vnew_all)
        o = o * lax.rsqrt(jnp.mean(o * o, axis=1, keepdims=True) + NORM_EPS) * ng_ref[:, hs]
        o_ref[:, hs] = o * _silu(gate[:, hs])

    @pl.when(step == pl.num_programs(1) - 1)
    def _():
        sout_ref[0] = s_scr[...]


def _gdn(dqkv, small, dgate, tail0, s0, conv_w, alog_row, dtb_row, ng_row, batch, seq, chunk, rb):
    steps = seq // rb
    row = lambda w: pl.BlockSpec((rb, w), lambda b, i: (b * steps + i, 0))
    const = lambda shp: pl.BlockSpec(shp, lambda b, i: (0,) * len(shp))
    return pl.pallas_call(
        functools.partial(_gdn_kernel, chunk=chunk),
        out_shape=[jax.ShapeDtypeStruct((dqkv.shape[0], W_GROUP), F32),
                   jax.ShapeDtypeStruct((batch, N_H, HEAD_DIM, HEAD_DIM), F32)],
        grid=(batch, steps),
        in_specs=[row(3 * W_GROUP), row(LANES), row(W_GROUP),
                  pl.BlockSpec((1, 8, 3 * W_GROUP), lambda b, i: (b, 0, 0)),
                  pl.BlockSpec((1, N_H, HEAD_DIM, HEAD_DIM), lambda b, i: (b, 0, 0, 0)),
                  const((CONV_W, 3 * W_GROUP)), const((1, LANES)), const((1, LANES)), const((1, W_GROUP))],
        out_specs=[row(W_GROUP), pl.BlockSpec((1, N_H, HEAD_DIM, HEAD_DIM), lambda b, i: (b, 0, 0, 0))],
        scratch_shapes=[pltpu.VMEM((N_H, HEAD_DIM, HEAD_DIM), F32), pltpu.VMEM((8, 3 * W_GROUP), F32)],
        compiler_params=_cparams(("parallel", "arbitrary")),
        name="gdn",
    )(dqkv, small, dgate, tail0, s0, conv_w, alog_row, dtb_row, ng_row)


def _lane_row(vec, offset):
    return jnp.zeros((1, LANES), F32).at[0, offset:offset + vec.shape[0]].set(vec)


def _pad_rows(x, bs, s, rows):
    w = x.shape[-1]
    return jnp.concatenate([x.reshape(bs, s, w), jnp.zeros((bs, rows - s, w), x.dtype)], axis=1)


def kernel(x_prompt, x_sample, cache_dsa_k, cache_dsa_v, cache_dsa_kidx, cache_sb_k, cache_sb_v, state_gdn_S,
           state_gdn_conv, page_table, c_prompt, c_sample, w_cond, b_cond, w_in, w_out, ln_g, ln_b, conv_w, a_log,
           dt_bias, gdn_norm_g, gmlp_w_s, gmlp_b_s, w_ff1, w_ff2):
    bp, seq, d = x_prompt.shape
    bs, s, _ = x_sample.shape
    depth = w_in.shape[0]
    page = cache_dsa_k.shape[2]
    alpha = (2 * depth) ** 0.25
    assert seq % BLK == 0 and page == BLK and s >= CONV_W - 1 and s % 8 == 0 and s <= BLK

    c_all = jnp.concatenate([c_prompt, c_sample], axis=0)
    rc = ((c_all.shape[0] + 7) // 8) * 8
    c_all = jnp.concatenate([c_all, jnp.zeros((rc - c_all.shape[0], d), F32)], axis=0)
    mod = _cond(c_all, w_cond, b_cond)

    cache_it = jnp.transpose(cache_dsa_kidx, (0, 1, 3, 2))
    cache_akt, cache_avt, cache_ckt, cache_cvt = (jnp.transpose(t, (0, 1, 3, 4, 2))
                                                  for t in (cache_dsa_k, cache_dsa_v, cache_sb_k, cache_sb_v))

    xp = x_prompt.reshape(bp * seq, d)
    xs = x_sample.reshape(bs * s, d)
    c_gmlp = min(seq, GMLP_CHUNK)
    cs_gmlp = min(s, GMLP_CHUNK)
    st_p, st_s = [], []
    for l in range(depth):
        w_pad = _pad_w_in(w_in[l])
        wout_b, w1_b, w2_b = w_out[l].astype(BF16), w_ff1[l].astype(BF16), w_ff2[l].astype(BF16)
        alog_row, dtb_row = _lane_row(a_log[l], SMALL_A), _lane_row(dt_bias[l], SMALL_A)
        ng_row = jnp.tile(gdn_norm_g[l], N_H).reshape(1, W_GROUP)
        pre_ln = l == 0

        assert bp == 1
        mods = [mod[l, 0:1, i * d:(i + 1) * d] for i in range(6)]
        pr = _proj(xp, mods[0], mods[1], w_pad, pre_ln, 512 if (bp * seq) % 512 == 0 else BLK, True)
        out_a = _dsa_prompt(pr["aq"], pr["iq"], pr["small"], pr["ik_b"], pr["ak_b"], pr["av_t"], bp, seq)
        out_b, _ = _gmlp(pr["bu"], pr["bv"], gmlp_w_s[l][:, :c_gmlp, :c_gmlp], gmlp_b_s[l][:, :c_gmlp].T,
                         c_gmlp, c_gmlp, False)
        out_c = _sb_prompt(pr["cq"], pr["ck_b"], pr["cv_b"], bp, seq)
        gchunk = math.gcd(seq, GDN_CHUNK)
        out_d, s_new = _gdn(pr["dqkv"], pr["small"], pr["dgate"], jnp.zeros((bp, 8, 3 * W_GROUP), F32),
                            jnp.zeros((bp, N_H, HEAD_DIM, HEAD_DIM), F32), conv_w[l], alog_row, dtb_row, ng_row,
                            bp, seq, gchunk, 256 if seq % 256 == 0 else gchunk)
        xp = _outffn(xp, (out_a, out_b, out_c, out_d), mods[2], mods[3], mods[4], mods[5], wout_b, ln_g[l], ln_b[l],
                     w1_b, w2_b, pre_ln, alpha, tm=256 if (bp * seq) % 256 == 0 else BLK)
        buf_new = pr["dqkv"].reshape(bp, seq, 3 * W_GROUP)[:, seq - (CONV_W - 1):]
        st_p.append((pr["ak"].reshape(bp, seq, N_H, HEAD_DIM), pr["av"].reshape(bp, seq, N_H, HEAD_DIM),
                     pr["ik"].reshape(bp, seq, IDX_DIM), pr["ck"].reshape(bp, seq, N_H, HEAD_DIM),
                     pr["cv"].reshape(bp, seq, N_H, HEAD_DIM), s_new, buf_new))

        mods = [jnp.repeat(mod[l, bp:bp + bs, i * d:(i + 1) * d], s, axis=0) for i in range(6)]
        ps = _proj(xs, mods[0], mods[1], w_pad, pre_ln, bs * s, False)
        out_a = _dsa_sample(page_table, ps["aq"], ps["iq"], ps["small"], _pad_rows(ps["ik"], bs, s, BLK),
                            _pad_rows(ps["ak"], bs, s, BLK), _pad_rows(ps["av"], bs, s, BLK),
                            cache_it, cache_akt, cache_avt, l, bs, s)
        out_b, v_rows = _gmlp(ps["bu"], ps["bv"], jnp.tile(gmlp_w_s[l][:, :cs_gmlp, :cs_gmlp], (1, bs, bs)),
                              jnp.tile(gmlp_b_s[l][:, :cs_gmlp].T, (bs, 1)), bs * s, cs_gmlp, True)
        out_c = _sb_sample(page_table, ps["cq"], _pad_rows(ps["ck"], bs, s, BLK), _pad_rows(ps["cv"], bs, s, BLK),
                           cache_ckt, cache_cvt, l, bs, s)
        tail0 = jnp.concatenate([jnp.zeros((bs, 8 - (CONV_W - 1), 3 * W_GROUP), F32), state_gdn_conv[l]], axis=1)
        out_d, s_new = _gdn(ps["dqkv"], ps["small"], ps["dgate"], tail0, state_gdn_S[l], conv_w[l], alog_row,
                            dtb_row, ng_row, bs, s, math.gcd(s, GDN_CHUNK), s)
        xs = _outffn(xs, (out_a, out_b, out_c, out_d), mods[2], mods[3], mods[4], mods[5], wout_b, ln_g[l], ln_b[l],
                     w1_b, w2_b, pre_ln, alpha, tm=bs * s)
        buf_new = ps["dqkv"].reshape(bs, s, 3 * W_GROUP)[:, s - (CONV_W - 1):]
        st_s.append((ps["ak"].reshape(bs, s, N_H, HEAD_DIM), ps["av"].reshape(bs, s, N_H, HEAD_DIM),
                     ps["ik"].reshape(bs, s, IDX_DIM), ps["ck"].reshape(bs, s, N_H, HEAD_DIM),
                     ps["cv"].reshape(bs, s, N_H, HEAD_DIM), s_new, buf_new, v_rows.reshape(bs, s, W_GROUP)))

    outs_p = tuple(jnp.stack(t) for t in zip(*st_p))
    outs_s = tuple(jnp.stack(t) for t in zip(*st_s))
    return (xp.reshape(bp, seq, d), xs.reshape(bs, s, d)) + outs_p + outs_s
```

```python
import functools
import math

import jax
import jax.numpy as jnp
from jax import lax
from jax.experimental import pallas as pl
from jax.experimental.pallas import tpu as pltpu

F32 = jnp.float32
BF16 = jnp.bfloat16
I32 = jnp.int32
I16 = jnp.int16
I16_MIN = -2 ** 15

LN_EPS = 1e-5
NORM_EPS = 1e-6
N_H = 4
HEAD_DIM = 64
W_GROUP = N_H * HEAD_DIM
IDX_HEADS = 4
IDX_DIM = 64
TOPK_MAX = 256
GMLP_CHUNK = 128
GDN_CHUNK = 64
CONV_W = 4
BLK = 128
LANES = 128
VMEM_LIMIT = 56 * 1024 * 1024
SB_ZERO_TAIL = -110.0
INT_MIN = -2 ** 31
INT_MAX = 2 ** 31 - 1
LOG2E = 1.4426950408889634
MASKED = -1e30
M_INIT = -1e20
UNROLL = 8
PAGE_UNROLL = 4
SB_RING = 3
PAGE_RING = 8
COUNT_UNROLL = 4
GDN_BASE = 8

PIECES = (("dqkv", 768, 768), ("aq", 256, 256), ("ak", 256, 256), ("av", 256, 256), ("iq", 256, 256),
          ("bu", 256, 256), ("bv", 256, 256), ("cq", 256, 256), ("ck", 256, 256), ("cv", 256, 256),
          ("dgate", 256, 256), ("ik", 64, 128), ("small", 12, 128))
N_PAD = sum(p[2] for p in PIECES)
SMALL_IW, SMALL_BETA, SMALL_A = 0, 4, 8
BF16_COPIES = ("ak", "ck", "cv", "ik")


def _cparams(sem, vmem=VMEM_LIMIT):
    return pltpu.CompilerParams(dimension_semantics=sem, vmem_limit_bytes=vmem)


def _ln_plain(x):
    mu = jnp.mean(x, axis=-1, keepdims=True)
    xc = x - mu
    var = jnp.mean(xc * xc, axis=-1, keepdims=True)
    return xc * lax.rsqrt(var + LN_EPS)


def _sigmoid(x):
    return 1.0 / (1.0 + jnp.exp(-x))


def _silu(x):
    return x * _sigmoid(x)


def _softplus(x):
    return jnp.maximum(x, 0.0) + jnp.log1p(jnp.exp(-jnp.abs(x)))


def _gelu_tanh(x):
    c = math.sqrt(2.0 / math.pi)
    return x * (0.5 * (1.0 + jnp.tanh(c * (x + 0.044715 * (x * x * x)))))


def _dot(a, b):
    return jnp.dot(a.astype(BF16), b.astype(BF16), preferred_element_type=F32)


def _dot_nt(a, b):
    return lax.dot_general(a.astype(BF16), b.astype(BF16), (((1,), (1,)), ((), ())),
                           preferred_element_type=F32)


def _dot_tn(a, b):
    return lax.dot_general(a.astype(BF16), b.astype(BF16), (((0,), (0,)), ((), ())),
                           preferred_element_type=F32)


def _split(x):
    hi = x.astype(BF16)
    lo = (x - hi.astype(F32)).astype(BF16)
    return hi, lo


def _dot_precise_lhs(a, b01):
    hi, lo = _split(a)
    return jnp.dot(hi, b01, preferred_element_type=F32) + jnp.dot(lo, b01, preferred_element_type=F32)


def _dot_precise_rhs(a01, b):
    hi, lo = _split(b)
    return jnp.dot(a01, hi, preferred_element_type=F32) + jnp.dot(a01, lo, preferred_element_type=F32)


def _dot3(a, b):
    ah, al = _split(a)
    bh, bl = _split(b)
    return (jnp.dot(ah, bh, preferred_element_type=F32) + jnp.dot(al, bh, preferred_element_type=F32)
            + jnp.dot(ah, bl, preferred_element_type=F32))


def _iota(shape, dim):
    return lax.broadcasted_iota(I32, shape, dim)


def _sort_key(score):
    bits = pltpu.bitcast(score, I32)
    return bits ^ ((bits >> 31) & jnp.int32(0x7FFFFFFF))


def _alibi_slope(h):
    return 2.0 ** (-8.0 * (h + 1) / N_H)


def _heads(x):
    return [x[:, h * HEAD_DIM:(h + 1) * HEAD_DIM] for h in range(N_H)]


def _cond_kernel(c_ref, w_ref, b_ref, o_ref):
    s = _silu(c_ref[...])
    o_ref[0] = _dot3(s, w_ref[0]) + b_ref[0]


def _cond(c_all, w_cond, b_cond):
    depth, d, n6 = w_cond.shape
    rc = c_all.shape[0]
    tn = 1536 if n6 % 1536 == 0 else n6
    return pl.pallas_call(
        _cond_kernel,
        out_shape=jax.ShapeDtypeStruct((depth, rc, n6), F32),
        grid=(depth, n6 // tn),
        in_specs=[pl.BlockSpec((rc, d), lambda l, j: (0, 0)),
                  pl.BlockSpec((1, d, tn), lambda l, j: (l, 0, j)),
                  pl.BlockSpec((1, 1, tn), lambda l, j: (l, 0, j))],
        out_specs=pl.BlockSpec((1, rc, tn), lambda l, j: (l, 0, j)),
        compiler_params=_cparams(("arbitrary", "arbitrary")),
        name="cond",
    )(c_all, w_cond, b_cond.reshape(depth, 1, n6))


def _proj_kernel(x_ref, sh_ref, sc_ref, w_ref, *o_refs, pre_ln, extras):
    x = x_ref[...]
    if pre_ln:
        x = _ln_plain(x)
    hb = (x * (1.0 + sc_ref[...]) + sh_ref[...]).astype(BF16)
    col = 0
    for (name, width, padded), o_ref in zip(PIECES, o_refs[:len(PIECES)]):
        res = jnp.dot(hb, w_ref[:, col:col + padded], preferred_element_type=F32)
        o_ref[...] = res if o_ref.shape[-1] == padded else res[:, :o_ref.shape[-1]]
        if extras and name in BF16_COPIES:
            o_refs[len(PIECES) + BF16_COPIES.index(name)][...] = res.astype(BF16)
        if extras and name == "av":
            vt_ref = o_refs[len(PIECES) + len(BF16_COPIES)]
            for j in range(vt_ref.shape[0]):
                vt_ref[j] = res[j * BLK:(j + 1) * BLK, :].T.astype(BF16)
        col += padded


def _mod_spec(m, tm, d):
    if m.shape[0] == 1:
        return pl.BlockSpec((1, d), lambda i: (0, 0))
    return pl.BlockSpec((tm, d), lambda i: (i, 0))


def _proj(x, shift, scale, w_pad, pre_ln, tm, extras):
    r, d = x.shape
    out_shapes, out_specs = [], []
    for name, width, padded in PIECES:
        w_out = width if name == "ik" else padded
        out_shapes.append(jax.ShapeDtypeStruct((r, w_out), F32))
        out_specs.append(pl.BlockSpec((tm, w_out), lambda i: (i, 0)))
    if extras:
        for name in BF16_COPIES:
            w_out = LANES if name == "ik" else W_GROUP
            out_shapes.append(jax.ShapeDtypeStruct((r, w_out), BF16))
            out_specs.append(pl.BlockSpec((tm, w_out), lambda i: (i, 0)))
        out_shapes.append(jax.ShapeDtypeStruct((r // BLK, W_GROUP, BLK), BF16))
        out_specs.append(pl.BlockSpec((tm // BLK, W_GROUP, BLK), lambda i: (i, 0, 0)))
    outs = pl.pallas_call(
        functools.partial(_proj_kernel, pre_ln=pre_ln, extras=extras),
        out_shape=out_shapes,
        grid=(r // tm,),
        in_specs=[pl.BlockSpec((tm, d), lambda i: (i, 0)), _mod_spec(shift, tm, d), _mod_spec(scale, tm, d),
                  pl.BlockSpec((d, N_PAD), lambda i: (0, 0))],
        out_specs=out_specs,
        compiler_params=_cparams(("parallel",)),
        name="proj",
    )(x, shift, scale, w_pad)
    res = {name: o for (name, _, _), o in zip(PIECES, outs)}
    if extras:
        for i, name in enumerate(BF16_COPIES):
            res[name + "_b"] = outs[len(PIECES) + i]
        res["av_t"] = outs[len(PIECES) + len(BF16_COPIES)]
    return res


def _pad_w_in(w_in_l):
    off = {}
    start = 0
    for name, n in (("aq", 256), ("ak", 256), ("av", 256), ("iq", 256), ("ik", 64), ("iw", 4), ("bu", 256),
                    ("bv", 256), ("cq", 256), ("ck", 256), ("cv", 256), ("dqkv", 768), ("dbeta", 4), ("da", 4),
                    ("dgate", 256)):
        off[name] = (start, n)
        start += n
    d = w_in_l.shape[0]
    cols = []
    for name, width, padded in PIECES:
        if name == "small":
            parts = [w_in_l[:, off[k][0]:off[k][0] + off[k][1]] for k in ("iw", "dbeta", "da")]
            piece = jnp.concatenate(parts, axis=1)
        else:
            piece = w_in_l[:, off[name][0]:off[name][0] + off[name][1]]
        if padded > piece.shape[1]:
            piece = jnp.concatenate([piece, jnp.zeros((d, padded - piece.shape[1]), piece.dtype)], axis=1)
        cols.append(piece)
    return jnp.concatenate(cols, axis=1).astype(BF16)


def _outffn_kernel(x_ref, oa_ref, ob_ref, oc_ref, od_ref, gm_ref, shf_ref, scf_ref, gf_ref,
                   wout_ref, lng_ref, lnb_ref, w1_ref, w2_ref, o_ref, *, pre_ln, alpha):
    x = x_ref[...]
    if pre_ln:
        x = _ln_plain(x)
    mo = None
    for g, r in enumerate((oa_ref, ob_ref, oc_ref, od_ref)):
        part = _dot(r[...], wout_ref[g * W_GROUP:(g + 1) * W_GROUP, :])
        mo = part if mo is None else mo + part
    x1 = _ln_plain(alpha * x + gm_ref[...] * mo) * lng_ref[0:1, :] + lnb_ref[0:1, :]
    hf = x1 * (1.0 + scf_ref[...]) + shf_ref[...]
    a = jnp.maximum(_dot(hf, w1_ref[...]), 0.0)
    ff = _dot(a * a, w2_ref[...])
    o_ref[...] = _ln_plain(alpha * x1 + gf_ref[...] * ff) * lng_ref[1:2, :] + lnb_ref[1:2, :]


def _outffn(x, branches, gm, shf, scf, gf, wout_b, ln_g, ln_b, w1_b, w2_b, pre_ln, alpha, tm):
    r, d = x.shape
    dff = w1_b.shape[1]
    row = lambda w: pl.BlockSpec((tm, w), lambda i: (i, 0))
    const = lambda shp: pl.BlockSpec(shp, lambda i: (0, 0), pipeline_mode=pl.Buffered(1))
    return pl.pallas_call(
        functools.partial(_outffn_kernel, pre_ln=pre_ln, alpha=alpha),
        out_shape=jax.ShapeDtypeStruct((r, d), F32),
        grid=(r // tm,),
        in_specs=[row(d)] + [row(W_GROUP)] * 4 + [_mod_spec(m, tm, d) for m in (gm, shf, scf, gf)]
        + [const((d, d)), const((2, d)), const((2, d)), const((d, dff)), const((dff, d))],
        out_specs=row(d),
        compiler_params=_cparams(("parallel",)),
        name="outffn",
    )(x, *branches, gm, shf, scf, gf, wout_b, ln_g, ln_b, w1_b, w2_b)


def _gmlp_kernel(u_ref, v_ref, w_ref, bt_ref, ob_ref, *vn_refs, chunk):
    u = _gelu_tanh(u_ref[...])
    vn = _ln_plain(_gelu_tanh(v_ref[...]))
    if vn_refs:
        vn_refs[0][...] = vn
    rows = u.shape[0]
    ri, ci = _iota((rows, rows), 0), _iota((rows, rows), 1)
    mask = ci <= ri
    if rows != chunk:
        sh = int(math.log2(chunk))
        mask = mask & ((ri >> sh) == (ci >> sh))
    vb = vn.astype(BF16)
    for g in range(N_H):
        w = jnp.where(mask, w_ref[g], 0.0).astype(BF16)
        mixed = jnp.dot(w, vb[:, g * HEAD_DIM:(g + 1) * HEAD_DIM], preferred_element_type=F32)
        mixed = mixed + bt_ref[:, g:g + 1]
        ob_ref[:, g * HEAD_DIM:(g + 1) * HEAD_DIM] = u[:, g * HEAD_DIM:(g + 1) * HEAD_DIM] * mixed


def _gmlp(bu, bv, w_tiled, bt_tiled, rows, chunk, want_vn):
    r = bu.shape[0]
    row = pl.BlockSpec((rows, W_GROUP), lambda i: (i, 0))
    out_shape = [jax.ShapeDtypeStruct((r, W_GROUP), F32)]
    out_specs = [row]
    if want_vn:
        out_shape.append(jax.ShapeDtypeStruct((r, W_GROUP), F32))
        out_specs.append(row)
    outs = pl.pallas_call(
        functools.partial(_gmlp_kernel, chunk=chunk),
        out_shape=out_shape,
        grid=(r // rows,),
        in_specs=[row, row, pl.BlockSpec((N_H, rows, rows), lambda i: (0, 0, 0)),
                  pl.BlockSpec((rows, N_H), lambda i: (0, 0))],
        out_specs=out_specs,
        compiler_params=_cparams(("parallel",)),
        name="gmlp",
    )(bu, bv, w_tiled, bt_tiled)
    return outs if want_vn else (outs[0], None)


def _sb_block(qk, pv, vis, carry, acc, upper):
    new_carry, new_acc = [], []
    for h in range(N_H):
        z = qk(h)
        l_raw = -_softplus(z)
        l_vis = l_raw if vis is None else jnp.where(vis, l_raw, 0.0)
        tail = _dot_precise_lhs(l_vis, upper) + carry[h]
        w = jnp.exp(z + l_raw + tail)
        if vis is not None:
            w = jnp.where(vis, w, 0.0)
        new_acc.append(acc[h] + pv(h, w))
        new_carry.append(carry[h] + jnp.sum(l_vis, axis=1, keepdims=True))
    return new_carry, new_acc


def _sb_live(carry):
    m = carry[0]
    for c in carry[1:]:
        m = jnp.maximum(m, c)
    return (jnp.max(m) > SB_ZERO_TAIL).astype(I32)


def _upper_ones():
    return (_iota((BLK, BLK), 0) > _iota((BLK, BLK), 1)).astype(BF16)


def _natural_kv(q_heads, k, v):
    kh, vh = _heads(k), _heads(v)
    return (lambda h: _dot_nt(q_heads[h], kh[h])), (lambda h, w: _dot(w, vh[h]))


def _sbp_kernel(q_ref, k_ref, v_ref, o_ref):
    qb = pl.program_id(1)
    q_heads = _heads((q_ref[...] * (HEAD_DIM ** -0.5)).astype(BF16))
    upper = _upper_ones()
    ri, ci = _iota((BLK, BLK), 0), _iota((BLK, BLK), 1)
    zero_c = [jnp.zeros((BLK, 1), F32)] * N_H
    zero_a = [jnp.zeros((BLK, HEAD_DIM), F32)] * N_H
    off = pl.multiple_of(qb * BLK, BLK)
    qk, pv = _natural_kv(q_heads, k_ref[pl.ds(off, BLK), :], v_ref[pl.ds(off, BLK), :])
    carry, acc = _sb_block(qk, pv, ci < ri, zero_c, zero_a, upper)

    def cond(st):
        return jnp.logical_and(st[0] >= 0, st[1] > 0)

    def body(st):
        kb = st[0]
        o = pl.multiple_of(kb * BLK, BLK)
        qk, pv = _natural_kv(q_heads, k_ref[pl.ds(o, BLK), :], v_ref[pl.ds(o, BLK), :])
        c, a = _sb_block(qk, pv, None, list(st[2:2 + N_H]), list(st[2 + N_H:]), upper)
        return (kb - 1, _sb_live(c), *c, *a)

    st = lax.while_loop(cond, body, (qb - 1, _sb_live(carry), *carry, *acc))
    for h in range(N_H):
        o_ref[:, h * HEAD_DIM:(h + 1) * HEAD_DIM] = st[2 + N_H + h]


def _sb_prompt(cq, ck_b, cv_b, batch, seq):
    nq = seq // BLK
    full = pl.BlockSpec((seq, W_GROUP), lambda b, i: (b, 0), pipeline_mode=pl.Buffered(1))
    return pl.pallas_call(
        _sbp_kernel,
        out_shape=jax.ShapeDtypeStruct(cq.shape, F32),
        grid=(batch, nq),
        in_specs=[pl.BlockSpec((BLK, W_GROUP), lambda b, i: (b * nq + i, 0)), full, full],
        out_specs=pl.BlockSpec((BLK, W_GROUP), lambda b, i: (b * nq + i, 0)),
        compiler_params=_cparams(("parallel", "arbitrary")),
        name="sb_prompt",
    )(cq, ck_b, cv_b)


def _sbs_kernel(pt_ref, q_ref, kn_ref, vn_ref, kc_ref, vc_ref, o_ref, kbuf, vbuf, sem, *, layer, n_pages):
    b = pl.program_id(0)
    rows = q_ref.shape[0]
    q_heads = _heads((q_ref[...] * (HEAD_DIM ** -0.5)).astype(BF16))
    upper = _upper_ones()
    ri, ci = _iota((rows, BLK), 0), _iota((rows, BLK), 1)
    zero_c = [jnp.zeros((rows, 1), F32)] * N_H
    zero_a = [jnp.zeros((rows, HEAD_DIM), F32)] * N_H
    def copies(p):
        page, slot = pt_ref[b, p], p % SB_RING
        return (pltpu.make_async_copy(kc_ref.at[layer, page], kbuf.at[slot], sem.at[0, slot]),
                pltpu.make_async_copy(vc_ref.at[layer, page], vbuf.at[slot], sem.at[1, slot]))

    def start(p):
        for c in copies(p):
            c.start()

    def wait(p):
        for c in copies(p):
            c.wait()

    for j in range(1, min(SB_RING, n_pages + 1)):
        start(n_pages - j)
    qk, pv = _natural_kv(q_heads, kn_ref[0], vn_ref[0])
    carry, acc = _sb_block(qk, pv, ci < ri, zero_c, zero_a, upper)

    def cond(st):
        return jnp.logical_and(st[0] >= 0, st[1] > 0)

    def body(st):
        p = st[0]
        slot = p % SB_RING
        wait(p)

        @pl.when(p - (SB_RING - 1) >= 0)
        def _():
            start(p - (SB_RING - 1))

        c, a = _sb_block(lambda h: _dot(q_heads[h], kbuf[slot, h]), lambda h, w: _dot_nt(w, vbuf[slot, h]), None,
                         list(st[2:2 + N_H]), list(st[2 + N_H:]), upper)
        return (p - 1, _sb_live(c), *c, *a)

    st = lax.while_loop(cond, body, (jnp.int32(n_pages - 1), _sb_live(carry), *carry, *acc))
    for j in range(SB_RING - 1):
        @pl.when(st[0] - j >= 0)
        def _():
            wait(st[0] - j)

    for h in range(N_H):
        o_ref[:, h * HEAD_DIM:(h + 1) * HEAD_DIM] = st[2 + N_H + h]


def _sb_sample(page_table, cq, ck_new_pad, cv_new_pad, cache_kt, cache_vt, layer, bs, s):
    n_pages = page_table.shape[1]
    page = cache_kt.shape[-1]
    return pl.pallas_call(
        functools.partial(_sbs_kernel, layer=layer, n_pages=n_pages),
        out_shape=jax.ShapeDtypeStruct(cq.shape, F32),
        grid_spec=pltpu.PrefetchScalarGridSpec(
            num_scalar_prefetch=1,
            grid=(bs,),
            in_specs=[pl.BlockSpec((s, W_GROUP), lambda b, pt: (b, 0)),
                      pl.BlockSpec((1, BLK, W_GROUP), lambda b, pt: (b, 0, 0)),
                      pl.BlockSpec((1, BLK, W_GROUP), lambda b, pt: (b, 0, 0)),
                      pl.BlockSpec(memory_space=pl.ANY), pl.BlockSpec(memory_space=pl.ANY)],
            out_specs=pl.BlockSpec((s, W_GROUP), lambda b, pt: (b, 0)),
            scratch_shapes=[pltpu.VMEM((SB_RING, N_H, HEAD_DIM, page), F32),
                            pltpu.VMEM((SB_RING, N_H, HEAD_DIM, page), F32),
                            pltpu.SemaphoreType.DMA((2, SB_RING))]),
        compiler_params=_cparams(("arbitrary",)),
        name="sb_sample",
    )(page_table, cq, ck_new_pad, cv_new_pad, cache_kt, cache_vt)


def _kth_largest_key(count_ge, shape, k):
    kf = jnp.float32(k)
    t = jnp.where(count_ge(jnp.zeros(shape, I32)) >= kf, jnp.int32(0), jnp.int32(INT_MIN))

    def body(i, t):
        cand = t + (jnp.int32(1) << (30 - i))
        return jnp.where(count_ge(cand) >= kf, cand, t)

    return lax.fori_loop(0, 31, body, t)


def _kth_largest_i16(count_ge, shape, k):
    t = jnp.where(count_ge(jnp.zeros(shape, I32)) >= k, jnp.int32(0), jnp.int32(I16_MIN))

    def body(i, t):
        cand = t + (jnp.int32(1) << (14 - i))
        return jnp.where(count_ge(cand) >= k, cand, t)

    return lax.fori_loop(0, 15, body, t)


def _dsap_kernel(aq_ref, iq_ref, sm_ref, ik_ref, k_ref, vt_ref, o_ref, key_ref, hi_ref, lo_ref, s_scr, *, topk):
    qb = pl.program_id(1)
    nkb = qb + 1
    ri, ci = _iota((BLK, BLK), 0), _iota((BLK, BLK), 1)
    diag_ok = ri <= ci

    iq_t = (iq_ref[...] * (IDX_DIM ** -0.5)).T
    iq_all = jnp.concatenate([iq_t[h * IDX_DIM:(h + 1) * IDX_DIM] for h in range(IDX_HEADS)],
                             axis=1).astype(BF16)
    sm_t = sm_ref[...].T
    iw = [sm_t[SMALL_IW + h:SMALL_IW + h + 1, :] * (IDX_HEADS ** -0.5) for h in range(IDX_HEADS)]

    def score_block(kb, diagonal):
        o = pl.multiple_of(kb * BLK, BLK)
        kidx = ik_ref[pl.ds(o, BLK), :][:, :IDX_DIM]
        s_all = jnp.dot(kidx, iq_all, preferred_element_type=F32)
        score = iw[0] * jnp.maximum(s_all[:, 0:BLK], 0.0)
        for h in range(1, IDX_HEADS):
            score = score + iw[h] * jnp.maximum(s_all[:, h * BLK:(h + 1) * BLK], 0.0)
        if diagonal:
            score = jnp.where(diag_ok, score, -jnp.inf)
        key = _sort_key(score)
        key_ref[kb] = key
        hi_ref[kb] = (key >> 16).astype(I16)
        lo_ref[kb] = ((key & 0xFFFF) + I16_MIN).astype(I16)

    def score_group(g, _):
        for u in range(UNROLL):
            score_block(jnp.minimum(g * UNROLL + u, qb), False)
        return 0

    trips = (qb + UNROLL - 1) // UNROLL
    lax.fori_loop(0, trips, score_group, 0)
    score_block(qb, True)

    def count_blocks(ref, pred, dtype):
        def one(kb, cnt):
            return cnt + jnp.where(pred(ref[kb]), jnp.ones((), dtype), jnp.zeros((), dtype))

        def group(g, cnt):
            for u in range(COUNT_UNROLL):
                cnt = one(g * COUNT_UNROLL + u, cnt)
            return cnt

        groups = nkb // COUNT_UNROLL
        cnt = lax.fori_loop(0, groups, group, jnp.zeros((BLK, BLK), dtype))
        cnt = lax.fori_loop(groups * COUNT_UNROLL, nkb, one, cnt)
        return jnp.sum(cnt.astype(F32), axis=0, keepdims=True)

    def count(pred):
        return count_blocks(key_ref, pred, I32)

    def count16(ref, pred):
        return count_blocks(ref, pred, I16)

    kf = jnp.float32(topk)
    t_hi = _kth_largest_i16(lambda c: count16(hi_ref, lambda v: v >= c.astype(I16)), (1, BLK), kf)
    t_hi16 = t_hi.astype(I16)
    k_lo = kf - count16(hi_ref, lambda v: v > t_hi16)

    def keep_low(kb, _):
        lo_ref[kb] = jnp.where(hi_ref[kb] == t_hi16, lo_ref[kb], jnp.int16(I16_MIN))
        return 0

    lax.fori_loop(0, nkb, keep_low, 0)
    t_lo = _kth_largest_i16(lambda c: count16(lo_ref, lambda v: v >= c.astype(I16)), (1, BLK), k_lo)
    thr = (t_hi << 16) + (t_lo - I16_MIN)
    n_gt = count(lambda key: key > thr)
    n_eq = count(lambda key: key == thr)
    room = jnp.float32(topk) - n_gt
    tie_break = jnp.max(n_eq - room) > 0.0

    aq_t = (aq_ref[...] * (HEAD_DIM ** -0.5 * LOG2E)).T
    bias = [(_alibi_slope(h) * LOG2E) * ri.astype(F32) for h in range(N_H)]
    zeros = jnp.zeros((HEAD_DIM, BLK), F32)
    q_bd = jnp.concatenate(
        [jnp.concatenate([aq_t[g * HEAD_DIM:(g + 1) * HEAD_DIM] if g == h else zeros for g in range(N_H)], axis=0)
         for h in range(N_H)], axis=1).astype(BF16)
    strict_lower = (ci < ri).astype(BF16)

    half = W_GROUP // 2

    def logits_to(slot, kb):
        kblk = k_ref[pl.ds(pl.multiple_of(kb * BLK, BLK), BLK), :]
        for g in range(2):
            s_scr[slot, :, g * 2 * BLK:(g + 1) * 2 * BLK] = jnp.dot(
                kblk[:, g * half:(g + 1) * half], q_bd[g * half:(g + 1) * half, g * 2 * BLK:(g + 1) * 2 * BLK],
                preferred_element_type=F32)

    def attend(ties, idx, slot, diagonal, st):
        kb = jnp.minimum(idx, qb)
        if diagonal:
            thr_b = thr
        else:
            logits_to(1 - slot, jnp.minimum(idx + 1, qb))
            thr_b = jnp.where(idx < qb, thr, jnp.int32(INT_MAX))
        seen = st[0]
        key = key_ref[kb]
        if ties:
            eq = key == thr_b
            rank = seen + jnp.dot(strict_lower, eq.astype(BF16), preferred_element_type=F32)
            sel = jnp.logical_or(key > thr_b, jnp.logical_and(eq, rank < room))
            seen = seen + jnp.sum(eq.astype(F32), axis=0, keepdims=True)
        else:
            sel = key >= thr_b
        if diagonal:
            sel = jnp.logical_and(sel, diag_ok)
        blk_off = ((kb - qb) * BLK).astype(F32)
        vt = vt_ref[kb]
        out = [seen]
        for h in range(N_H):
            m_old, l_old, a_old = st[1 + 3 * h], st[2 + 3 * h], st[3 + 3 * h]
            c_blk = (_alibi_slope(h) * LOG2E) * blk_off
            s = jnp.where(sel, s_scr[slot, :, h * BLK:(h + 1) * BLK] + bias[h], MASKED)
            m_new = jnp.maximum(m_old, jnp.max(s, axis=0, keepdims=True) + c_blk)
            p = jnp.exp2(s - (m_new - c_blk))
            alpha = jnp.exp2(m_old - m_new)
            pv = jnp.dot(vt[h * HEAD_DIM:(h + 1) * HEAD_DIM, :], p.astype(BF16), preferred_element_type=F32)
            out += [m_new, alpha * l_old + jnp.sum(p, axis=0, keepdims=True), alpha * a_old + pv]
        return tuple(out)

    init = [jnp.zeros((1, BLK), F32)]
    for h in range(N_H):
        init += [jnp.full((1, BLK), M_INIT, F32), jnp.zeros((1, BLK), F32), jnp.zeros((HEAD_DIM, BLK), F32)]
    def sweep(ties):
        def group(g, st):
            for u in range(UNROLL):
                st = attend(ties, g * UNROLL + u, u % 2, False, st)
            return st

        logits_to(0, 0)
        st = lax.fori_loop(0, trips, group, tuple(init))
        return attend(ties, qb, 0, True, st)

    st = lax.cond(tie_break, lambda: sweep(True), lambda: sweep(False))
    out_t = jnp.concatenate([st[3 + 3 * h] / st[2 + 3 * h] for h in range(N_H)], axis=0)
    o_ref[...] = out_t.T


def _dsa_prompt(aq, iq, small, ik_b, ak_b, av_t, batch, seq):
    nq = seq // BLK
    topk = min(TOPK_MAX, seq // 4)
    blk = lambda w: pl.BlockSpec((BLK, w), lambda b, i: (b * nq + i, 0))
    full = lambda w: pl.BlockSpec((seq, w), lambda b, i: (b, 0), pipeline_mode=pl.Buffered(1))
    return pl.pallas_call(
        functools.partial(_dsap_kernel, topk=topk),
        out_shape=jax.ShapeDtypeStruct(aq.shape, F32),
        grid=(batch, nq),
        in_specs=[blk(W_GROUP), blk(W_GROUP), blk(LANES), full(LANES), full(W_GROUP),
                  pl.BlockSpec((nq, W_GROUP, BLK), lambda b, i: (b, 0, 0), pipeline_mode=pl.Buffered(1))],
        out_specs=blk(W_GROUP),
        scratch_shapes=[pltpu.VMEM((nq, BLK, BLK), I32), pltpu.VMEM((nq, BLK, BLK), I16),
                        pltpu.VMEM((nq, BLK, BLK), I16), pltpu.VMEM((2, BLK, N_H * BLK), F32)],
        compiler_params=_cparams(("parallel", "arbitrary")),
        name="dsa_prompt",
    )(aq, iq, small, ik_b, ak_b, av_t)


def _dsas_kernel(pt_ref, aq_ref, iq_ref, sm_ref, ikn_ref, akn_ref, avn_ref, ic_ref, kc_ref, vc_ref, o_ref,
                 key_ref, s_ref, ibuf, kbuf, vbuf, sem, *, layer, n_pages, topk):
    b = pl.program_id(0)
    s_rows = aq_ref.shape[0]
    past = n_pages * BLK
    ri, ci = _iota((s_rows, BLK), 0), _iota((s_rows, BLK), 1)
    new_valid = jnp.logical_and(ci <= ri, ci < s_rows)

    iq = iq_ref[...] * (IDX_DIM ** -0.5)
    iq_all = jnp.concatenate(_heads(iq), axis=0).astype(BF16)
    iw = sm_ref[...][:, SMALL_IW:SMALL_IW + IDX_HEADS] * (IDX_HEADS ** -0.5)

    def idx_copy(p, slot):
        return pltpu.make_async_copy(ic_ref.at[layer, pt_ref[b, p]], ibuf.at[slot], sem.at[0, slot])

    def k_copy(p, slot):
        return pltpu.make_async_copy(kc_ref.at[layer, pt_ref[b, p]], kbuf.at[slot], sem.at[1, slot])

    def v_copy(p, slot):
        return pltpu.make_async_copy(vc_ref.at[layer, pt_ref[b, p]], vbuf.at[slot], sem.at[2, slot])

    def score_keys(s_all, valid):
        score = iw[:, 0:1] * jnp.maximum(s_all[0:s_rows], 0.0)
        for h in range(1, IDX_HEADS):
            score = score + iw[:, h:h + 1] * jnp.maximum(s_all[h * s_rows:(h + 1) * s_rows], 0.0)
        if valid is not None:
            score = jnp.where(valid, score, -jnp.inf)
        return _sort_key(score)

    u_n = PAGE_UNROLL
    trips = n_pages // u_n

    def prefetch(copy):
        for g in range(min(PAGE_RING - 1, trips)):
            for u in range(u_n):
                copy(g * u_n + u, g * u_n + u).start()

    def stream(copy, body, carry):
        def trip(g, carry):
            ahead = g + PAGE_RING - 1

            @pl.when(ahead < trips)
            def _():
                for u in range(u_n):
                    copy(ahead * u_n + u, (ahead % PAGE_RING) * u_n + u).start()

            base = (g % PAGE_RING) * u_n
            for u in range(u_n):
                copy(g * u_n + u, base + u).wait()
            for u in range(u_n):
                carry = body(g * u_n + u, base + u, carry)
            return carry

        return lax.fori_loop(0, trips, trip, carry)

    prefetch(idx_copy)
    prefetch(k_copy)
    prefetch(v_copy)

    def p1(p, slot, _):
        key_ref[p] = score_keys(_dot(iq_all, ibuf[slot]), None)
        return 0

    stream(idx_copy, p1, 0)
    key_ref[n_pages] = score_keys(_dot_nt(iq_all, ikn_ref[0]), new_valid)

    def count(pred):
        per_trip = math.gcd(n_pages, 16)

        def body(g, cnt):
            for u in range(per_trip):
                cnt = cnt + jnp.where(pred(key_ref[g * per_trip + u]), 1, 0)
            return cnt
        cnt = lax.fori_loop(0, n_pages // per_trip, body, jnp.zeros((s_rows, BLK), I32))
        cnt = cnt + jnp.where(pred(key_ref[n_pages]), 1, 0)
        return jnp.sum(cnt.astype(F32), axis=1, keepdims=True)

    thr = _kth_largest_key(lambda cand: count(lambda key: key >= cand), (s_rows, 1), topk)
    n_gt = count(lambda key: key > thr)
    n_eq = count(lambda key: key == thr)
    room = jnp.float32(topk) - n_gt
    tie_break = jnp.max(n_eq - room) > 0.0

    q_heads = _heads((aq_ref[...] * (HEAD_DIM ** -0.5)).astype(BF16))
    lower = (_iota((BLK, BLK), 0) < _iota((BLK, BLK), 1)).astype(BF16)

    def logits_block(ties, p, qk, valid, kpos0, st):
        seen = st[0]
        key = key_ref[p]
        if ties:
            eq = key == thr
            rank = seen + jnp.dot(eq.astype(BF16), lower, preferred_element_type=F32)
            sel = jnp.logical_or(key > thr, jnp.logical_and(eq, rank < room))
            seen = seen + jnp.sum(eq.astype(F32), axis=1, keepdims=True)
        else:
            sel = key >= thr
        if valid is not None:
            sel = jnp.logical_and(sel, valid)
        dist = ((past - kpos0) + ri - ci).astype(F32)
        out = [seen]
        for h in range(N_H):
            s = jnp.where(sel, qk(h) - _alibi_slope(h) * dist, MASKED)
            s_ref[p, h] = s
            out.append(jnp.maximum(st[1 + h], s))
        return tuple(out)

    qk_new, pv_new = _natural_kv(q_heads, akn_ref[0], avn_ref[0])

    def pass_a(ties):
        st = stream(k_copy,
                    lambda p, slot, st: logits_block(ties, p, lambda h: _dot(q_heads[h], kbuf[slot, h]), None,
                                                     p * BLK, st),
                    (jnp.zeros((s_rows, 1), F32),) + (jnp.full((s_rows, BLK), MASKED, F32),) * N_H)
        return logits_block(ties, n_pages, qk_new, new_valid, past, st)[1:]

    run_max = lax.cond(tie_break, lambda: pass_a(True), lambda: pass_a(False))
    m_fin = [jnp.max(run_max[h], axis=1, keepdims=True) for h in range(N_H)]

    def values_block(p, pv, st):
        out_l, out_a = [], []
        for h in range(N_H):
            pr = jnp.exp(s_ref[p, h] - m_fin[h])
            out_l.append(st[h] + pr)
            out_a.append(st[N_H + h] + pv(h, pr))
        return tuple(out_l + out_a)

    st = stream(v_copy,
                lambda p, slot, st: values_block(p, lambda h, pr: _dot_nt(pr, vbuf[slot, h]), st),
                (jnp.zeros((s_rows, BLK), F32),) * N_H + (jnp.zeros((s_rows, HEAD_DIM), F32),) * N_H)
    st = values_block(n_pages, pv_new, st)
    for h in range(N_H):
        o_ref[:, h * HEAD_DIM:(h + 1) * HEAD_DIM] = st[N_H + h] / jnp.sum(st[h], axis=1, keepdims=True)


def _dsa_sample(page_table, aq, iq, small, ik_new_pad, ak_new_pad, av_new_pad, cache_it, cache_kt, cache_vt,
                layer, bs, s):
    n_pages = page_table.shape[1]
    page = cache_kt.shape[-1]
    topk = min(TOPK_MAX, (n_pages * page + s) // 4)
    row = lambda w: pl.BlockSpec((s, w), lambda b, pt: (b, 0))
    new = lambda w: pl.BlockSpec((1, BLK, w), lambda b, pt: (b, 0, 0))
    hbm = pl.BlockSpec(memory_space=pl.ANY)
    return pl.pallas_call(
        functools.partial(_dsas_kernel, layer=layer, n_pages=n_pages, topk=topk),
        out_shape=jax.ShapeDtypeStruct(aq.shape, F32),
        grid_spec=pltpu.PrefetchScalarGridSpec(
            num_scalar_prefetch=1,
            grid=(bs,),
            in_specs=[row(W_GROUP), row(W_GROUP), row(LANES), new(IDX_DIM), new(W_GROUP), new(W_GROUP),
                      hbm, hbm, hbm],
            out_specs=row(W_GROUP),
            scratch_shapes=[pltpu.VMEM((n_pages + 1, s, BLK), I32), pltpu.VMEM((n_pages + 1, N_H, s, BLK), F32),
                            pltpu.VMEM((PAGE_RING * PAGE_UNROLL, IDX_DIM, page), F32),
                            pltpu.VMEM((PAGE_RING * PAGE_UNROLL, N_H, HEAD_DIM, page), F32),
                            pltpu.VMEM((PAGE_RING * PAGE_UNROLL, N_H, HEAD_DIM, page), F32),
                            pltpu.SemaphoreType.DMA((3, PAGE_RING * PAGE_UNROLL))]),
        compiler_params=_cparams(("arbitrary",)),
        name="dsa_sample",
    )(page_table, aq, iq, small, ik_new_pad, ak_new_pad, av_new_pad, cache_it, cache_kt, cache_vt)


def _gdn_kernel(x_ref, sm_ref, gate_ref, tail0_ref, s0_ref, cw_ref, alog_ref, dtb_ref, ng_ref,
                o_ref, sout_ref, s_scr, tail_scr, *, chunk):
    step = pl.program_id(1)
    rb = x_ref.shape[0]
    n_chunks = rb // chunk

    @pl.when(step == 0)
    def _():
        s_scr[...] = s0_ref[0]
        tail_scr[...] = tail0_ref[0]

    x = x_ref[...]
    xfull = jnp.concatenate([tail_scr[...], x], axis=0)
    y = xfull[8:8 + rb] * cw_ref[CONV_W - 1:CONV_W, :]
    for j in range(CONV_W - 1):
        y = y + pltpu.roll(xfull, CONV_W - 1 - j, 0)[8:8 + rb] * cw_ref[j:j + 1, :]
    tail_scr[...] = xfull[rb:rb + 8]
    c = _silu(y)

    sm = sm_ref[...]
    beta_all = _sigmoid(sm)
    g_all = -jnp.exp(alog_ref[...]) * _softplus(sm + dtb_ref[...])
    rbp = ((rb + LANES - 1) // LANES) * LANES
    g_pad = g_all if rbp == rb else jnp.concatenate([g_all, jnp.zeros((rbp - rb, LANES), F32)], axis=0)
    g_t = g_pad.T[:, :rb]

    ri, ci = _iota((rb, rb), 0), _iota((rb, rb), 1)
    incl, strict, upper = ci <= ri, ci < ri, ri <= ci
    if n_chunks > 1:
        sh = int(math.log2(chunk))
        same = (ri >> sh) == (ci >> sh)
        incl, strict, upper = incl & same, strict & same, upper & same
    eye = (ri == ci).astype(F32)

    def same_blk(size):
        sh_b = int(math.log2(size))
        return (ri >> sh_b) == (ci >> sh_b)

    gcol_all = _dot_precise_rhs(incl.astype(BF16), g_all)
    grow_all = _dot_precise_lhs(g_t, upper.astype(BF16))
    gate = gate_ref[...]
    n_sq = int(math.log2(chunk))
    scale = HEAD_DIM ** -0.5

    for h in range(N_H):
        hs = slice(h * HEAD_DIM, (h + 1) * HEAD_DIM)
        qh = c[:, h * HEAD_DIM:(h + 1) * HEAD_DIM]
        kh = c[:, W_GROUP + h * HEAD_DIM:W_GROUP + (h + 1) * HEAD_DIM]
        vh = c[:, 2 * W_GROUP + h * HEAD_DIM:2 * W_GROUP + (h + 1) * HEAD_DIM]
        qh = qh * lax.rsqrt(jnp.sum(qh * qh, axis=1, keepdims=True) + NORM_EPS) * scale
        kh = kh * lax.rsqrt(jnp.sum(kh * kh, axis=1, keepdims=True) + NORM_EPS)
        bh = beta_all[:, SMALL_BETA + h:SMALL_BETA + h + 1]
        gcol = gcol_all[:, SMALL_A + h:SMALL_A + h + 1]
        grow = grow_all[SMALL_A + h:SMALL_A + h + 1, :]
        decay = jnp.where(incl, jnp.exp(jnp.where(incl, gcol - grow, 0.0)), 0.0)
        kb = kh * bh
        m = jnp.where(strict, _dot_nt(kb, kh) * decay, 0.0)
        size = min(chunk, GDN_BASE)
        pw = -jnp.where(same_blk(size), m, 0.0)
        t = eye + pw
        for _ in range(int(math.log2(size)) - 1):
            pw = _dot(pw, pw)
            t = t + _dot(t, pw)
        while size < chunk:
            off = jnp.where(jnp.logical_and(same_blk(2 * size), jnp.logical_not(same_blk(size))), m, 0.0)
            t = t - _dot(_dot(t, off), t)
            size *= 2
        u = _dot(t, vh * bh)
        w = _dot(t, kb * jnp.exp(gcol))
        attn = jnp.where(incl, _dot_nt(qh, kh) * decay, 0.0)
        qg = qh * jnp.exp(gcol)
        s_h = s_scr[h]
        vnews, inters = [], []
        for cidx in range(n_chunks):
            rows = slice(cidx * chunk, (cidx + 1) * chunk)
            vnew = u[rows] - _dot(w[rows], s_h)
            inters.append(_dot(qg[rows], s_h))
            glast = gcol[(cidx + 1) * chunk - 1:(cidx + 1) * chunk, :]
            s_h = s_h * jnp.exp(glast) + _dot_tn(kh[rows] * jnp.exp(glast - gcol[rows]), vnew)
            vnews.append(vnew)
        s_scr[h] = s_h
        vnew_all = vnews[0] if n_chunks == 1 else jnp.concatenate(vnews, axis=0)
        inter = inters[0] if n_chunks == 1 else jnp.concatenate(inters, axis=0)
        o = inter + _dot(attn, vnew_all)
        o = o * lax.rsqrt(jnp.mean(o * o, axis=1, keepdims=True) + NORM_EPS) * ng_ref[:, hs]
        o_ref[:, hs] = o * _silu(gate[:, hs])

    @pl.when(step == pl.num_programs(1) - 1)
    def _():
        sout_ref[0] = s_scr[...]


def _gdn(dqkv, small, dgate, tail0, s0, conv_w, alog_row, dtb_row, ng_row, batch, seq, chunk, rb):
    steps = seq // rb
    row = lambda w: pl.BlockSpec((rb, w), lambda b, i: (b * steps + i, 0))
    const = lambda shp: pl.BlockSpec(shp, lambda b, i: (0,) * len(shp))
    return pl.pallas_call(
        functools.partial(_gdn_kernel, chunk=chunk),
        out_shape=[jax.ShapeDtypeStruct((dqkv.shape[0], W_GROUP), F32),
                   jax.ShapeDtypeStruct((batch, N_H, HEAD_DIM, HEAD_DIM), F32)],
        grid=(batch, steps),
        in_specs=[row(3 * W_GROUP), row(LANES), row(W_GROUP),
                  pl.BlockSpec((1, 8, 3 * W_GROUP), lambda b, i: (b, 0, 0)),
                  pl.BlockSpec((1, N_H, HEAD_DIM, HEAD_DIM), lambda b, i: (b, 0, 0, 0)),
                  const((CONV_W, 3 * W_GROUP)), const((1, LANES)), const((1, LANES)), const((1, W_GROUP))],
        out_specs=[row(W_GROUP), pl.BlockSpec((1, N_H, HEAD_DIM, HEAD_DIM), lambda b, i: (b, 0, 0, 0))],
        scratch_shapes=[pltpu.VMEM((N_H, HEAD_DIM, HEAD_DIM), F32), pltpu.VMEM((8, 3 * W_GROUP), F32)],
        compiler_params=_cparams(("parallel", "arbitrary")),
        name="gdn",
    )(dqkv, small, dgate, tail0, s0, conv_w, alog_row, dtb_row, ng_row)


def _lane_row(vec, offset):
    return jnp.zeros((1, LANES), F32).at[0, offset:offset + vec.shape[0]].set(vec)


def _pad_rows(x, bs, s, rows):
    w = x.shape[-1]
    return jnp.concatenate([x.reshape(bs, s, w), jnp.zeros((bs, rows - s, w), x.dtype)], axis=1)


def kernel(x_prompt, x_sample, cache_dsa_k, cache_dsa_v, cache_dsa_kidx, cache_sb_k, cache_sb_v, state_gdn_S,
           state_gdn_conv, page_table, c_prompt, c_sample, w_cond, b_cond, w_in, w_out, ln_g, ln_b, conv_w, a_log,
           dt_bias, gdn_norm_g, gmlp_w_s, gmlp_b_s, w_ff1, w_ff2):
    bp, seq, d = x_prompt.shape
    bs, s, _ = x_sample.shape
    depth = w_in.shape[0]
    page = cache_dsa_k.shape[2]
    alpha = (2 * depth) ** 0.25
    assert seq % BLK == 0 and page == BLK and s >= CONV_W - 1 and s % 8 == 0 and s <= BLK

    c_all = jnp.concatenate([c_prompt, c_sample], axis=0)
    rc = ((c_all.shape[0] + 7) // 8) * 8
    c_all = jnp.concatenate([c_all, jnp.zeros((rc - c_all.shape[0], d), F32)], axis=0)
    mod = _cond(c_all, w_cond, b_cond)

    cache_it = jnp.transpose(cache_dsa_kidx, (0, 1, 3, 2))
    cache_akt, cache_avt, cache_ckt, cache_cvt = (jnp.transpose(t, (0, 1, 3, 4, 2))
                                                  for t in (cache_dsa_k, cache_dsa_v, cache_sb_k, cache_sb_v))

    xp = x_prompt.reshape(bp * seq, d)
    xs = x_sample.reshape(bs * s, d)
    c_gmlp = min(seq, GMLP_CHUNK)
    cs_gmlp = min(s, GMLP_CHUNK)
    st_p, st_s = [], []
    for l in range(depth):
        w_pad = _pad_w_in(w_in[l])
        wout_b, w1_b, w2_b = w_out[l].astype(BF16), w_ff1[l].astype(BF16), w_ff2[l].astype(BF16)
        alog_row, dtb_row = _lane_row(a_log[l], SMALL_A), _lane_row(dt_bias[l], SMALL_A)
        ng_row = jnp.tile(gdn_norm_g[l], N_H).reshape(1, W_GROUP)
        pre_ln = l == 0

        assert bp == 1
        mods = [mod[l, 0:1, i * d:(i + 1) * d] for i in range(6)]
        pr = _proj(xp, mods[0], mods[1], w_pad, pre_ln, 512 if (bp * seq) % 512 == 0 else BLK, True)
        out_a = _dsa_prompt(pr["aq"], pr["iq"], pr["small"], pr["ik_b"], pr["ak_b"], pr["av_t"], bp, seq)
        out_b, _ = _gmlp(pr["bu"], pr["bv"], gmlp_w_s[l][:, :c_gmlp, :c_gmlp], gmlp_b_s[l][:, :c_gmlp].T,
                         c_gmlp, c_gmlp, False)
        out_c = _sb_prompt(pr["cq"], pr["ck_b"], pr["cv_b"], bp, seq)
        gchunk = math.gcd(seq, GDN_CHUNK)
        out_d, s_new = _gdn(pr["dqkv"], pr["small"], pr["dgate"], jnp.zeros((bp, 8, 3 * W_GROUP), F32),
                            jnp.zeros((bp, N_H, HEAD_DIM, HEAD_DIM), F32), conv_w[l], alog_row, dtb_row, ng_row,
                            bp, seq, gchunk, 256 if seq % 256 == 0 else gchunk)
        xp = _outffn(xp, (out_a, out_b, out_c, out_d), mods[2], mods[3], mods[4], mods[5], wout_b, ln_g[l], ln_b[l],
                     w1_b, w2_b, pre_ln, alpha, tm=256 if (bp * seq) % 256 == 0 else BLK)
        buf_new = pr["dqkv"].reshape(bp, seq, 3 * W_GROUP)[:, seq - (CONV_W - 1):]
        st_p.append((pr["ak"].reshape(bp, seq, N_H, HEAD_DIM), pr["av"].reshape(bp, seq, N_H, HEAD_DIM),
                     pr["ik"].reshape(bp, seq, IDX_DIM), pr["ck"].reshape(bp, seq, N_H, HEAD_DIM),
                     pr["cv"].reshape(bp, seq, N_H, HEAD_DIM), s_new, buf_new))

        mods = [jnp.repeat(mod[l, bp:bp + bs, i * d:(i + 1) * d], s, axis=0) for i in range(6)]
        ps = _proj(xs, mods[0], mods[1], w_pad, pre_ln, bs * s, False)
        out_a = _dsa_sample(page_table, ps["aq"], ps["iq"], ps["small"], _pad_rows(ps["ik"], bs, s, BLK),
                            _pad_rows(ps["ak"], bs, s, BLK), _pad_rows(ps["av"], bs, s, BLK),
                            cache_it, cache_akt, cache_avt, l, bs, s)
        out_b, v_rows = _gmlp(ps["bu"], ps["bv"], jnp.tile(gmlp_w_s[l][:, :cs_gmlp, :cs_gmlp], (1, bs, bs)),
                              jnp.tile(gmlp_b_s[l][:, :cs_gmlp].T, (bs, 1)), bs * s, cs_gmlp, True)
        out_c = _sb_sample(page_table, ps["cq"], _pad_rows(ps["ck"], bs, s, BLK), _pad_rows(ps["cv"], bs, s, BLK),
                           cache_ckt, cache_cvt, l, bs, s)
        tail0 = jnp.concatenate([jnp.zeros((bs, 8 - (CONV_W - 1), 3 * W_GROUP), F32), state_gdn_conv[l]], axis=1)
        out_d, s_new = _gdn(ps["dqkv"], ps["small"], ps["dgate"], tail0, state_gdn_S[l], conv_w[l], alog_row,
                            dtb_row, ng_row, bs, s, math.gcd(s, GDN_CHUNK), s)
        xs = _outffn(xs, (out_a, out_b, out_c, out_d), mods[2], mods[3], mods[4], mods[5], wout_b, ln_g[l], ln_b[l],
                     w1_b, w2_b, pre_ln, alpha, tm=bs * s)
        buf_new = ps["dqkv"].reshape(bs, s, 3 * W_GROUP)[:, s - (CONV_W - 1):]
        st_s.append((ps["ak"].reshape(bs, s, N_H, HEAD_DIM), ps["av"].reshape(bs, s, N_H, HEAD_DIM),
                     ps["ik"].reshape(bs, s, IDX_DIM), ps["ck"].reshape(bs, s, N_H, HEAD_DIM),
                     ps["cv"].reshape(bs, s, N_H, HEAD_DIM), s_new, buf_new, v_rows.reshape(bs, s, W_GROUP)))

    outs_p = tuple(jnp.stack(t) for t in zip(*st_p))
    outs_s = tuple(jnp.stack(t) for t in zip(*st_s))
    return (xp.reshape(bp, seq, d), xs.reshape(bs, s, d)) + outs_p + outs_s
```

```python
import functools
import math

import jax
import jax.numpy as jnp
from jax import lax
from jax.experimental import pallas as pl
from jax.experimental.pallas import tpu as pltpu

F32 = jnp.float32
BF16 = jnp.bfloat16
I32 = jnp.int32

LN_EPS = 1e-5
NORM_EPS = 1e-6
N_H = 4
HEAD_DIM = 64
W_GROUP = N_H * HEAD_DIM
IDX_HEADS = 4
IDX_DIM = 64
TOPK_MAX = 256
GMLP_CHUNK = 128
GDN_CHUNK = 64
CONV_W = 4
BLK = 128
LANES = 128
VMEM_LIMIT = 56 * 1024 * 1024
SB_ZERO_TAIL = -110.0
INT_MIN = -2 ** 31
INT_MAX = 2 ** 31 - 1
LOG2E = 1.4426950408889634
MASKED = -1e30
M_INIT = -1e20
UNROLL = 8
PAGE_UNROLL = 4
SB_RING = 3
PAGE_RING = 8
COUNT_UNROLL = 4
GDN_BASE = 8

PIECES = (("dqkv", 768, 768), ("aq", 256, 256), ("ak", 256, 256), ("av", 256, 256), ("iq", 256, 256),
          ("bu", 256, 256), ("bv", 256, 256), ("cq", 256, 256), ("ck", 256, 256), ("cv", 256, 256),
          ("dgate", 256, 256), ("ik", 64, 128), ("small", 12, 128))
N_PAD = sum(p[2] for p in PIECES)
SMALL_IW, SMALL_BETA, SMALL_A = 0, 4, 8
BF16_COPIES = ("ak", "ck", "cv", "ik")


def _cparams(sem, vmem=VMEM_LIMIT):
    return pltpu.CompilerParams(dimension_semantics=sem, vmem_limit_bytes=vmem)


def _ln_plain(x):
    mu = jnp.mean(x, axis=-1, keepdims=True)
    xc = x - mu
    var = jnp.mean(xc * xc, axis=-1, keepdims=True)
    return xc * lax.rsqrt(var + LN_EPS)


def _sigmoid(x):
    return 1.0 / (1.0 + jnp.exp(-x))


def _silu(x):
    return x * _sigmoid(x)


def _softplus(x):
    return jnp.maximum(x, 0.0) + jnp.log1p(jnp.exp(-jnp.abs(x)))


def _gelu_tanh(x):
    c = math.sqrt(2.0 / math.pi)
    return x * (0.5 * (1.0 + jnp.tanh(c * (x + 0.044715 * (x * x * x)))))


def _dot(a, b):
    return jnp.dot(a.astype(BF16), b.astype(BF16), preferred_element_type=F32)


def _dot_nt(a, b):
    return lax.dot_general(a.astype(BF16), b.astype(BF16), (((1,), (1,)), ((), ())),
                           preferred_element_type=F32)


def _dot_tn(a, b):
    return lax.dot_general(a.astype(BF16), b.astype(BF16), (((0,), (0,)), ((), ())),
                           preferred_element_type=F32)


def _split(x):
    hi = x.astype(BF16)
    lo = (x - hi.astype(F32)).astype(BF16)
    return hi, lo


def _dot_precise_lhs(a, b01):
    hi, lo = _split(a)
    return jnp.dot(hi, b01, preferred_element_type=F32) + jnp.dot(lo, b01, preferred_element_type=F32)


def _dot_precise_rhs(a01, b):
    hi, lo = _split(b)
    return jnp.dot(a01, hi, preferred_element_type=F32) + jnp.dot(a01, lo, preferred_element_type=F32)


def _dot3(a, b):
    ah, al = _split(a)
    bh, bl = _split(b)
    return (jnp.dot(ah, bh, preferred_element_type=F32) + jnp.dot(al, bh, preferred_element_type=F32)
            + jnp.dot(ah, bl, preferred_element_type=F32))


def _iota(shape, dim):
    return lax.broadcasted_iota(I32, shape, dim)


def _sort_key(score):
    bits = pltpu.bitcast(score, I32)
    return bits ^ ((bits >> 31) & jnp.int32(0x7FFFFFFF))


def _alibi_slope(h):
    return 2.0 ** (-8.0 * (h + 1) / N_H)


def _heads(x):
    return [x[:, h * HEAD_DIM:(h + 1) * HEAD_DIM] for h in range(N_H)]


def _cond_kernel(c_ref, w_ref, b_ref, o_ref):
    s = _silu(c_ref[...])
    o_ref[0] = _dot3(s, w_ref[0]) + b_ref[0]


def _cond(c_all, w_cond, b_cond):
    depth, d, n6 = w_cond.shape
    rc = c_all.shape[0]
    tn = 1536 if n6 % 1536 == 0 else n6
    return pl.pallas_call(
        _cond_kernel,
        out_shape=jax.ShapeDtypeStruct((depth, rc, n6), F32),
        grid=(depth, n6 // tn),
        in_specs=[pl.BlockSpec((rc, d), lambda l, j: (0, 0)),
                  pl.BlockSpec((1, d, tn), lambda l, j: (l, 0, j)),
                  pl.BlockSpec((1, 1, tn), lambda l, j: (l, 0, j))],
        out_specs=pl.BlockSpec((1, rc, tn), lambda l, j: (l, 0, j)),
        compiler_params=_cparams(("arbitrary", "arbitrary")),
        name="cond",
    )(c_all, w_cond, b_cond.reshape(depth, 1, n6))


def _proj_kernel(x_ref, sh_ref, sc_ref, w_ref, *o_refs, pre_ln, extras):
    x = x_ref[...]
    if pre_ln:
        x = _ln_plain(x)
    hb = (x * (1.0 + sc_ref[...]) + sh_ref[...]).astype(BF16)
    col = 0
    for (name, width, padded), o_ref in zip(PIECES, o_refs[:len(PIECES)]):
        res = jnp.dot(hb, w_ref[:, col:col + padded], preferred_element_type=F32)
        o_ref[...] = res if o_ref.shape[-1] == padded else res[:, :o_ref.shape[-1]]
        if extras and name in BF16_COPIES:
            o_refs[len(PIECES) + BF16_COPIES.index(name)][...] = res.astype(BF16)
        if extras and name == "av":
            vt_ref = o_refs[len(PIECES) + len(BF16_COPIES)]
            for j in range(vt_ref.shape[0]):
                vt_ref[j] = res[j * BLK:(j + 1) * BLK, :].T.astype(BF16)
        col += padded


def _mod_spec(m, tm, d):
    if m.shape[0] == 1:
        return pl.BlockSpec((1, d), lambda i: (0, 0))
    return pl.BlockSpec((tm, d), lambda i: (i, 0))


def _proj(x, shift, scale, w_pad, pre_ln, tm, extras):
    r, d = x.shape
    out_shapes, out_specs = [], []
    for name, width, padded in PIECES:
        w_out = width if name == "ik" else padded
        out_shapes.append(jax.ShapeDtypeStruct((r, w_out), F32))
        out_specs.append(pl.BlockSpec((tm, w_out), lambda i: (i, 0)))
    if extras:
        for name in BF16_COPIES:
            w_out = LANES if name == "ik" else W_GROUP
            out_shapes.append(jax.ShapeDtypeStruct((r, w_out), BF16))
            out_specs.append(pl.BlockSpec((tm, w_out), lambda i: (i, 0)))
        out_shapes.append(jax.ShapeDtypeStruct((r // BLK, W_GROUP, BLK), BF16))
        out_specs.append(pl.BlockSpec((tm // BLK, W_GROUP, BLK), lambda i: (i, 0, 0)))
    outs = pl.pallas_call(
        functools.partial(_proj_kernel, pre_ln=pre_ln, extras=extras),
        out_shape=out_shapes,
        grid=(r // tm,),
        in_specs=[pl.BlockSpec((tm, d), lambda i: (i, 0)), _mod_spec(shift, tm, d), _mod_spec(scale, tm, d),
                  pl.BlockSpec((d, N_PAD), lambda i: (0, 0))],
        out_specs=out_specs,
        compiler_params=_cparams(("parallel",)),
        name="proj",
    )(x, shift, scale, w_pad)
    res = {name: o for (name, _, _), o in zip(PIECES, outs)}
    if extras:
        for i, name in enumerate(BF16_COPIES):
            res[name + "_b"] = outs[len(PIECES) + i]
        res["av_t"] = outs[len(PIECES) + len(BF16_COPIES)]
    return res


def _pad_w_in(w_in_l):
    off = {}
    start = 0
    for name, n in (("aq", 256), ("ak", 256), ("av", 256), ("iq", 256), ("ik", 64), ("iw", 4), ("bu", 256),
                    ("bv", 256), ("cq", 256), ("ck", 256), ("cv", 256), ("dqkv", 768), ("dbeta", 4), ("da", 4),
                    ("dgate", 256)):
        off[name] = (start, n)
        start += n
    d = w_in_l.shape[0]
    cols = []
    for name, width, padded in PIECES:
        if name == "small":
            parts = [w_in_l[:, off[k][0]:off[k][0] + off[k][1]] for k in ("iw", "dbeta", "da")]
            piece = jnp.concatenate(parts, axis=1)
        else:
            piece = w_in_l[:, off[name][0]:off[name][0] + off[name][1]]
        if padded > piece.shape[1]:
            piece = jnp.concatenate([piece, jnp.zeros((d, padded - piece.shape[1]), piece.dtype)], axis=1)
        cols.append(piece)
    return jnp.concatenate(cols, axis=1).astype(BF16)


def _outffn_kernel(x_ref, oa_ref, ob_ref, oc_ref, od_ref, gm_ref, shf_ref, scf_ref, gf_ref,
                   wout_ref, lng_ref, lnb_ref, w1_ref, w2_ref, o_ref, *, pre_ln, alpha):
    x = x_ref[...]
    if pre_ln:
        x = _ln_plain(x)
    mo = None
    for g, r in enumerate((oa_ref, ob_ref, oc_ref, od_ref)):
        part = _dot(r[...], wout_ref[g * W_GROUP:(g + 1) * W_GROUP, :])
        mo = part if mo is None else mo + part
    x1 = _ln_plain(alpha * x + gm_ref[...] * mo) * lng_ref[0:1, :] + lnb_ref[0:1, :]
    hf = x1 * (1.0 + scf_ref[...]) + shf_ref[...]
    a = jnp.maximum(_dot(hf, w1_ref[...]), 0.0)
    ff = _dot(a * a, w2_ref[...])
    o_ref[...] = _ln_plain(alpha * x1 + gf_ref[...] * ff) * lng_ref[1:2, :] + lnb_ref[1:2, :]


def _outffn(x, branches, gm, shf, scf, gf, wout_b, ln_g, ln_b, w1_b, w2_b, pre_ln, alpha, tm):
    r, d = x.shape
    dff = w1_b.shape[1]
    row = lambda w: pl.BlockSpec((tm, w), lambda i: (i, 0))
    const = lambda shp: pl.BlockSpec(shp, lambda i: (0, 0), pipeline_mode=pl.Buffered(1))
    return pl.pallas_call(
        functools.partial(_outffn_kernel, pre_ln=pre_ln, alpha=alpha),
        out_shape=jax.ShapeDtypeStruct((r, d), F32),
        grid=(r // tm,),
        in_specs=[row(d)] + [row(W_GROUP)] * 4 + [_mod_spec(m, tm, d) for m in (gm, shf, scf, gf)]
        + [const((d, d)), const((2, d)), const((2, d)), const((d, dff)), const((dff, d))],
        out_specs=row(d),
        compiler_params=_cparams(("parallel",)),
        name="outffn",
    )(x, *branches, gm, shf, scf, gf, wout_b, ln_g, ln_b, w1_b, w2_b)


def _gmlp_kernel(u_ref, v_ref, w_ref, bt_ref, ob_ref, *vn_refs, chunk):
    u = _gelu_tanh(u_ref[...])
    vn = _ln_plain(_gelu_tanh(v_ref[...]))
    if vn_refs:
        vn_refs[0][...] = vn
    rows = u.shape[0]
    ri, ci = _iota((rows, rows), 0), _iota((rows, rows), 1)
    mask = ci <= ri
    if rows != chunk:
        sh = int(math.log2(chunk))
        mask = mask & ((ri >> sh) == (ci >> sh))
    vb = vn.astype(BF16)
    for g in range(N_H):
        w = jnp.where(mask, w_ref[g], 0.0).astype(BF16)
        mixed = jnp.dot(w, vb[:, g * HEAD_DIM:(g + 1) * HEAD_DIM], preferred_element_type=F32)
        mixed = mixed + bt_ref[:, g:g + 1]
        ob_ref[:, g * HEAD_DIM:(g + 1) * HEAD_DIM] = u[:, g * HEAD_DIM:(g + 1) * HEAD_DIM] * mixed


def _gmlp(bu, bv, w_tiled, bt_tiled, rows, chunk, want_vn):
    r = bu.shape[0]
    row = pl.BlockSpec((rows, W_GROUP), lambda i: (i, 0))
    out_shape = [jax.ShapeDtypeStruct((r, W_GROUP), F32)]
    out_specs = [row]
    if want_vn:
        out_shape.append(jax.ShapeDtypeStruct((r, W_GROUP), F32))
        out_specs.append(row)
    outs = pl.pallas_call(
        functools.partial(_gmlp_kernel, chunk=chunk),
        out_shape=out_shape,
        grid=(r // rows,),
        in_specs=[row, row, pl.BlockSpec((N_H, rows, rows), lambda i: (0, 0, 0)),
                  pl.BlockSpec((rows, N_H), lambda i: (0, 0))],
        out_specs=out_specs,
        compiler_params=_cparams(("parallel",)),
        name="gmlp",
    )(bu, bv, w_tiled, bt_tiled)
    return outs if want_vn else (outs[0], None)


def _sb_block(qk, pv, vis, carry, acc, upper):
    new_carry, new_acc = [], []
    for h in range(N_H):
        z = qk(h)
        l_raw = -_softplus(z)
        l_vis = l_raw if vis is None else jnp.where(vis, l_raw, 0.0)
        tail = _dot_precise_lhs(l_vis, upper) + carry[h]
        w = jnp.exp(z + l_raw + tail)
        if vis is not None:
            w = jnp.where(vis, w, 0.0)
        new_acc.append(acc[h] + pv(h, w))
        new_carry.append(carry[h] + jnp.sum(l_vis, axis=1, keepdims=True))
    return new_carry, new_acc


def _sb_live(carry):
    m = carry[0]
    for c in carry[1:]:
        m = jnp.maximum(m, c)
    return (jnp.max(m) > SB_ZERO_TAIL).astype(I32)


def _upper_ones():
    return (_iota((BLK, BLK), 0) > _iota((BLK, BLK), 1)).astype(BF16)


def _natural_kv(q_heads, k, v):
    kh, vh = _heads(k), _heads(v)
    return (lambda h: _dot_nt(q_heads[h], kh[h])), (lambda h, w: _dot(w, vh[h]))


def _sbp_kernel(q_ref, k_ref, v_ref, o_ref):
    qb = pl.program_id(1)
    q_heads = _heads((q_ref[...] * (HEAD_DIM ** -0.5)).astype(BF16))
    upper = _upper_ones()
    ri, ci = _iota((BLK, BLK), 0), _iota((BLK, BLK), 1)
    zero_c = [jnp.zeros((BLK, 1), F32)] * N_H
    zero_a = [jnp.zeros((BLK, HEAD_DIM), F32)] * N_H
    off = pl.multiple_of(qb * BLK, BLK)
    qk, pv = _natural_kv(q_heads, k_ref[pl.ds(off, BLK), :], v_ref[pl.ds(off, BLK), :])
    carry, acc = _sb_block(qk, pv, ci < ri, zero_c, zero_a, upper)

    def cond(st):
        return jnp.logical_and(st[0] >= 0, st[1] > 0)

    def body(st):
        kb = st[0]
        o = pl.multiple_of(kb * BLK, BLK)
        qk, pv = _natural_kv(q_heads, k_ref[pl.ds(o, BLK), :], v_ref[pl.ds(o, BLK), :])
        c, a = _sb_block(qk, pv, None, list(st[2:2 + N_H]), list(st[2 + N_H:]), upper)
        return (kb - 1, _sb_live(c), *c, *a)

    st = lax.while_loop(cond, body, (qb - 1, _sb_live(carry), *carry, *acc))
    for h in range(N_H):
        o_ref[:, h * HEAD_DIM:(h + 1) * HEAD_DIM] = st[2 + N_H + h]


def _sb_prompt(cq, ck_b, cv_b, batch, seq):
    nq = seq // BLK
    full = pl.BlockSpec((seq, W_GROUP), lambda b, i: (b, 0), pipeline_mode=pl.Buffered(1))
    return pl.pallas_call(
        _sbp_kernel,
        out_shape=jax.ShapeDtypeStruct(cq.shape, F32),
        grid=(batch, nq),
        in_specs=[pl.BlockSpec((BLK, W_GROUP), lambda b, i: (b * nq + i, 0)), full, full],
        out_specs=pl.BlockSpec((BLK, W_GROUP), lambda b, i: (b * nq + i, 0)),
        compiler_params=_cparams(("parallel", "arbitrary")),
        name="sb_prompt",
    )(cq, ck_b, cv_b)


def _sbs_kernel(pt_ref, q_ref, kn_ref, vn_ref, kc_ref, vc_ref, o_ref, kbuf, vbuf, sem, *, layer, n_pages):
    b = pl.program_id(0)
    rows = q_ref.shape[0]
    q_heads = _heads((q_ref[...] * (HEAD_DIM ** -0.5)).astype(BF16))
    upper = _upper_ones()
    ri, ci = _iota((rows, BLK), 0), _iota((rows, BLK), 1)
    zero_c = [jnp.zeros((rows, 1), F32)] * N_H
    zero_a = [jnp.zeros((rows, HEAD_DIM), F32)] * N_H
    def copies(p):
        page, slot = pt_ref[b, p], p % SB_RING
        return (pltpu.make_async_copy(kc_ref.at[layer, page], kbuf.at[slot], sem.at[0, slot]),
                pltpu.make_async_copy(vc_ref.at[layer, page], vbuf.at[slot], sem.at[1, slot]))

    def start(p):
        for c in copies(p):
            c.start()

    def wait(p):
        for c in copies(p):
            c.wait()

    for j in range(1, min(SB_RING, n_pages + 1)):
        start(n_pages - j)
    qk, pv = _natural_kv(q_heads, kn_ref[0], vn_ref[0])
    carry, acc = _sb_block(qk, pv, ci < ri, zero_c, zero_a, upper)

    def cond(st):
        return jnp.logical_and(st[0] >= 0, st[1] > 0)

    def body(st):
        p = st[0]
        slot = p % SB_RING
        wait(p)

        @pl.when(p - (SB_RING - 1) >= 0)
        def _():
            start(p - (SB_RING - 1))

        c, a = _sb_block(lambda h: _dot(q_heads[h], kbuf[slot, h]), lambda h, w: _dot_nt(w, vbuf[slot, h]), None,
                         list(st[2:2 + N_H]), list(st[2 + N_H:]), upper)
        return (p - 1, _sb_live(c), *c, *a)

    st = lax.while_loop(cond, body, (jnp.int32(n_pages - 1), _sb_live(carry), *carry, *acc))
    for j in range(SB_RING - 1):
        @pl.when(st[0] - j >= 0)
        def _():
            wait(st[0] - j)

    for h in range(N_H):
        o_ref[:, h * HEAD_DIM:(h + 1) * HEAD_DIM] = st[2 + N_H + h]


def _sb_sample(page_table, cq, ck_new_pad, cv_new_pad, cache_kt, cache_vt, layer, bs, s):
    n_pages = page_table.shape[1]
    page = cache_kt.shape[-1]
    return pl.pallas_call(
        functools.partial(_sbs_kernel, layer=layer, n_pages=n_pages),
        out_shape=jax.ShapeDtypeStruct(cq.shape, F32),
        grid_spec=pltpu.PrefetchScalarGridSpec(
            num_scalar_prefetch=1,
            grid=(bs,),
            in_specs=[pl.BlockSpec((s, W_GROUP), lambda b, pt: (b, 0)),
                      pl.BlockSpec((1, BLK, W_GROUP), lambda b, pt: (b, 0, 0)),
                      pl.BlockSpec((1, BLK, W_GROUP), lambda b, pt: (b, 0, 0)),
                      pl.BlockSpec(memory_space=pl.ANY), pl.BlockSpec(memory_space=pl.ANY)],
            out_specs=pl.BlockSpec((s, W_GROUP), lambda b, pt: (b, 0)),
            scratch_shapes=[pltpu.VMEM((SB_RING, N_H, HEAD_DIM, page), F32),
                            pltpu.VMEM((SB_RING, N_H, HEAD_DIM, page), F32),
                            pltpu.SemaphoreType.DMA((2, SB_RING))]),
        compiler_params=_cparams(("arbitrary",)),
        name="sb_sample",
    )(page_table, cq, ck_new_pad, cv_new_pad, cache_kt, cache_vt)


def _kth_largest_key(count_ge, shape, k):
    kf = jnp.float32(k)
    t = jnp.where(count_ge(jnp.zeros(shape, I32)) >= kf, jnp.int32(0), jnp.int32(INT_MIN))

    def body(i, t):
        cand = t + (jnp.int32(1) << (30 - i))
        return jnp.where(count_ge(cand) >= kf, cand, t)

    return lax.fori_loop(0, 31, body, t)


def _dsap_kernel(aq_ref, iq_ref, sm_ref, ik_ref, k_ref, vt_ref, o_ref, key_ref, s_scr, *, topk):
    qb = pl.program_id(1)
    nkb = qb + 1
    ri, ci = _iota((BLK, BLK), 0), _iota((BLK, BLK), 1)
    diag_ok = ri <= ci

    iq_t = (iq_ref[...] * (IDX_DIM ** -0.5)).T
    iq_all = jnp.concatenate([iq_t[h * IDX_DIM:(h + 1) * IDX_DIM] for h in range(IDX_HEADS)],
                             axis=1).astype(BF16)
    sm_t = sm_ref[...].T
    iw = [sm_t[SMALL_IW + h:SMALL_IW + h + 1, :] * (IDX_HEADS ** -0.5) for h in range(IDX_HEADS)]

    def score_block(kb, diagonal):
        o = pl.multiple_of(kb * BLK, BLK)
        kidx = ik_ref[pl.ds(o, BLK), :][:, :IDX_DIM]
        s_all = jnp.dot(kidx, iq_all, preferred_element_type=F32)
        score = iw[0] * jnp.maximum(s_all[:, 0:BLK], 0.0)
        for h in range(1, IDX_HEADS):
            score = score + iw[h] * jnp.maximum(s_all[:, h * BLK:(h + 1) * BLK], 0.0)
        if diagonal:
            score = jnp.where(diag_ok, score, -jnp.inf)
        key = _sort_key(score)
        key_ref[kb] = key

    def score_group(g, _):
        for u in range(UNROLL):
            score_block(jnp.minimum(g * UNROLL + u, qb), False)
        return 0

    trips = (qb + UNROLL - 1) // UNROLL
    lax.fori_loop(0, trips, score_group, 0)
    score_block(qb, True)

    def count(pred):
        def one(kb, cnt):
            return cnt + jnp.where(pred(key_ref[kb]), 1, 0)

        def group(g, cnt):
            for u in range(COUNT_UNROLL):
                cnt = one(g * COUNT_UNROLL + u, cnt)
            return cnt

        groups = nkb // COUNT_UNROLL
        cnt = lax.fori_loop(0, groups, group, jnp.zeros((BLK, BLK), I32))
        cnt = lax.fori_loop(groups * COUNT_UNROLL, nkb, one, cnt)
        return jnp.sum(cnt.astype(F32), axis=0, keepdims=True)

    thr = _kth_largest_key(lambda cand: count(lambda key: key >= cand), (1, BLK), topk)
    n_gt = count(lambda key: key > thr)
    n_eq = count(lambda key: key == thr)
    room = jnp.float32(topk) - n_gt
    tie_break = jnp.max(n_eq - room) > 0.0

    aq_t = (aq_ref[...] * (HEAD_DIM ** -0.5 * LOG2E)).T
    bias = [(_alibi_slope(h) * LOG2E) * ri.astype(F32) for h in range(N_H)]
    zeros = jnp.zeros((HEAD_DIM, BLK), F32)
    q_bd = jnp.concatenate(
        [jnp.concatenate([aq_t[g * HEAD_DIM:(g + 1) * HEAD_DIM] if g == h else zeros for g in range(N_H)], axis=0)
         for h in range(N_H)], axis=1).astype(BF16)
    strict_lower = (ci < ri).astype(BF16)

    half = W_GROUP // 2

    def logits_to(slot, kb):
        kblk = k_ref[pl.ds(pl.multiple_of(kb * BLK, BLK), BLK), :]
        for g in range(2):
            s_scr[slot, :, g * 2 * BLK:(g + 1) * 2 * BLK] = jnp.dot(
                kblk[:, g * half:(g + 1) * half], q_bd[g * half:(g + 1) * half, g * 2 * BLK:(g + 1) * 2 * BLK],
                preferred_element_type=F32)

    def attend(ties, idx, slot, diagonal, st):
        kb = jnp.minimum(idx, qb)
        if diagonal:
            thr_b = thr
        else:
            logits_to(1 - slot, jnp.minimum(idx + 1, qb))
            thr_b = jnp.where(idx < qb, thr, jnp.int32(INT_MAX))
        seen = st[0]
        key = key_ref[kb]
        if ties:
            eq = key == thr_b
            rank = seen + jnp.dot(strict_lower, eq.astype(BF16), preferred_element_type=F32)
            sel = jnp.logical_or(key > thr_b, jnp.logical_and(eq, rank < room))
            seen = seen + jnp.sum(eq.astype(F32), axis=0, keepdims=True)
        else:
            sel = key >= thr_b
        if diagonal:
            sel = jnp.logical_and(sel, diag_ok)
        blk_off = ((kb - qb) * BLK).astype(F32)
        vt = vt_ref[kb]
        out = [seen]
        for h in range(N_H):
            m_old, l_old, a_old = st[1 + 3 * h], st[2 + 3 * h], st[3 + 3 * h]
            c_blk = (_alibi_slope(h) * LOG2E) * blk_off
            s = jnp.where(sel, s_scr[slot, :, h * BLK:(h + 1) * BLK] + bias[h], MASKED)
            m_new = jnp.maximum(m_old, jnp.max(s, axis=0, keepdims=True) + c_blk)
            p = jnp.exp2(s - (m_new - c_blk))
            alpha = jnp.exp2(m_old - m_new)
            pv = jnp.dot(vt[h * HEAD_DIM:(h + 1) * HEAD_DIM, :], p.astype(BF16), preferred_element_type=F32)
            out += [m_new, alpha * l_old + jnp.sum(p, axis=0, keepdims=True), alpha * a_old + pv]
        return tuple(out)

    init = [jnp.zeros((1, BLK), F32)]
    for h in range(N_H):
        init += [jnp.full((1, BLK), M_INIT, F32), jnp.zeros((1, BLK), F32), jnp.zeros((HEAD_DIM, BLK), F32)]
    def sweep(ties):
        def group(g, st):
            for u in range(UNROLL):
                st = attend(ties, g * UNROLL + u, u % 2, False, st)
            return st

        logits_to(0, 0)
        st = lax.fori_loop(0, trips, group, tuple(init))
        return attend(ties, qb, 0, True, st)

    st = lax.cond(tie_break, lambda: sweep(True), lambda: sweep(False))
    out_t = jnp.concatenate([st[3 + 3 * h] / st[2 + 3 * h] for h in range(N_H)], axis=0)
    o_ref[...] = out_t.T


def _dsa_prompt(aq, iq, small, ik_b, ak_b, av_t, batch, seq):
    nq = seq // BLK
    topk = min(TOPK_MAX, seq // 4)
    blk = lambda w: pl.BlockSpec((BLK, w), lambda b, i: (b * nq + i, 0))
    full = lambda w: pl.BlockSpec((seq, w), lambda b, i: (b, 0), pipeline_mode=pl.Buffered(1))
    return pl.pallas_call(
        functools.partial(_dsap_kernel, topk=topk),
        out_shape=jax.ShapeDtypeStruct(aq.shape, F32),
        grid=(batch, nq),
        in_specs=[blk(W_GROUP), blk(W_GROUP), blk(LANES), full(LANES), full(W_GROUP),
                  pl.BlockSpec((nq, W_GROUP, BLK), lambda b, i: (b, 0, 0), pipeline_mode=pl.Buffered(1))],
        out_specs=blk(W_GROUP),
        scratch_shapes=[pltpu.VMEM((nq, BLK, BLK), I32), pltpu.VMEM((2, BLK, N_H * BLK), F32)],
        compiler_params=_cparams(("parallel", "arbitrary")),
        name="dsa_prompt",
    )(aq, iq, small, ik_b, ak_b, av_t)


def _dsas_kernel(pt_ref, aq_ref, iq_ref, sm_ref, ikn_ref, akn_ref, avn_ref, ic_ref, kc_ref, vc_ref, o_ref,
                 key_ref, s_ref, ibuf, kbuf, vbuf, sem, *, layer, n_pages, topk):
    b = pl.program_id(0)
    s_rows = aq_ref.shape[0]
    past = n_pages * BLK
    ri, ci = _iota((s_rows, BLK), 0), _iota((s_rows, BLK), 1)
    new_valid = jnp.logical_and(ci <= ri, ci < s_rows)

    iq = iq_ref[...] * (IDX_DIM ** -0.5)
    iq_all = jnp.concatenate(_heads(iq), axis=0).astype(BF16)
    iw = sm_ref[...][:, SMALL_IW:SMALL_IW + IDX_HEADS] * (IDX_HEADS ** -0.5)

    def idx_copy(p, slot):
        return pltpu.make_async_copy(ic_ref.at[layer, pt_ref[b, p]], ibuf.at[slot], sem.at[0, slot])

    def k_copy(p, slot):
        return pltpu.make_async_copy(kc_ref.at[layer, pt_ref[b, p]], kbuf.at[slot], sem.at[1, slot])

    def v_copy(p, slot):
        return pltpu.make_async_copy(vc_ref.at[layer, pt_ref[b, p]], vbuf.at[slot], sem.at[2, slot])

    def score_keys(s_all, valid):
        score = iw[:, 0:1] * jnp.maximum(s_all[0:s_rows], 0.0)
        for h in range(1, IDX_HEADS):
            score = score + iw[:, h:h + 1] * jnp.maximum(s_all[h * s_rows:(h + 1) * s_rows], 0.0)
        if valid is not None:
            score = jnp.where(valid, score, -jnp.inf)
        return _sort_key(score)

    u_n = PAGE_UNROLL
    trips = n_pages // u_n

    def prefetch(copy):
        for g in range(min(PAGE_RING - 1, trips)):
            for u in range(u_n):
                copy(g * u_n + u, g * u_n + u).start()

    def stream(copy, body, carry):
        def trip(g, carry):
            ahead = g + PAGE_RING - 1

            @pl.when(ahead < trips)
            def _():
                for u in range(u_n):
                    copy(ahead * u_n + u, (ahead % PAGE_RING) * u_n + u).start()

            base = (g % PAGE_RING) * u_n
            for u in range(u_n):
                copy(g * u_n + u, base + u).wait()
            for u in range(u_n):
                carry = body(g * u_n + u, base + u, carry)
            return carry

        return lax.fori_loop(0, trips, trip, carry)

    prefetch(idx_copy)
    prefetch(k_copy)
    prefetch(v_copy)

    def p1(p, slot, _):
        key_ref[p] = score_keys(_dot(iq_all, ibuf[slot]), None)
        return 0

    stream(idx_copy, p1, 0)
    key_ref[n_pages] = score_keys(_dot_nt(iq_all, ikn_ref[0]), new_valid)

    def count(pred):
        per_trip = math.gcd(n_pages, 16)

        def body(g, cnt):
            for u in range(per_trip):
                cnt = cnt + jnp.where(pred(key_ref[g * per_trip + u]), 1, 0)
            return cnt
        cnt = lax.fori_loop(0, n_pages // per_trip, body, jnp.zeros((s_rows, BLK), I32))
        cnt = cnt + jnp.where(pred(key_ref[n_pages]), 1, 0)
        return jnp.sum(cnt.astype(F32), axis=1, keepdims=True)

    thr = _kth_largest_key(lambda cand: count(lambda key: key >= cand), (s_rows, 1), topk)
    n_gt = count(lambda key: key > thr)
    n_eq = count(lambda key: key == thr)
    room = jnp.float32(topk) - n_gt
    tie_break = jnp.max(n_eq - room) > 0.0

    q_heads = _heads((aq_ref[...] * (HEAD_DIM ** -0.5)).astype(BF16))
    lower = (_iota((BLK, BLK), 0) < _iota((BLK, BLK), 1)).astype(BF16)

    def logits_block(ties, p, qk, valid, kpos0, st):
        seen = st[0]
        key = key_ref[p]
        if ties:
            eq = key == thr
            rank = seen + jnp.dot(eq.astype(BF16), lower, preferred_element_type=F32)
            sel = jnp.logical_or(key > thr, jnp.logical_and(eq, rank < room))
            seen = seen + jnp.sum(eq.astype(F32), axis=1, keepdims=True)
        else:
            sel = key >= thr
        if valid is not None:
            sel = jnp.logical_and(sel, valid)
        dist = ((past - kpos0) + ri - ci).astype(F32)
        out = [seen]
        for h in range(N_H):
            s = jnp.where(sel, qk(h) - _alibi_slope(h) * dist, MASKED)
            s_ref[p, h] = s
            out.append(jnp.maximum(st[1 + h], s))
        return tuple(out)

    qk_new, pv_new = _natural_kv(q_heads, akn_ref[0], avn_ref[0])

    def pass_a(ties):
        st = stream(k_copy,
                    lambda p, slot, st: logits_block(ties, p, lambda h: _dot(q_heads[h], kbuf[slot, h]), None,
                                                     p * BLK, st),
                    (jnp.zeros((s_rows, 1), F32),) + (jnp.full((s_rows, BLK), MASKED, F32),) * N_H)
        return logits_block(ties, n_pages, qk_new, new_valid, past, st)[1:]

    run_max = lax.cond(tie_break, lambda: pass_a(True), lambda: pass_a(False))
    m_fin = [jnp.max(run_max[h], axis=1, keepdims=True) for h in range(N_H)]

    def values_block(p, pv, st):
        out_l, out_a = [], []
        for h in range(N_H):
            pr = jnp.exp(s_ref[p, h] - m_fin[h])
            out_l.append(st[h] + pr)
            out_a.append(st[N_H + h] + pv(h, pr))
        return tuple(out_l + out_a)

    st = stream(v_copy,
                lambda p, slot, st: values_block(p, lambda h, pr: _dot_nt(pr, vbuf[slot, h]), st),
                (jnp.zeros((s_rows, BLK), F32),) * N_H + (jnp.zeros((s_rows, HEAD_DIM), F32),) * N_H)
    st = values_block(n_pages, pv_new, st)
    for h in range(N_H):
        o_ref[:, h * HEAD_DIM:(h + 1) * HEAD_DIM] = st[N_H + h] / jnp.sum(st[h], axis=1, keepdims=True)


def _dsa_sample(page_table, aq, iq, small, ik_new_pad, ak_new_pad, av_new_pad, cache_it, cache_kt, cache_vt,
                layer, bs, s):
    n_pages = page_table.shape[1]
    page = cache_kt.shape[-1]
    topk = min(TOPK_MAX, (n_pages * page + s) // 4)
    row = lambda w: pl.BlockSpec((s, w), lambda b, pt: (b, 0))
    new = lambda w: pl.BlockSpec((1, BLK, w), lambda b, pt: (b, 0, 0))
    hbm = pl.BlockSpec(memory_space=pl.ANY)
    return pl.pallas_call(
        functools.partial(_dsas_kernel, layer=layer, n_pages=n_pages, topk=topk),
        out_shape=jax.ShapeDtypeStruct(aq.shape, F32),
        grid_spec=pltpu.PrefetchScalarGridSpec(
            num_scalar_prefetch=1,
            grid=(bs,),
            in_specs=[row(W_GROUP), row(W_GROUP), row(LANES), new(IDX_DIM), new(W_GROUP), new(W_GROUP),
                      hbm, hbm, hbm],
            out_specs=row(W_GROUP),
            scratch_shapes=[pltpu.VMEM((n_pages + 1, s, BLK), I32), pltpu.VMEM((n_pages + 1, N_H, s, BLK), F32),
                            pltpu.VMEM((PAGE_RING * PAGE_UNROLL, IDX_DIM, page), F32),
                            pltpu.VMEM((PAGE_RING * PAGE_UNROLL, N_H, HEAD_DIM, page), F32),
                            pltpu.VMEM((PAGE_RING * PAGE_UNROLL, N_H, HEAD_DIM, page), F32),
                            pltpu.SemaphoreType.DMA((3, PAGE_RING * PAGE_UNROLL))]),
        compiler_params=_cparams(("arbitrary",)),
        name="dsa_sample",
    )(page_table, aq, iq, small, ik_new_pad, ak_new_pad, av_new_pad, cache_it, cache_kt, cache_vt)


def _gdn_kernel(x_ref, sm_ref, gate_ref, tail0_ref, s0_ref, cw_ref, alog_ref, dtb_ref, ng_ref,
                o_ref, sout_ref, s_scr, tail_scr, *, chunk):
    step = pl.program_id(1)
    rb = x_ref.shape[0]
    n_chunks = rb // chunk

    @pl.when(step == 0)
    def _():
        s_scr[...] = s0_ref[0]
        tail_scr[...] = tail0_ref[0]

    x = x_ref[...]
    xfull = jnp.concatenate([tail_scr[...], x], axis=0)
    y = xfull[8:8 + rb] * cw_ref[CONV_W - 1:CONV_W, :]
    for j in range(CONV_W - 1):
        y = y + pltpu.roll(xfull, CONV_W - 1 - j, 0)[8:8 + rb] * cw_ref[j:j + 1, :]
    tail_scr[...] = xfull[rb:rb + 8]
    c = _silu(y)

    sm = sm_ref[...]
    beta_all = _sigmoid(sm)
    g_all = -jnp.exp(alog_ref[...]) * _softplus(sm + dtb_ref[...])
    rbp = ((rb + LANES - 1) // LANES) * LANES
    g_pad = g_all if rbp == rb else jnp.concatenate([g_all, jnp.zeros((rbp - rb, LANES), F32)], axis=0)
    g_t = g_pad.T[:, :rb]

    ri, ci = _iota((rb, rb), 0), _iota((rb, rb), 1)
    incl, strict, upper = ci <= ri, ci < ri, ri <= ci
    if n_chunks > 1:
        sh = int(math.log2(chunk))
        same = (ri >> sh) == (ci >> sh)
        incl, strict, upper = incl & same, strict & same, upper & same
    eye = (ri == ci).astype(F32)

    def same_blk(size):
        sh_b = int(math.log2(size))
        return (ri >> sh_b) == (ci >> sh_b)

    gcol_all = _dot_precise_rhs(incl.astype(BF16), g_all)
    grow_all = _dot_precise_lhs(g_t, upper.astype(BF16))
    gate = gate_ref[...]
    n_sq = int(math.log2(chunk))
    scale = HEAD_DIM ** -0.5

    for h in range(N_H):
        hs = slice(h * HEAD_DIM, (h + 1) * HEAD_DIM)
        qh = c[:, h * HEAD_DIM:(h + 1) * HEAD_DIM]
        kh = c[:, W_GROUP + h * HEAD_DIM:W_GROUP + (h + 1) * HEAD_DIM]
        vh = c[:, 2 * W_GROUP + h * HEAD_DIM:2 * W_GROUP + (h + 1) * HEAD_DIM]
        qh = qh * lax.rsqrt(jnp.sum(qh * qh, axis=1, keepdims=True) + NORM_EPS) * scale
        kh = kh * lax.rsqrt(jnp.sum(kh * kh, axis=1, keepdims=True) + NORM_EPS)
        bh = beta_all[:, SMALL_BETA + h:SMALL_BETA + h + 1]
        gcol = gcol_all[:, SMALL_A + h:SMALL_A + h + 1]
        grow = grow_all[SMALL_A + h:SMALL_A + h + 1, :]
        decay = jnp.where(incl, jnp.exp(jnp.where(incl, gcol - grow, 0.0)), 0.0)
        kb = kh * bh
        m = jnp.where(strict, _dot_nt(kb, kh) * decay, 0.0)
        size = min(chunk, GDN_BASE)
        pw = -jnp.where(same_blk(size), m, 0.0)
        t = eye + pw
        for _ in range(int(math.log2(size)) - 1):
            pw = _dot(pw, pw)
            t = t + _dot(t, pw)
        while size < chunk:
            off = jnp.where(jnp.logical_and(same_blk(2 * size), jnp.logical_not(same_blk(size))), m, 0.0)
            t = t - _dot(_dot(t, off), t)
            size *= 2
        u = _dot(t, vh * bh)
        w = _dot(t, kb * jnp.exp(gcol))
        attn = jnp.where(incl, _dot_nt(qh, kh) * decay, 0.0)
        qg = qh * jnp.exp(gcol)
        s_h = s_scr[h]
        vnews, inters = [], []
        for cidx in range(n_chunks):
            rows = slice(cidx * chunk, (cidx + 1) * chunk)
            vnew = u[rows] - _dot(w[rows], s_h)
            inters.append(_dot(qg[rows], s_h))
            glast = gcol[(cidx + 1) * chunk - 1:(cidx + 1) * chunk, :]
            s_h = s_h * jnp.exp(glast) + _dot_tn(kh[rows] * jnp.exp(glast - gcol[rows]), vnew)
            vnews.append(vnew)
        s_scr[h] = s_h
        vnew_all = vnews[0] if n_chunks == 1 else jnp.concatenate(vnews, axis=0)
        inter = inters[0] if n_chunks == 1 else jnp.concatenate(inters, axis=0)
        o = inter + _dot(attn, vnew_all)
        o = o * lax.rsqrt(jnp.mean(o * o, axis=1, keepdims=True) + NORM_EPS) * ng_ref[:, hs]
        o_ref[:, hs] = o * _silu(gate[:, hs])

    @pl.when(step == pl.num_programs(1) - 1)
    def _():
        sout_ref[0] = s_scr[...]


def _gdn(dqkv, small, dgate, tail0, s0, conv_w, alog_row, dtb_row, ng_row, batch, seq, chunk, rb):
    steps = seq // rb
    row = lambda w: pl.BlockSpec((rb, w), lambda b, i: (b * steps + i, 0))
    const = lambda shp: pl.BlockSpec(shp, lambda b, i: (0,) * len(shp))
    return pl.pallas_call(
        functools.partial(_gdn_kernel, chunk=chunk),
        out_shape=[jax.ShapeDtypeStruct((dqkv.shape[0], W_GROUP), F32),
                   jax.ShapeDtypeStruct((batch, N_H, HEAD_DIM, HEAD_DIM), F32)],
        grid=(batch, steps),
        in_specs=[row(3 * W_GROUP), row(LANES), row(W_GROUP),
                  pl.BlockSpec((1, 8, 3 * W_GROUP), lambda b, i: (b, 0, 0)),
                  pl.BlockSpec((1, N_H, HEAD_DIM, HEAD_DIM), lambda b, i: (b, 0, 0, 0)),
                  const((CONV_W, 3 * W_GROUP)), const((1, LANES)), const((1, LANES)), const((1, W_GROUP))],
        out_specs=[row(W_GROUP), pl.BlockSpec((1, N_H, HEAD_DIM, HEAD_DIM), lambda b, i: (b, 0, 0, 0))],
        scratch_shapes=[pltpu.VMEM((N_H, HEAD_DIM, HEAD_DIM), F32), pltpu.VMEM((8, 3 * W_GROUP), F32)],
        compiler_params=_cparams(("parallel", "arbitrary")),
        name="gdn",
    )(dqkv, small, dgate, tail0, s0, conv_w, alog_row, dtb_row, ng_row)


def _lane_row(vec, offset):
    return jnp.zeros((1, LANES), F32).at[0, offset:offset + vec.shape[0]].set(vec)


def _pad_rows(x, bs, s, rows):
    w = x.shape[-1]
    return jnp.concatenate([x.reshape(bs, s, w), jnp.zeros((bs, rows - s, w), x.dtype)], axis=1)


def kernel(x_prompt, x_sample, cache_dsa_k, cache_dsa_v, cache_dsa_kidx, cache_sb_k, cache_sb_v, state_gdn_S,
           state_gdn_conv, page_table, c_prompt, c_sample, w_cond, b_cond, w_in, w_out, ln_g, ln_b, conv_w, a_log,
           dt_bias, gdn_norm_g, gmlp_w_s, gmlp_b_s, w_ff1, w_ff2):
    bp, seq, d = x_prompt.shape
    bs, s, _ = x_sample.shape
    depth = w_in.shape[0]
    page = cache_dsa_k.shape[2]
    alpha = (2 * depth) ** 0.25
    assert seq % BLK == 0 and page == BLK and s >= CONV_W - 1 and s % 8 == 0 and s <= BLK

    c_all = jnp.concatenate([c_prompt, c_sample], axis=0)
    rc = ((c_all.shape[0] + 7) // 8) * 8
    c_all = jnp.concatenate([c_all, jnp.zeros((rc - c_all.shape[0], d), F32)], axis=0)
    mod = _cond(c_all, w_cond, b_cond)

    cache_it = jnp.transpose(cache_dsa_kidx, (0, 1, 3, 2))
    cache_akt, cache_avt, cache_ckt, cache_cvt = (jnp.transpose(t, (0, 1, 3, 4, 2))
                                                  for t in (cache_dsa_k, cache_dsa_v, cache_sb_k, cache_sb_v))

    xp = x_prompt.reshape(bp * seq, d)
    xs = x_sample.reshape(bs * s, d)
    c_gmlp = min(seq, GMLP_CHUNK)
    cs_gmlp = min(s, GMLP_CHUNK)
    st_p, st_s = [], []
    for l in range(depth):
        w_pad = _pad_w_in(w_in[l])
        wout_b, w1_b, w2_b = w_out[l].astype(BF16), w_ff1[l].astype(BF16), w_ff2[l].astype(BF16)
        alog_row, dtb_row = _lane_row(a_log[l], SMALL_A), _lane_row(dt_bias[l], SMALL_A)
        ng_row = jnp.tile(gdn_norm_g[l], N_H).reshape(1, W_GROUP)
        pre_ln = l == 0

        assert bp == 1
        mods = [mod[l, 0:1, i * d:(i + 1) * d] for i in range(6)]
        pr = _proj(xp, mods[0], mods[1], w_pad, pre_ln, 512 if (bp * seq) % 512 == 0 else BLK, True)
        out_a = _dsa_prompt(pr["aq"], pr["iq"], pr["small"], pr["ik_b"], pr["ak_b"], pr["av_t"], bp, seq)
        out_b, _ = _gmlp(pr["bu"], pr["bv"], gmlp_w_s[l][:, :c_gmlp, :c_gmlp], gmlp_b_s[l][:, :c_gmlp].T,
                         c_gmlp, c_gmlp, False)
        out_c = _sb_prompt(pr["cq"], pr["ck_b"], pr["cv_b"], bp, seq)
        gchunk = math.gcd(seq, GDN_CHUNK)
        out_d, s_new = _gdn(pr["dqkv"], pr["small"], pr["dgate"], jnp.zeros((bp, 8, 3 * W_GROUP), F32),
                            jnp.zeros((bp, N_H, HEAD_DIM, HEAD_DIM), F32), conv_w[l], alog_row, dtb_row, ng_row,
                            bp, seq, gchunk, 256 if seq % 256 == 0 else gchunk)
        xp = _outffn(xp, (out_a, out_b, out_c, out_d), mods[2], mods[3], mods[4], mods[5], wout_b, ln_g[l], ln_b[l],
                     w1_b, w2_b, pre_ln, alpha, tm=256 if (bp * seq) % 256 == 0 else BLK)
        buf_new = pr["dqkv"].reshape(bp, seq, 3 * W_GROUP)[:, seq - (CONV_W - 1):]
        st_p.append((pr["ak"].reshape(bp, seq, N_H, HEAD_DIM), pr["av"].reshape(bp, seq, N_H, HEAD_DIM),
                     pr["ik"].reshape(bp, seq, IDX_DIM), pr["ck"].reshape(bp, seq, N_H, HEAD_DIM),
                     pr["cv"].reshape(bp, seq, N_H, HEAD_DIM), s_new, buf_new))

        mods = [jnp.repeat(mod[l, bp:bp + bs, i * d:(i + 1) * d], s, axis=0) for i in range(6)]
        ps = _proj(xs, mods[0], mods[1], w_pad, pre_ln, bs * s, False)
        out_a = _dsa_sample(page_table, ps["aq"], ps["iq"], ps["small"], _pad_rows(ps["ik"], bs, s, BLK),
                            _pad_rows(ps["ak"], bs, s, BLK), _pad_rows(ps["av"], bs, s, BLK),
                            cache_it, cache_akt, cache_avt, l, bs, s)
        out_b, v_rows = _gmlp(ps["bu"], ps["bv"], jnp.tile(gmlp_w_s[l][:, :cs_gmlp, :cs_gmlp], (1, bs, bs)),
                              jnp.tile(gmlp_b_s[l][:, :cs_gmlp].T, (bs, 1)), bs * s, cs_gmlp, True)
        out_c = _sb_sample(page_table, ps["cq"], _pad_rows(ps["ck"], bs, s, BLK), _pad_rows(ps["cv"], bs, s, BLK),
                           cache_ckt, cache_cvt, l, bs, s)
        tail0 = jnp.concatenate([jnp.zeros((bs, 8 - (CONV_W - 1), 3 * W_GROUP), F32), state_gdn_conv[l]], axis=1)
        out_d, s_new = _gdn(ps["dqkv"], ps["small"], ps["dgate"], tail0, state_gdn_S[l], conv_w[l], alog_row,
                            dtb_row, ng_row, bs, s, math.gcd(s, GDN_CHUNK), s)
        xs = _outffn(xs, (out_a, out_b, out_c, out_d), mods[2], mods[3], mods[4], mods[5], wout_b, ln_g[l], ln_b[l],
                     w1_b, w2_b, pre_ln, alpha, tm=bs * s)
        buf_new = ps["dqkv"].reshape(bs, s, 3 * W_GROUP)[:, s - (CONV_W - 1):]
        st_s.append((ps["ak"].reshape(bs, s, N_H, HEAD_DIM), ps["av"].reshape(bs, s, N_H, HEAD_DIM),
                     ps["ik"].reshape(bs, s, IDX_DIM), ps["ck"].reshape(bs, s, N_H, HEAD_DIM),
                     ps["cv"].reshape(bs, s, N_H, HEAD_DIM), s_new, buf_new, v_rows.reshape(bs, s, W_GROUP)))

    outs_p = tuple(jnp.stack(t) for t in zip(*st_p))
    outs_s = tuple(jnp.stack(t) for t in zip(*st_s))
    return (xp.reshape(bp, seq, d), xs.reshape(bs, s, d)) + outs_p + outs_s
```

```python
import functools
import math

import jax
import jax.numpy as jnp
from jax import lax
from jax.experimental import pallas as pl
from jax.experimental.pallas import tpu as pltpu

F32 = jnp.float32
BF16 = jnp.bfloat16
I32 = jnp.int32

LN_EPS = 1e-5
NORM_EPS = 1e-6
N_H = 4
HEAD_DIM = 64
W_GROUP = N_H * HEAD_DIM
IDX_HEADS = 4
IDX_DIM = 64
TOPK_MAX = 256
GMLP_CHUNK = 128
GDN_CHUNK = 64
CONV_W = 4
BLK = 128
LANES = 128
VMEM_LIMIT = 56 * 1024 * 1024
SB_ZERO_TAIL = -110.0
INT_MIN = -2 ** 31
INT_MAX = 2 ** 31 - 1
LOG2E = 1.4426950408889634
MASKED = -1e30
M_INIT = -1e20
UNROLL = 8
PAGE_UNROLL = 4
SB_RING = 3
PAGE_RING = 8
COUNT_UNROLL = 16
GDN_BASE = 8

PIECES = (("dqkv", 768, 768), ("aq", 256, 256), ("ak", 256, 256), ("av", 256, 256), ("iq", 256, 256),
          ("bu", 256, 256), ("bv", 256, 256), ("cq", 256, 256), ("ck", 256, 256), ("cv", 256, 256),
          ("dgate", 256, 256), ("ik", 64, 128), ("small", 12, 128))
N_PAD = sum(p[2] for p in PIECES)
SMALL_IW, SMALL_BETA, SMALL_A = 0, 4, 8
BF16_COPIES = ("ak", "ck", "cv", "ik")


def _cparams(sem, vmem=VMEM_LIMIT):
    return pltpu.CompilerParams(dimension_semantics=sem, vmem_limit_bytes=vmem)


def _ln_plain(x):
    mu = jnp.mean(x, axis=-1, keepdims=True)
    xc = x - mu
    var = jnp.mean(xc * xc, axis=-1, keepdims=True)
    return xc * lax.rsqrt(var + LN_EPS)


def _sigmoid(x):
    return 1.0 / (1.0 + jnp.exp(-x))


def _silu(x):
    return x * _sigmoid(x)


def _softplus(x):
    return jnp.maximum(x, 0.0) + jnp.log1p(jnp.exp(-jnp.abs(x)))


def _gelu_tanh(x):
    c = math.sqrt(2.0 / math.pi)
    return x * (0.5 * (1.0 + jnp.tanh(c * (x + 0.044715 * (x * x * x)))))


def _dot(a, b):
    return jnp.dot(a.astype(BF16), b.astype(BF16), preferred_element_type=F32)


def _dot_nt(a, b):
    return lax.dot_general(a.astype(BF16), b.astype(BF16), (((1,), (1,)), ((), ())),
                           preferred_element_type=F32)


def _dot_tn(a, b):
    return lax.dot_general(a.astype(BF16), b.astype(BF16), (((0,), (0,)), ((), ())),
                           preferred_element_type=F32)


def _split(x):
    hi = x.astype(BF16)
    lo = (x - hi.astype(F32)).astype(BF16)
    return hi, lo


def _dot_precise_lhs(a, b01):
    hi, lo = _split(a)
    return jnp.dot(hi, b01, preferred_element_type=F32) + jnp.dot(lo, b01, preferred_element_type=F32)


def _dot_precise_rhs(a01, b):
    hi, lo = _split(b)
    return jnp.dot(a01, hi, preferred_element_type=F32) + jnp.dot(a01, lo, preferred_element_type=F32)


def _dot3(a, b):
    ah, al = _split(a)
    bh, bl = _split(b)
    return (jnp.dot(ah, bh, preferred_element_type=F32) + jnp.dot(al, bh, preferred_element_type=F32)
            + jnp.dot(ah, bl, preferred_element_type=F32))


def _iota(shape, dim):
    return lax.broadcasted_iota(I32, shape, dim)


def _sort_key(score):
    bits = pltpu.bitcast(score, I32)
    return bits ^ ((bits >> 31) & jnp.int32(0x7FFFFFFF))


def _alibi_slope(h):
    return 2.0 ** (-8.0 * (h + 1) / N_H)


def _heads(x):
    return [x[:, h * HEAD_DIM:(h + 1) * HEAD_DIM] for h in range(N_H)]


def _cond_kernel(c_ref, w_ref, b_ref, o_ref):
    s = _silu(c_ref[...])
    o_ref[0] = _dot3(s, w_ref[0]) + b_ref[0]


def _cond(c_all, w_cond, b_cond):
    depth, d, n6 = w_cond.shape
    rc = c_all.shape[0]
    tn = 1536 if n6 % 1536 == 0 else n6
    return pl.pallas_call(
        _cond_kernel,
        out_shape=jax.ShapeDtypeStruct((depth, rc, n6), F32),
        grid=(depth, n6 // tn),
        in_specs=[pl.BlockSpec((rc, d), lambda l, j: (0, 0)),
                  pl.BlockSpec((1, d, tn), lambda l, j: (l, 0, j)),
                  pl.BlockSpec((1, 1, tn), lambda l, j: (l, 0, j))],
        out_specs=pl.BlockSpec((1, rc, tn), lambda l, j: (l, 0, j)),
        compiler_params=_cparams(("arbitrary", "arbitrary")),
        name="cond",
    )(c_all, w_cond, b_cond.reshape(depth, 1, n6))


def _proj_kernel(x_ref, sh_ref, sc_ref, w_ref, *o_refs, pre_ln, extras):
    x = x_ref[...]
    if pre_ln:
        x = _ln_plain(x)
    hb = (x * (1.0 + sc_ref[...]) + sh_ref[...]).astype(BF16)
    col = 0
    for (name, width, padded), o_ref in zip(PIECES, o_refs[:len(PIECES)]):
        res = jnp.dot(hb, w_ref[:, col:col + padded], preferred_element_type=F32)
        o_ref[...] = res if o_ref.shape[-1] == padded else res[:, :o_ref.shape[-1]]
        if extras and name in BF16_COPIES:
            o_refs[len(PIECES) + BF16_COPIES.index(name)][...] = res.astype(BF16)
        if extras and name == "av":
            vt_ref = o_refs[len(PIECES) + len(BF16_COPIES)]
            for j in range(vt_ref.shape[0]):
                vt_ref[j] = res[j * BLK:(j + 1) * BLK, :].T.astype(BF16)
        col += padded


def _mod_spec(m, tm, d):
    if m.shape[0] == 1:
        return pl.BlockSpec((1, d), lambda i: (0, 0))
    return pl.BlockSpec((tm, d), lambda i: (i, 0))


def _proj(x, shift, scale, w_pad, pre_ln, tm, extras):
    r, d = x.shape
    out_shapes, out_specs = [], []
    for name, width, padded in PIECES:
        w_out = width if name == "ik" else padded
        out_shapes.append(jax.ShapeDtypeStruct((r, w_out), F32))
        out_specs.append(pl.BlockSpec((tm, w_out), lambda i: (i, 0)))
    if extras:
        for name in BF16_COPIES:
            w_out = LANES if name == "ik" else W_GROUP
            out_shapes.append(jax.ShapeDtypeStruct((r, w_out), BF16))
            out_specs.append(pl.BlockSpec((tm, w_out), lambda i: (i, 0)))
        out_shapes.append(jax.ShapeDtypeStruct((r // BLK, W_GROUP, BLK), BF16))
        out_specs.append(pl.BlockSpec((tm // BLK, W_GROUP, BLK), lambda i: (i, 0, 0)))
    outs = pl.pallas_call(
        functools.partial(_proj_kernel, pre_ln=pre_ln, extras=extras),
        out_shape=out_shapes,
        grid=(r // tm,),
        in_specs=[pl.BlockSpec((tm, d), lambda i: (i, 0)), _mod_spec(shift, tm, d), _mod_spec(scale, tm, d),
                  pl.BlockSpec((d, N_PAD), lambda i: (0, 0))],
        out_specs=out_specs,
        compiler_params=_cparams(("parallel",)),
        name="proj",
    )(x, shift, scale, w_pad)
    res = {name: o for (name, _, _), o in zip(PIECES, outs)}
    if extras:
        for i, name in enumerate(BF16_COPIES):
            res[name + "_b"] = outs[len(PIECES) + i]
        res["av_t"] = outs[len(PIECES) + len(BF16_COPIES)]
    return res


def _pad_w_in(w_in_l):
    off = {}
    start = 0
    for name, n in (("aq", 256), ("ak", 256), ("av", 256), ("iq", 256), ("ik", 64), ("iw", 4), ("bu", 256),
                    ("bv", 256), ("cq", 256), ("ck", 256), ("cv", 256), ("dqkv", 768), ("dbeta", 4), ("da", 4),
                    ("dgate", 256)):
        off[name] = (start, n)
        start += n
    d = w_in_l.shape[0]
    cols = []
    for name, width, padded in PIECES:
        if name == "small":
            parts = [w_in_l[:, off[k][0]:off[k][0] + off[k][1]] for k in ("iw", "dbeta", "da")]
            piece = jnp.concatenate(parts, axis=1)
        else:
            piece = w_in_l[:, off[name][0]:off[name][0] + off[name][1]]
        if padded > piece.shape[1]:
            piece = jnp.concatenate([piece, jnp.zeros((d, padded - piece.shape[1]), piece.dtype)], axis=1)
        cols.append(piece)
    return jnp.concatenate(cols, axis=1).astype(BF16)


def _outffn_kernel(x_ref, oa_ref, ob_ref, oc_ref, od_ref, gm_ref, shf_ref, scf_ref, gf_ref,
                   wout_ref, lng_ref, lnb_ref, w1_ref, w2_ref, o_ref, *, pre_ln, alpha):
    x = x_ref[...]
    if pre_ln:
        x = _ln_plain(x)
    mo = None
    for g, r in enumerate((oa_ref, ob_ref, oc_ref, od_ref)):
        part = _dot(r[...], wout_ref[g * W_GROUP:(g + 1) * W_GROUP, :])
        mo = part if mo is None else mo + part
    x1 = _ln_plain(alpha * x + gm_ref[...] * mo) * lng_ref[0:1, :] + lnb_ref[0:1, :]
    hf = x1 * (1.0 + scf_ref[...]) + shf_ref[...]
    a = jnp.maximum(_dot(hf, w1_ref[...]), 0.0)
    ff = _dot(a * a, w2_ref[...])
    o_ref[...] = _ln_plain(alpha * x1 + gf_ref[...] * ff) * lng_ref[1:2, :] + lnb_ref[1:2, :]


def _outffn(x, branches, gm, shf, scf, gf, wout_b, ln_g, ln_b, w1_b, w2_b, pre_ln, alpha, tm):
    r, d = x.shape
    dff = w1_b.shape[1]
    row = lambda w: pl.BlockSpec((tm, w), lambda i: (i, 0))
    const = lambda shp: pl.BlockSpec(shp, lambda i: (0, 0), pipeline_mode=pl.Buffered(1))
    return pl.pallas_call(
        functools.partial(_outffn_kernel, pre_ln=pre_ln, alpha=alpha),
        out_shape=jax.ShapeDtypeStruct((r, d), F32),
        grid=(r // tm,),
        in_specs=[row(d)] + [row(W_GROUP)] * 4 + [_mod_spec(m, tm, d) for m in (gm, shf, scf, gf)]
        + [const((d, d)), const((2, d)), const((2, d)), const((d, dff)), const((dff, d))],
        out_specs=row(d),
        compiler_params=_cparams(("parallel",)),
        name="outffn",
    )(x, *branches, gm, shf, scf, gf, wout_b, ln_g, ln_b, w1_b, w2_b)


def _gmlp_kernel(u_ref, v_ref, w_ref, bt_ref, ob_ref, *vn_refs, chunk):
    u = _gelu_tanh(u_ref[...])
    vn = _ln_plain(_gelu_tanh(v_ref[...]))
    if vn_refs:
        vn_refs[0][...] = vn
    rows = u.shape[0]
    ri, ci = _iota((rows, rows), 0), _iota((rows, rows), 1)
    mask = ci <= ri
    if rows != chunk:
        sh = int(math.log2(chunk))
        mask = mask & ((ri >> sh) == (ci >> sh))
    vb = vn.astype(BF16)
    for g in range(N_H):
        w = jnp.where(mask, w_ref[g], 0.0).astype(BF16)
        mixed = jnp.dot(w, vb[:, g * HEAD_DIM:(g + 1) * HEAD_DIM], preferred_element_type=F32)
        mixed = mixed + bt_ref[:, g:g + 1]
        ob_ref[:, g * HEAD_DIM:(g + 1) * HEAD_DIM] = u[:, g * HEAD_DIM:(g + 1) * HEAD_DIM] * mixed


def _gmlp(bu, bv, w_tiled, bt_tiled, rows, chunk, want_vn):
    r = bu.shape[0]
    row = pl.BlockSpec((rows, W_GROUP), lambda i: (i, 0))
    out_shape = [jax.ShapeDtypeStruct((r, W_GROUP), F32)]
    out_specs = [row]
    if want_vn:
        out_shape.append(jax.ShapeDtypeStruct((r, W_GROUP), F32))
        out_specs.append(row)
    outs = pl.pallas_call(
        functools.partial(_gmlp_kernel, chunk=chunk),
        out_shape=out_shape,
        grid=(r // rows,),
        in_specs=[row, row, pl.BlockSpec((N_H, rows, rows), lambda i: (0, 0, 0)),
                  pl.BlockSpec((rows, N_H), lambda i: (0, 0))],
        out_specs=out_specs,
        compiler_params=_cparams(("parallel",)),
        name="gmlp",
    )(bu, bv, w_tiled, bt_tiled)
    return outs if want_vn else (outs[0], None)


def _sb_block(qk, pv, vis, carry, acc, upper):
    new_carry, new_acc = [], []
    for h in range(N_H):
        z = qk(h)
        l_raw = -_softplus(z)
        l_vis = l_raw if vis is None else jnp.where(vis, l_raw, 0.0)
        tail = _dot_precise_lhs(l_vis, upper) + carry[h]
        w = jnp.exp(z + l_raw + tail)
        if vis is not None:
            w = jnp.where(vis, w, 0.0)
        new_acc.append(acc[h] + pv(h, w))
        new_carry.append(carry[h] + jnp.sum(l_vis, axis=1, keepdims=True))
    return new_carry, new_acc


def _sb_live(carry):
    m = carry[0]
    for c in carry[1:]:
        m = jnp.maximum(m, c)
    return (jnp.max(m) > SB_ZERO_TAIL).astype(I32)


def _upper_ones():
    return (_iota((BLK, BLK), 0) > _iota((BLK, BLK), 1)).astype(BF16)


def _natural_kv(q_heads, k, v):
    kh, vh = _heads(k), _heads(v)
    return (lambda h: _dot_nt(q_heads[h], kh[h])), (lambda h, w: _dot(w, vh[h]))


def _sbp_kernel(q_ref, k_ref, v_ref, o_ref):
    qb = pl.program_id(1)
    q_heads = _heads((q_ref[...] * (HEAD_DIM ** -0.5)).astype(BF16))
    upper = _upper_ones()
    ri, ci = _iota((BLK, BLK), 0), _iota((BLK, BLK), 1)
    zero_c = [jnp.zeros((BLK, 1), F32)] * N_H
    zero_a = [jnp.zeros((BLK, HEAD_DIM), F32)] * N_H
    off = pl.multiple_of(qb * BLK, BLK)
    qk, pv = _natural_kv(q_heads, k_ref[pl.ds(off, BLK), :], v_ref[pl.ds(off, BLK), :])
    carry, acc = _sb_block(qk, pv, ci < ri, zero_c, zero_a, upper)

    def cond(st):
        return jnp.logical_and(st[0] >= 0, st[1] > 0)

    def body(st):
        kb = st[0]
        o = pl.multiple_of(kb * BLK, BLK)
        qk, pv = _natural_kv(q_heads, k_ref[pl.ds(o, BLK), :], v_ref[pl.ds(o, BLK), :])
        c, a = _sb_block(qk, pv, None, list(st[2:2 + N_H]), list(st[2 + N_H:]), upper)
        return (kb - 1, _sb_live(c), *c, *a)

    st = lax.while_loop(cond, body, (qb - 1, _sb_live(carry), *carry, *acc))
    for h in range(N_H):
        o_ref[:, h * HEAD_DIM:(h + 1) * HEAD_DIM] = st[2 + N_H + h]


def _sb_prompt(cq, ck_b, cv_b, batch, seq):
    nq = seq // BLK
    full = pl.BlockSpec((seq, W_GROUP), lambda b, i: (b, 0), pipeline_mode=pl.Buffered(1))
    return pl.pallas_call(
        _sbp_kernel,
        out_shape=jax.ShapeDtypeStruct(cq.shape, F32),
        grid=(batch, nq),
        in_specs=[pl.BlockSpec((BLK, W_GROUP), lambda b, i: (b * nq + i, 0)), full, full],
        out_specs=pl.BlockSpec((BLK, W_GROUP), lambda b, i: (b * nq + i, 0)),
        compiler_params=_cparams(("parallel", "arbitrary")),
        name="sb_prompt",
    )(cq, ck_b, cv_b)


def _sbs_kernel(pt_ref, q_ref, kn_ref, vn_ref, kc_ref, vc_ref, o_ref, kbuf, vbuf, sem, *, layer, n_pages):
    b = pl.program_id(0)
    rows = q_ref.shape[0]
    q_heads = _heads((q_ref[...] * (HEAD_DIM ** -0.5)).astype(BF16))
    upper = _upper_ones()
    ri, ci = _iota((rows, BLK), 0), _iota((rows, BLK), 1)
    zero_c = [jnp.zeros((rows, 1), F32)] * N_H
    zero_a = [jnp.zeros((rows, HEAD_DIM), F32)] * N_H
    def copies(p):
        page, slot = pt_ref[b, p], p % SB_RING
        return (pltpu.make_async_copy(kc_ref.at[layer, page], kbuf.at[slot], sem.at[0, slot]),
                pltpu.make_async_copy(vc_ref.at[layer, page], vbuf.at[slot], sem.at[1, slot]))

    def start(p):
        for c in copies(p):
            c.start()

    def wait(p):
        for c in copies(p):
            c.wait()

    for j in range(1, min(SB_RING, n_pages + 1)):
        start(n_pages - j)
    qk, pv = _natural_kv(q_heads, kn_ref[0], vn_ref[0])
    carry, acc = _sb_block(qk, pv, ci < ri, zero_c, zero_a, upper)

    def cond(st):
        return jnp.logical_and(st[0] >= 0, st[1] > 0)

    def body(st):
        p = st[0]
        slot = p % SB_RING
        wait(p)

        @pl.when(p - (SB_RING - 1) >= 0)
        def _():
            start(p - (SB_RING - 1))

        c, a = _sb_block(lambda h: _dot(q_heads[h], kbuf[slot, h]), lambda h, w: _dot_nt(w, vbuf[slot, h]), None,
                         list(st[2:2 + N_H]), list(st[2 + N_H:]), upper)
        return (p - 1, _sb_live(c), *c, *a)

    st = lax.while_loop(cond, body, (jnp.int32(n_pages - 1), _sb_live(carry), *carry, *acc))
    for j in range(SB_RING - 1):
        @pl.when(st[0] - j >= 0)
        def _():
            wait(st[0] - j)

    for h in range(N_H):
        o_ref[:, h * HEAD_DIM:(h + 1) * HEAD_DIM] = st[2 + N_H + h]


def _sb_sample(page_table, cq, ck_new_pad, cv_new_pad, cache_kt, cache_vt, layer, bs, s):
    n_pages = page_table.shape[1]
    page = cache_kt.shape[-1]
    return pl.pallas_call(
        functools.partial(_sbs_kernel, layer=layer, n_pages=n_pages),
        out_shape=jax.ShapeDtypeStruct(cq.shape, F32),
        grid_spec=pltpu.PrefetchScalarGridSpec(
            num_scalar_prefetch=1,
            grid=(bs,),
            in_specs=[pl.BlockSpec((s, W_GROUP), lambda b, pt: (b, 0)),
                      pl.BlockSpec((1, BLK, W_GROUP), lambda b, pt: (b, 0, 0)),
                      pl.BlockSpec((1, BLK, W_GROUP), lambda b, pt: (b, 0, 0)),
                      pl.BlockSpec(memory_space=pl.ANY), pl.BlockSpec(memory_space=pl.ANY)],
            out_specs=pl.BlockSpec((s, W_GROUP), lambda b, pt: (b, 0)),
            scratch_shapes=[pltpu.VMEM((SB_RING, N_H, HEAD_DIM, page), F32),
                            pltpu.VMEM((SB_RING, N_H, HEAD_DIM, page), F32),
                            pltpu.SemaphoreType.DMA((2, SB_RING))]),
        compiler_params=_cparams(("arbitrary",)),
        name="sb_sample",
    )(page_table, cq, ck_new_pad, cv_new_pad, cache_kt, cache_vt)


def _kth_largest_key(count_ge, shape, k):
    kf = jnp.float32(k)
    t = jnp.where(count_ge(jnp.zeros(shape, I32)) >= kf, jnp.int32(0), jnp.int32(INT_MIN))

    def body(i, t):
        cand = t + (jnp.int32(1) << (30 - i))
        return jnp.where(count_ge(cand) >= kf, cand, t)

    return lax.fori_loop(0, 31, body, t)


def _dsap_kernel(aq_ref, iq_ref, sm_ref, ik_ref, k_ref, vt_ref, o_ref, key_ref, s_scr, *, topk):
    qb = pl.program_id(1)
    nkb = qb + 1
    ri, ci = _iota((BLK, BLK), 0), _iota((BLK, BLK), 1)
    diag_ok = ri <= ci

    iq_t = (iq_ref[...] * (IDX_DIM ** -0.5)).T
    iq_all = jnp.concatenate([iq_t[h * IDX_DIM:(h + 1) * IDX_DIM] for h in range(IDX_HEADS)],
                             axis=1).astype(BF16)
    sm_t = sm_ref[...].T
    iw = [sm_t[SMALL_IW + h:SMALL_IW + h + 1, :] * (IDX_HEADS ** -0.5) for h in range(IDX_HEADS)]

    def score_block(kb, diagonal):
        o = pl.multiple_of(kb * BLK, BLK)
        kidx = ik_ref[pl.ds(o, BLK), :][:, :IDX_DIM]
        s_all = jnp.dot(kidx, iq_all, preferred_element_type=F32)
        score = iw[0] * jnp.maximum(s_all[:, 0:BLK], 0.0)
        for h in range(1, IDX_HEADS):
            score = score + iw[h] * jnp.maximum(s_all[:, h * BLK:(h + 1) * BLK], 0.0)
        if diagonal:
            score = jnp.where(diag_ok, score, -jnp.inf)
        key = _sort_key(score)
        key_ref[kb] = key

    def score_group(g, _):
        for u in range(UNROLL):
            score_block(jnp.minimum(g * UNROLL + u, qb), False)
        return 0

    trips = (qb + UNROLL - 1) // UNROLL
    lax.fori_loop(0, trips, score_group, 0)
    score_block(qb, True)

    def count(pred):
        def group(g, cnt):
            for u in range(COUNT_UNROLL):
                idx = g * COUNT_UNROLL + u
                weight = (idx <= qb).astype(I32)
                cnt = cnt + jnp.where(pred(key_ref[jnp.minimum(idx, qb)]), weight, 0)
            return cnt

        cnt = lax.fori_loop(0, (nkb + COUNT_UNROLL - 1) // COUNT_UNROLL, group, jnp.zeros((BLK, BLK), I32))
        return jnp.sum(cnt.astype(F32), axis=0, keepdims=True)

    thr = _kth_largest_key(lambda cand: count(lambda key: key >= cand), (1, BLK), topk)
    n_gt = count(lambda key: key > thr)
    n_eq = count(lambda key: key == thr)
    room = jnp.float32(topk) - n_gt
    tie_break = jnp.max(n_eq - room) > 0.0

    aq_t = (aq_ref[...] * (HEAD_DIM ** -0.5 * LOG2E)).T
    bias = [(_alibi_slope(h) * LOG2E) * ri.astype(F32) for h in range(N_H)]
    zeros = jnp.zeros((HEAD_DIM, BLK), F32)
    q_bd = jnp.concatenate(
        [jnp.concatenate([aq_t[g * HEAD_DIM:(g + 1) * HEAD_DIM] if g == h else zeros for g in range(N_H)], axis=0)
         for h in range(N_H)], axis=1).astype(BF16)
    strict_lower = (ci < ri).astype(BF16)

    half = W_GROUP // 2

    def logits_to(slot, kb):
        kblk = k_ref[pl.ds(pl.multiple_of(kb * BLK, BLK), BLK), :]
        for g in range(2):
            s_scr[slot, :, g * 2 * BLK:(g + 1) * 2 * BLK] = jnp.dot(
                kblk[:, g * half:(g + 1) * half], q_bd[g * half:(g + 1) * half, g * 2 * BLK:(g + 1) * 2 * BLK],
                preferred_element_type=F32)

    def attend(ties, idx, slot, diagonal, st):
        kb = jnp.minimum(idx, qb)
        if diagonal:
            thr_b = thr
        else:
            logits_to(1 - slot, jnp.minimum(idx + 1, qb))
            thr_b = jnp.where(idx < qb, thr, jnp.int32(INT_MAX))
        seen = st[0]
        key = key_ref[kb]
        if ties:
            eq = key == thr_b
            rank = seen + jnp.dot(strict_lower, eq.astype(BF16), preferred_element_type=F32)
            sel = jnp.logical_or(key > thr_b, jnp.logical_and(eq, rank < room))
            seen = seen + jnp.sum(eq.astype(F32), axis=0, keepdims=True)
        else:
            sel = key >= thr_b
        if diagonal:
            sel = jnp.logical_and(sel, diag_ok)
        blk_off = ((kb - qb) * BLK).astype(F32)
        vt = vt_ref[kb]
        out = [seen]
        for h in range(N_H):
            m_old, l_old, a_old = st[1 + 3 * h], st[2 + 3 * h], st[3 + 3 * h]
            c_blk = (_alibi_slope(h) * LOG2E) * blk_off
            s = jnp.where(sel, s_scr[slot, :, h * BLK:(h + 1) * BLK] + bias[h], MASKED)
            m_new = jnp.maximum(m_old, jnp.max(s, axis=0, keepdims=True) + c_blk)
            p = jnp.exp2(s - (m_new - c_blk))
            alpha = jnp.exp2(m_old - m_new)
            pv = jnp.dot(vt[h * HEAD_DIM:(h + 1) * HEAD_DIM, :], p.astype(BF16), preferred_element_type=F32)
            out += [m_new, alpha * l_old + jnp.sum(p, axis=0, keepdims=True), alpha * a_old + pv]
        return tuple(out)

    init = [jnp.zeros((1, BLK), F32)]
    for h in range(N_H):
        init += [jnp.full((1, BLK), M_INIT, F32), jnp.zeros((1, BLK), F32), jnp.zeros((HEAD_DIM, BLK), F32)]
    def sweep(ties):
        def group(g, st):
            for u in range(UNROLL):
                st = attend(ties, g * UNROLL + u, u % 2, False, st)
            return st

        logits_to(0, 0)
        st = lax.fori_loop(0, trips, group, tuple(init))
        return attend(ties, qb, 0, True, st)

    st = lax.cond(tie_break, lambda: sweep(True), lambda: sweep(False))
    out_t = jnp.concatenate([st[3 + 3 * h] / st[2 + 3 * h] for h in range(N_H)], axis=0)
    o_ref[...] = out_t.T


def _dsa_prompt(aq, iq, small, ik_b, ak_b, av_t, batch, seq):
    nq = seq // BLK
    topk = min(TOPK_MAX, seq // 4)
    blk = lambda w: pl.BlockSpec((BLK, w), lambda b, i: (b * nq + i, 0))
    full = lambda w: pl.BlockSpec((seq, w), lambda b, i: (b, 0), pipeline_mode=pl.Buffered(1))
    return pl.pallas_call(
        functools.partial(_dsap_kernel, topk=topk),
        out_shape=jax.ShapeDtypeStruct(aq.shape, F32),
        grid=(batch, nq),
        in_specs=[blk(W_GROUP), blk(W_GROUP), blk(LANES), full(LANES), full(W_GROUP),
                  pl.BlockSpec((nq, W_GROUP, BLK), lambda b, i: (b, 0, 0), pipeline_mode=pl.Buffered(1))],
        out_specs=blk(W_GROUP),
        scratch_shapes=[pltpu.VMEM((nq, BLK, BLK), I32), pltpu.VMEM((2, BLK, N_H * BLK), F32)],
        compiler_params=_cparams(("parallel", "arbitrary")),
        name="dsa_prompt",
    )(aq, iq, small, ik_b, ak_b, av_t)


def _dsas_kernel(pt_ref, aq_ref, iq_ref, sm_ref, ikn_ref, akn_ref, avn_ref, ic_ref, kc_ref, vc_ref, o_ref,
                 key_ref, s_ref, ibuf, kbuf, vbuf, sem, *, layer, n_pages, topk):
    b = pl.program_id(0)
    s_rows = aq_ref.shape[0]
    past = n_pages * BLK
    ri, ci = _iota((s_rows, BLK), 0), _iota((s_rows, BLK), 1)
    new_valid = jnp.logical_and(ci <= ri, ci < s_rows)

    iq = iq_ref[...] * (IDX_DIM ** -0.5)
    iq_all = jnp.concatenate(_heads(iq), axis=0).astype(BF16)
    iw = sm_ref[...][:, SMALL_IW:SMALL_IW + IDX_HEADS] * (IDX_HEADS ** -0.5)

    def idx_copy(p, slot):
        return pltpu.make_async_copy(ic_ref.at[layer, pt_ref[b, p]], ibuf.at[slot], sem.at[0, slot])

    def k_copy(p, slot):
        return pltpu.make_async_copy(kc_ref.at[layer, pt_ref[b, p]], kbuf.at[slot], sem.at[1, slot])

    def v_copy(p, slot):
        return pltpu.make_async_copy(vc_ref.at[layer, pt_ref[b, p]], vbuf.at[slot], sem.at[2, slot])

    def score_keys(s_all, valid):
        score = iw[:, 0:1] * jnp.maximum(s_all[0:s_rows], 0.0)
        for h in range(1, IDX_HEADS):
            score = score + iw[:, h:h + 1] * jnp.maximum(s_all[h * s_rows:(h + 1) * s_rows], 0.0)
        if valid is not None:
            score = jnp.where(valid, score, -jnp.inf)
        return _sort_key(score)

    u_n = PAGE_UNROLL
    trips = n_pages // u_n

    def prefetch(copy):
        for g in range(min(PAGE_RING - 1, trips)):
            for u in range(u_n):
                copy(g * u_n + u, g * u_n + u).start()

    def stream(copy, body, carry):
        def trip(g, carry):
            ahead = g + PAGE_RING - 1

            @pl.when(ahead < trips)
            def _():
                for u in range(u_n):
                    copy(ahead * u_n + u, (ahead % PAGE_RING) * u_n + u).start()

            base = (g % PAGE_RING) * u_n
            for u in range(u_n):
                copy(g * u_n + u, base + u).wait()
            for u in range(u_n):
                carry = body(g * u_n + u, base + u, carry)
            return carry

        return lax.fori_loop(0, trips, trip, carry)

    prefetch(idx_copy)
    prefetch(k_copy)
    prefetch(v_copy)

    def p1(p, slot, _):
        key_ref[p] = score_keys(_dot(iq_all, ibuf[slot]), None)
        return 0

    stream(idx_copy, p1, 0)
    key_ref[n_pages] = score_keys(_dot_nt(iq_all, ikn_ref[0]), new_valid)

    def count(pred):
        per_trip = math.gcd(n_pages, 16)

        def body(g, cnt):
            for u in range(per_trip):
                cnt = cnt + jnp.where(pred(key_ref[g * per_trip + u]), 1, 0)
            return cnt
        cnt = lax.fori_loop(0, n_pages // per_trip, body, jnp.zeros((s_rows, BLK), I32))
        cnt = cnt + jnp.where(pred(key_ref[n_pages]), 1, 0)
        return jnp.sum(cnt.astype(F32), axis=1, keepdims=True)

    thr = _kth_largest_key(lambda cand: count(lambda key: key >= cand), (s_rows, 1), topk)
    n_gt = count(lambda key: key > thr)
    n_eq = count(lambda key: key == thr)
    room = jnp.float32(topk) - n_gt
    tie_break = jnp.max(n_eq - room) > 0.0

    q_heads = _heads((aq_ref[...] * (HEAD_DIM ** -0.5)).astype(BF16))
    lower = (_iota((BLK, BLK), 0) < _iota((BLK, BLK), 1)).astype(BF16)

    def logits_block(ties, p, qk, valid, kpos0, st):
        seen = st[0]
        key = key_ref[p]
        if ties:
            eq = key == thr
            rank = seen + jnp.dot(eq.astype(BF16), lower, preferred_element_type=F32)
            sel = jnp.logical_or(key > thr, jnp.logical_and(eq, rank < room))
            seen = seen + jnp.sum(eq.astype(F32), axis=1, keepdims=True)
        else:
            sel = key >= thr
        if valid is not None:
            sel = jnp.logical_and(sel, valid)
        dist = ((past - kpos0) + ri - ci).astype(F32)
        out = [seen]
        for h in range(N_H):
            s = jnp.where(sel, qk(h) - _alibi_slope(h) * dist, MASKED)
            s_ref[p, h] = s
            out.append(jnp.maximum(st[1 + h], s))
        return tuple(out)

    qk_new, pv_new = _natural_kv(q_heads, akn_ref[0], avn_ref[0])

    def pass_a(ties):
        st = stream(k_copy,
                    lambda p, slot, st: logits_block(ties, p, lambda h: _dot(q_heads[h], kbuf[slot, h]), None,
                                                     p * BLK, st),
                    (jnp.zeros((s_rows, 1), F32),) + (jnp.full((s_rows, BLK), MASKED, F32),) * N_H)
        return logits_block(ties, n_pages, qk_new, new_valid, past, st)[1:]

    run_max = lax.cond(tie_break, lambda: pass_a(True), lambda: pass_a(False))
    m_fin = [jnp.max(run_max[h], axis=1, keepdims=True) for h in range(N_H)]

    def values_block(p, pv, st):
        out_l, out_a = [], []
        for h in range(N_H):
            pr = jnp.exp(s_ref[p, h] - m_fin[h])
            out_l.append(st[h] + pr)
            out_a.append(st[N_H + h] + pv(h, pr))
        return tuple(out_l + out_a)

    st = stream(v_copy,
                lambda p, slot, st: values_block(p, lambda h, pr: _dot_nt(pr, vbuf[slot, h]), st),
                (jnp.zeros((s_rows, BLK), F32),) * N_H + (jnp.zeros((s_rows, HEAD_DIM), F32),) * N_H)
    st = values_block(n_pages, pv_new, st)
    for h in range(N_H):
        o_ref[:, h * HEAD_DIM:(h + 1) * HEAD_DIM] = st[N_H + h] / jnp.sum(st[h], axis=1, keepdims=True)


def _dsa_sample(page_table, aq, iq, small, ik_new_pad, ak_new_pad, av_new_pad, cache_it, cache_kt, cache_vt,
                layer, bs, s):
    n_pages = page_table.shape[1]
    page = cache_kt.shape[-1]
    topk = min(TOPK_MAX, (n_pages * page + s) // 4)
    row = lambda w: pl.BlockSpec((s, w), lambda b, pt: (b, 0))
    new = lambda w: pl.BlockSpec((1, BLK, w), lambda b, pt: (b, 0, 0))
    hbm = pl.BlockSpec(memory_space=pl.ANY)
    return pl.pallas_call(
        functools.partial(_dsas_kernel, layer=layer, n_pages=n_pages, topk=topk),
        out_shape=jax.ShapeDtypeStruct(aq.shape, F32),
        grid_spec=pltpu.PrefetchScalarGridSpec(
            num_scalar_prefetch=1,
            grid=(bs,),
            in_specs=[row(W_GROUP), row(W_GROUP), row(LANES), new(IDX_DIM), new(W_GROUP), new(W_GROUP),
                      hbm, hbm, hbm],
            out_specs=row(W_GROUP),
            scratch_shapes=[pltpu.VMEM((n_pages + 1, s, BLK), I32), pltpu.VMEM((n_pages + 1, N_H, s, BLK), F32),
                            pltpu.VMEM((PAGE_RING * PAGE_UNROLL, IDX_DIM, page), F32),
                            pltpu.VMEM((PAGE_RING * PAGE_UNROLL, N_H, HEAD_DIM, page), F32),
                            pltpu.VMEM((PAGE_RING * PAGE_UNROLL, N_H, HEAD_DIM, page), F32),
                            pltpu.SemaphoreType.DMA((3, PAGE_RING * PAGE_UNROLL))]),
        compiler_params=_cparams(("arbitrary",)),
        name="dsa_sample",
    )(page_table, aq, iq, small, ik_new_pad, ak_new_pad, av_new_pad, cache_it, cache_kt, cache_vt)


def _gdn_kernel(x_ref, sm_ref, gate_ref, tail0_ref, s0_ref, cw_ref, alog_ref, dtb_ref, ng_ref,
                o_ref, sout_ref, s_scr, tail_scr, *, chunk):
    step = pl.program_id(1)
    rb = x_ref.shape[0]
    n_chunks = rb // chunk

    @pl.when(step == 0)
    def _():
        s_scr[...] = s0_ref[0]
        tail_scr[...] = tail0_ref[0]

    x = x_ref[...]
    xfull = jnp.concatenate([tail_scr[...], x], axis=0)
    y = xfull[8:8 + rb] * cw_ref[CONV_W - 1:CONV_W, :]
    for j in range(CONV_W - 1):
        y = y + pltpu.roll(xfull, CONV_W - 1 - j, 0)[8:8 + rb] * cw_ref[j:j + 1, :]
    tail_scr[...] = xfull[rb:rb + 8]
    c = _silu(y)

    sm = sm_ref[...]
    beta_all = _sigmoid(sm)
    g_all = -jnp.exp(alog_ref[...]) * _softplus(sm + dtb_ref[...])
    rbp = ((rb + LANES - 1) // LANES) * LANES
    g_pad = g_all if rbp == rb else jnp.concatenate([g_all, jnp.zeros((rbp - rb, LANES), F32)], axis=0)
    g_t = g_pad.T[:, :rb]

    ri, ci = _iota((rb, rb), 0), _iota((rb, rb), 1)
    incl, strict, upper = ci <= ri, ci < ri, ri <= ci
    if n_chunks > 1:
        sh = int(math.log2(chunk))
        same = (ri >> sh) == (ci >> sh)
        incl, strict, upper = incl & same, strict & same, upper & same
    eye = (ri == ci).astype(F32)

    def same_blk(size):
        sh_b = int(math.log2(size))
        return (ri >> sh_b) == (ci >> sh_b)

    gcol_all = _dot_precise_rhs(incl.astype(BF16), g_all)
    grow_all = _dot_precise_lhs(g_t, upper.astype(BF16))
    gate = gate_ref[...]
    n_sq = int(math.log2(chunk))
    scale = HEAD_DIM ** -0.5

    for h in range(N_H):
        hs = slice(h * HEAD_DIM, (h + 1) * HEAD_DIM)
        qh = c[:, h * HEAD_DIM:(h + 1) * HEAD_DIM]
        kh = c[:, W_GROUP + h * HEAD_DIM:W_GROUP + (h + 1) * HEAD_DIM]
        vh = c[:, 2 * W_GROUP + h * HEAD_DIM:2 * W_GROUP + (h + 1) * HEAD_DIM]
        qh = qh * lax.rsqrt(jnp.sum(qh * qh, axis=1, keepdims=True) + NORM_EPS) * scale
        kh = kh * lax.rsqrt(jnp.sum(kh * kh, axis=1, keepdims=True) + NORM_EPS)
        bh = beta_all[:, SMALL_BETA + h:SMALL_BETA + h + 1]
        gcol = gcol_all[:, SMALL_A + h:SMALL_A + h + 1]
        grow = grow_all[SMALL_A + h:SMALL_A + h + 1, :]
        decay = jnp.where(incl, jnp.exp(jnp.where(incl, gcol - grow, 0.0)), 0.0)
        kb = kh * bh
        m = jnp.where(strict, _dot_nt(kb, kh) * decay, 0.0)
        size = min(chunk, GDN_BASE)
        pw = -jnp.where(same_blk(size), m, 0.0)
        t = eye + pw
        for _ in range(int(math.log2(size)) - 1):
            pw = _dot(pw, pw)
            t = t + _dot(t, pw)
        while size < chunk:
            off = jnp.where(jnp.logical_and(same_blk(2 * size), jnp.logical_not(same_blk(size))), m, 0.0)
            t = t - _dot(_dot(t, off), t)
            size *= 2
        u = _dot(t, vh * bh)
        w = _dot(t, kb * jnp.exp(gcol))
        attn = jnp.where(incl, _dot_nt(qh, kh) * decay, 0.0)
        qg = qh * jnp.exp(gcol)
        s_h = s_scr[h]
        vnews, inters = [], []
        for cidx in range(n_chunks):
            rows = slice(cidx * chunk, (cidx + 1) * chunk)
            vnew = u[rows] - _dot(w[rows], s_h)
            inters.append(_dot(qg[rows], s_h))
            glast = gcol[(cidx + 1) * chunk - 1:(cidx + 1) * chunk, :]
            s_h = s_h * jnp.exp(glast) + _dot_tn(kh[rows] * jnp.exp(glast - gcol[rows]), vnew)
            vnews.append(vnew)
        s_scr[h] = s_h
        vnew_all = vnews[0] if n_chunks == 1 else jnp.concatenate(vnews, axis=0)
        inter = inters[0] if n_chunks == 1 else jnp.concatenate(inters, axis=0)
        o = inter + _dot(attn, vnew_all)
        o = o * lax.rsqrt(jnp.mean(o * o, axis=1, keepdims=True) + NORM_EPS) * ng_ref[:, hs]
        o_ref[:, hs] = o * _silu(gate[:, hs])

    @pl.when(step == pl.num_programs(1) - 1)
    def _():
        sout_ref[0] = s_scr[...]


def _gdn(dqkv, small, dgate, tail0, s0, conv_w, alog_row, dtb_row, ng_row, batch, seq, chunk, rb):
    steps = seq // rb
    row = lambda w: pl.BlockSpec((rb, w), lambda b, i: (b * steps + i, 0))
    const = lambda shp: pl.BlockSpec(shp, lambda b, i: (0,) * len(shp))
    return pl.pallas_call(
        functools.partial(_gdn_kernel, chunk=chunk),
        out_shape=[jax.ShapeDtypeStruct((dqkv.shape[0], W_GROUP), F32),
                   jax.ShapeDtypeStruct((batch, N_H, HEAD_DIM, HEAD_DIM), F32)],
        grid=(batch, steps),
        in_specs=[row(3 * W_GROUP), row(LANES), row(W_GROUP),
                  pl.BlockSpec((1, 8, 3 * W_GROUP), lambda b, i: (b, 0, 0)),
                  pl.BlockSpec((1, N_H, HEAD_DIM, HEAD_DIM), lambda b, i: (b, 0, 0, 0)),
                  const((CONV_W, 3 * W_GROUP)), const((1, LANES)), const((1, LANES)), const((1, W_GROUP))],
        out_specs=[row(W_GROUP), pl.BlockSpec((1, N_H, HEAD_DIM, HEAD_DIM), lambda b, i: (b, 0, 0, 0))],
        scratch_shapes=[pltpu.VMEM((N_H, HEAD_DIM, HEAD_DIM), F32), pltpu.VMEM((8, 3 * W_GROUP), F32)],
        compiler_params=_cparams(("parallel", "arbitrary")),
        name="gdn",
    )(dqkv, small, dgate, tail0, s0, conv_w, alog_row, dtb_row, ng_row)


def _lane_row(vec, offset):
    return jnp.zeros((1, LANES), F32).at[0, offset:offset + vec.shape[0]].set(vec)


def _pad_rows(x, bs, s, rows):
    w = x.shape[-1]
    return jnp.concatenate([x.reshape(bs, s, w), jnp.zeros((bs, rows - s, w), x.dtype)], axis=1)


def kernel(x_prompt, x_sample, cache_dsa_k, cache_dsa_v, cache_dsa_kidx, cache_sb_k, cache_sb_v, state_gdn_S,
           state_gdn_conv, page_table, c_prompt, c_sample, w_cond, b_cond, w_in, w_out, ln_g, ln_b, conv_w, a_log,
           dt_bias, gdn_norm_g, gmlp_w_s, gmlp_b_s, w_ff1, w_ff2):
    bp, seq, d = x_prompt.shape
    bs, s, _ = x_sample.shape
    depth = w_in.shape[0]
    page = cache_dsa_k.shape[2]
    alpha = (2 * depth) ** 0.25
    assert seq % BLK == 0 and page == BLK and s >= CONV_W - 1 and s % 8 == 0 and s <= BLK

    c_all = jnp.concatenate([c_prompt, c_sample], axis=0)
    rc = ((c_all.shape[0] + 7) // 8) * 8
    c_all = jnp.concatenate([c_all, jnp.zeros((rc - c_all.shape[0], d), F32)], axis=0)
    mod = _cond(c_all, w_cond, b_cond)

    cache_it = jnp.transpose(cache_dsa_kidx, (0, 1, 3, 2))
    cache_akt, cache_avt, cache_ckt, cache_cvt = (jnp.transpose(t, (0, 1, 3, 4, 2))
                                                  for t in (cache_dsa_k, cache_dsa_v, cache_sb_k, cache_sb_v))

    xp = x_prompt.reshape(bp * seq, d)
    xs = x_sample.reshape(bs * s, d)
    c_gmlp = min(seq, GMLP_CHUNK)
    cs_gmlp = min(s, GMLP_CHUNK)
    st_p, st_s = [], []
    for l in range(depth):
        w_pad = _pad_w_in(w_in[l])
        wout_b, w1_b, w2_b = w_out[l].astype(BF16), w_ff1[l].astype(BF16), w_ff2[l].astype(BF16)
        alog_row, dtb_row = _lane_row(a_log[l], SMALL_A), _lane_row(dt_bias[l], SMALL_A)
        ng_row = jnp.tile(gdn_norm_g[l], N_H).reshape(1, W_GROUP)
        pre_ln = l == 0

        assert bp == 1
        mods = [mod[l, 0:1, i * d:(i + 1) * d] for i in range(6)]
        pr = _proj(xp, mods[0], mods[1], w_pad, pre_ln, 512 if (bp * seq) % 512 == 0 else BLK, True)
        out_a = _dsa_prompt(pr["aq"], pr["iq"], pr["small"], pr["ik_b"], pr["ak_b"], pr["av_t"], bp, seq)
        out_b, _ = _gmlp(pr["bu"], pr["bv"], gmlp_w_s[l][:, :c_gmlp, :c_gmlp], gmlp_b_s[l][:, :c_gmlp].T,
                         c_gmlp, c_gmlp, False)
        out_c = _sb_prompt(pr["cq"], pr["ck_b"], pr["cv_b"], bp, seq)
        gchunk = math.gcd(seq, GDN_CHUNK)
        out_d, s_new = _gdn(pr["dqkv"], pr["small"], pr["dgate"], jnp.zeros((bp, 8, 3 * W_GROUP), F32),
                            jnp.zeros((bp, N_H, HEAD_DIM, HEAD_DIM), F32), conv_w[l], alog_row, dtb_row, ng_row,
                            bp, seq, gchunk, 256 if seq % 256 == 0 else gchunk)
        xp = _outffn(xp, (out_a, out_b, out_c, out_d), mods[2], mods[3], mods[4], mods[5], wout_b, ln_g[l], ln_b[l],
                     w1_b, w2_b, pre_ln, alpha, tm=256 if (bp * seq) % 256 == 0 else BLK)
        buf_new = pr["dqkv"].reshape(bp, seq, 3 * W_GROUP)[:, seq - (CONV_W - 1):]
        st_p.append((pr["ak"].reshape(bp, seq, N_H, HEAD_DIM), pr["av"].reshape(bp, seq, N_H, HEAD_DIM),
                     pr["ik"].reshape(bp, seq, IDX_DIM), pr["ck"].reshape(bp, seq, N_H, HEAD_DIM),
                     pr["cv"].reshape(bp, seq, N_H, HEAD_DIM), s_new, buf_new))

        mods = [jnp.repeat(mod[l, bp:bp + bs, i * d:(i + 1) * d], s, axis=0) for i in range(6)]
        ps = _proj(xs, mods[0], mods[1], w_pad, pre_ln, bs * s, False)
        out_a = _dsa_sample(page_table, ps["aq"], ps["iq"], ps["small"], _pad_rows(ps["ik"], bs, s, BLK),
                            _pad_rows(ps["ak"], bs, s, BLK), _pad_rows(ps["av"], bs, s, BLK),
                            cache_it, cache_akt, cache_avt, l, bs, s)
        out_b, v_rows = _gmlp(ps["bu"], ps["bv"], jnp.tile(gmlp_w_s[l][:, :cs_gmlp, :cs_gmlp], (1, bs, bs)),
                              jnp.tile(gmlp_b_s[l][:, :cs_gmlp].T, (bs, 1)), bs * s, cs_gmlp, True)
        out_c = _sb_sample(page_table, ps["cq"], _pad_rows(ps["ck"], bs, s, BLK), _pad_rows(ps["cv"], bs, s, BLK),
                           cache_ckt, cache_cvt, l, bs, s)
        tail0 = jnp.concatenate([jnp.zeros((bs, 8 - (CONV_W - 1), 3 * W_GROUP), F32), state_gdn_conv[l]], axis=1)
        out_d, s_new = _gdn(ps["dqkv"], ps["small"], ps["dgate"], tail0, state_gdn_S[l], conv_w[l], alog_row,
                            dtb_row, ng_row, bs, s, math.gcd(s, GDN_CHUNK), s)
        xs = _outffn(xs, (out_a, out_b, out_c, out_d), mods[2], mods[3], mods[4], mods[5], wout_b, ln_g[l], ln_b[l],
                     w1_b, w2_b, pre_ln, alpha, tm=bs * s)
        buf_new = ps["dqkv"].reshape(bs, s, 3 * W_GROUP)[:, s - (CONV_W - 1):]
        st_s.append((ps["ak"].reshape(bs, s, N_H, HEAD_DIM), ps["av"].reshape(bs, s, N_H, HEAD_DIM),
                     ps["ik"].reshape(bs, s, IDX_DIM), ps["ck"].reshape(bs, s, N_H, HEAD_DIM),
                     ps["cv"].reshape(bs, s, N_H, HEAD_DIM), s_new, buf_new, v_rows.reshape(bs, s, W_GROUP)))

    outs_p = tuple(jnp.stack(t) for t in zip(*st_p))
    outs_s = tuple(jnp.stack(t) for t in zip(*st_s))
    return (xp.reshape(bp, seq, d), xs.reshape(bs, s, d)) + outs_p + outs_s
```

```python
import functools
import math

import jax
import jax.numpy as jnp
from jax import lax
from jax.experimental import pallas as pl
from jax.experimental.pallas import tpu as pltpu

F32 = jnp.float32
BF16 = jnp.bfloat16
I32 = jnp.int32

LN_EPS = 1e-5
NORM_EPS = 1e-6
N_H = 4
HEAD_DIM = 64
W_GROUP = N_H * HEAD_DIM
IDX_HEADS = 4
IDX_DIM = 64
TOPK_MAX = 256
GMLP_CHUNK = 128
GDN_CHUNK = 64
CONV_W = 4
BLK = 128
LANES = 128
VMEM_LIMIT = 56 * 1024 * 1024
SB_ZERO_TAIL = -110.0
INT_MIN = -2 ** 31
INT_MAX = 2 ** 31 - 1
LOG2E = 1.4426950408889634
MASKED = -1e30
M_INIT = -1e20
UNROLL = 8
PAGE_UNROLL = 8
SB_RING = 3
PAGE_RING = 4
COUNT_UNROLL = 16
GDN_BASE = 8

PIECES = (("dqkv", 768, 768), ("aq", 256, 256), ("ak", 256, 256), ("av", 256, 256), ("iq", 256, 256),
          ("bu", 256, 256), ("bv", 256, 256), ("cq", 256, 256), ("ck", 256, 256), ("cv", 256, 256),
          ("dgate", 256, 256), ("ik", 64, 128), ("small", 12, 128))
N_PAD = sum(p[2] for p in PIECES)
SMALL_IW, SMALL_BETA, SMALL_A = 0, 4, 8
BF16_COPIES = ("ak", "ck", "cv", "ik")


def _cparams(sem, vmem=VMEM_LIMIT):
    return pltpu.CompilerParams(dimension_semantics=sem, vmem_limit_bytes=vmem)


def _ln_plain(x):
    mu = jnp.mean(x, axis=-1, keepdims=True)
    xc = x - mu
    var = jnp.mean(xc * xc, axis=-1, keepdims=True)
    return xc * lax.rsqrt(var + LN_EPS)


def _sigmoid(x):
    return 1.0 / (1.0 + jnp.exp(-x))


def _silu(x):
    return x * _sigmoid(x)


def _softplus(x):
    return jnp.maximum(x, 0.0) + jnp.log1p(jnp.exp(-jnp.abs(x)))


def _gelu_tanh(x):
    c = math.sqrt(2.0 / math.pi)
    return x * (0.5 * (1.0 + jnp.tanh(c * (x + 0.044715 * (x * x * x)))))


def _dot(a, b):
    return jnp.dot(a.astype(BF16), b.astype(BF16), preferred_element_type=F32)


def _dot_nt(a, b):
    return lax.dot_general(a.astype(BF16), b.astype(BF16), (((1,), (1,)), ((), ())),
                           preferred_element_type=F32)


def _dot_tn(a, b):
    return lax.dot_general(a.astype(BF16), b.astype(BF16), (((0,), (0,)), ((), ())),
                           preferred_element_type=F32)


def _split(x):
    hi = x.astype(BF16)
    lo = (x - hi.astype(F32)).astype(BF16)
    return hi, lo


def _dot_precise_lhs(a, b01):
    hi, lo = _split(a)
    return jnp.dot(hi, b01, preferred_element_type=F32) + jnp.dot(lo, b01, preferred_element_type=F32)


def _dot_precise_rhs(a01, b):
    hi, lo = _split(b)
    return jnp.dot(a01, hi, preferred_element_type=F32) + jnp.dot(a01, lo, preferred_element_type=F32)


def _dot3(a, b):
    ah, al = _split(a)
    bh, bl = _split(b)
    return (jnp.dot(ah, bh, preferred_element_type=F32) + jnp.dot(al, bh, preferred_element_type=F32)
            + jnp.dot(ah, bl, preferred_element_type=F32))


def _iota(shape, dim):
    return lax.broadcasted_iota(I32, shape, dim)


def _sort_key(score):
    bits = pltpu.bitcast(score, I32)
    return bits ^ ((bits >> 31) & jnp.int32(0x7FFFFFFF))


def _alibi_slope(h):
    return 2.0 ** (-8.0 * (h + 1) / N_H)


def _heads(x):
    return [x[:, h * HEAD_DIM:(h + 1) * HEAD_DIM] for h in range(N_H)]


def _cond_kernel(c_ref, w_ref, b_ref, o_ref):
    s = _silu(c_ref[...])
    o_ref[0] = _dot3(s, w_ref[0]) + b_ref[0]


def _cond(c_all, w_cond, b_cond):
    depth, d, n6 = w_cond.shape
    rc = c_all.shape[0]
    tn = 1536 if n6 % 1536 == 0 else n6
    return pl.pallas_call(
        _cond_kernel,
        out_shape=jax.ShapeDtypeStruct((depth, rc, n6), F32),
        grid=(depth, n6 // tn),
        in_specs=[pl.BlockSpec((rc, d), lambda l, j: (0, 0)),
                  pl.BlockSpec((1, d, tn), lambda l, j: (l, 0, j)),
                  pl.BlockSpec((1, 1, tn), lambda l, j: (l, 0, j))],
        out_specs=pl.BlockSpec((1, rc, tn), lambda l, j: (l, 0, j)),
        compiler_params=_cparams(("arbitrary", "arbitrary")),
        name="cond",
    )(c_all, w_cond, b_cond.reshape(depth, 1, n6))


def _proj_kernel(x_ref, sh_ref, sc_ref, w_ref, *o_refs, pre_ln, extras):
    x = x_ref[...]
    if pre_ln:
        x = _ln_plain(x)
    hb = (x * (1.0 + sc_ref[...]) + sh_ref[...]).astype(BF16)
    col = 0
    for (name, width, padded), o_ref in zip(PIECES, o_refs[:len(PIECES)]):
        res = jnp.dot(hb, w_ref[:, col:col + padded], preferred_element_type=F32)
        o_ref[...] = res if o_ref.shape[-1] == padded else res[:, :o_ref.shape[-1]]
        if extras and name in BF16_COPIES:
            o_refs[len(PIECES) + BF16_COPIES.index(name)][...] = res.astype(BF16)
        if extras and name == "av":
            vt_ref = o_refs[len(PIECES) + len(BF16_COPIES)]
            for j in range(vt_ref.shape[0]):
                vt_ref[j] = res[j * BLK:(j + 1) * BLK, :].T.astype(BF16)
        col += padded


def _mod_spec(m, tm, d):
    if m.shape[0] == 1:
        return pl.BlockSpec((1, d), lambda i: (0, 0))
    return pl.BlockSpec((tm, d), lambda i: (i, 0))


def _proj(x, shift, scale, w_pad, pre_ln, tm, extras):
    r, d = x.shape
    out_shapes, out_specs = [], []
    for name, width, padded in PIECES:
        w_out = width if name == "ik" else padded
        out_shapes.append(jax.ShapeDtypeStruct((r, w_out), F32))
        out_specs.append(pl.BlockSpec((tm, w_out), lambda i: (i, 0)))
    if extras:
        for name in BF16_COPIES:
            w_out = LANES if name == "ik" else W_GROUP
            out_shapes.append(jax.ShapeDtypeStruct((r, w_out), BF16))
            out_specs.append(pl.BlockSpec((tm, w_out), lambda i: (i, 0)))
        out_shapes.append(jax.ShapeDtypeStruct((r // BLK, W_GROUP, BLK), BF16))
        out_specs.append(pl.BlockSpec((tm // BLK, W_GROUP, BLK), lambda i: (i, 0, 0)))
    outs = pl.pallas_call(
        functools.partial(_proj_kernel, pre_ln=pre_ln, extras=extras),
        out_shape=out_shapes,
        grid=(r // tm,),
        in_specs=[pl.BlockSpec((tm, d), lambda i: (i, 0)), _mod_spec(shift, tm, d), _mod_spec(scale, tm, d),
                  pl.BlockSpec((d, N_PAD), lambda i: (0, 0))],
        out_specs=out_specs,
        compiler_params=_cparams(("parallel",)),
        name="proj",
    )(x, shift, scale, w_pad)
    res = {name: o for (name, _, _), o in zip(PIECES, outs)}
    if extras:
        for i, name in enumerate(BF16_COPIES):
            res[name + "_b"] = outs[len(PIECES) + i]
        res["av_t"] = outs[len(PIECES) + len(BF16_COPIES)]
    return res


def _pad_w_in(w_in_l):
    off = {}
    start = 0
    for name, n in (("aq", 256), ("ak", 256), ("av", 256), ("iq", 256), ("ik", 64), ("iw", 4), ("bu", 256),
                    ("bv", 256), ("cq", 256), ("ck", 256), ("cv", 256), ("dqkv", 768), ("dbeta", 4), ("da", 4),
                    ("dgate", 256)):
        off[name] = (start, n)
        start += n
    d = w_in_l.shape[0]
    cols = []
    for name, width, padded in PIECES:
        if name == "small":
            parts = [w_in_l[:, off[k][0]:off[k][0] + off[k][1]] for k in ("iw", "dbeta", "da")]
            piece = jnp.concatenate(parts, axis=1)
        else:
            piece = w_in_l[:, off[name][0]:off[name][0] + off[name][1]]
        if padded > piece.shape[1]:
            piece = jnp.concatenate([piece, jnp.zeros((d, padded - piece.shape[1]), piece.dtype)], axis=1)
        cols.append(piece)
    return jnp.concatenate(cols, axis=1).astype(BF16)


def _outffn_kernel(x_ref, oa_ref, ob_ref, oc_ref, od_ref, gm_ref, shf_ref, scf_ref, gf_ref,
                   wout_ref, lng_ref, lnb_ref, w1_ref, w2_ref, o_ref, *, pre_ln, alpha):
    x = x_ref[...]
    if pre_ln:
        x = _ln_plain(x)
    mo = None
    for g, r in enumerate((oa_ref, ob_ref, oc_ref, od_ref)):
        part = _dot(r[...], wout_ref[g * W_GROUP:(g + 1) * W_GROUP, :])
        mo = part if mo is None else mo + part
    x1 = _ln_plain(alpha * x + gm_ref[...] * mo) * lng_ref[0:1, :] + lnb_ref[0:1, :]
    hf = x1 * (1.0 + scf_ref[...]) + shf_ref[...]
    a = jnp.maximum(_dot(hf, w1_ref[...]), 0.0)
    ff = _dot(a * a, w2_ref[...])
    o_ref[...] = _ln_plain(alpha * x1 + gf_ref[...] * ff) * lng_ref[1:2, :] + lnb_ref[1:2, :]


def _outffn(x, branches, gm, shf, scf, gf, wout_b, ln_g, ln_b, w1_b, w2_b, pre_ln, alpha, tm):
    r, d = x.shape
    dff = w1_b.shape[1]
    row = lambda w: pl.BlockSpec((tm, w), lambda i: (i, 0))
    const = lambda shp: pl.BlockSpec(shp, lambda i: (0, 0), pipeline_mode=pl.Buffered(1))
    return pl.pallas_call(
        functools.partial(_outffn_kernel, pre_ln=pre_ln, alpha=alpha),
        out_shape=jax.ShapeDtypeStruct((r, d), F32),
        grid=(r // tm,),
        in_specs=[row(d)] + [row(W_GROUP)] * 4 + [_mod_spec(m, tm, d) for m in (gm, shf, scf, gf)]
        + [const((d, d)), const((2, d)), const((2, d)), const((d, dff)), const((dff, d))],
        out_specs=row(d),
        compiler_params=_cparams(("parallel",)),
        name="outffn",
    )(x, *branches, gm, shf, scf, gf, wout_b, ln_g, ln_b, w1_b, w2_b)


def _gmlp_kernel(u_ref, v_ref, w_ref, bt_ref, ob_ref, *vn_refs, chunk):
    u = _gelu_tanh(u_ref[...])
    vn = _ln_plain(_gelu_tanh(v_ref[...]))
    if vn_refs:
        vn_refs[0][...] = vn
    rows = u.shape[0]
    ri, ci = _iota((rows, rows), 0), _iota((rows, rows), 1)
    mask = ci <= ri
    if rows != chunk:
        sh = int(math.log2(chunk))
        mask = mask & ((ri >> sh) == (ci >> sh))
    vb = vn.astype(BF16)
    for g in range(N_H):
        w = jnp.where(mask, w_ref[g], 0.0).astype(BF16)
        mixed = jnp.dot(w, vb[:, g * HEAD_DIM:(g + 1) * HEAD_DIM], preferred_element_type=F32)
        mixed = mixed + bt_ref[:, g:g + 1]
        ob_ref[:, g * HEAD_DIM:(g + 1) * HEAD_DIM] = u[:, g * HEAD_DIM:(g + 1) * HEAD_DIM] * mixed


def _gmlp(bu, bv, w_tiled, bt_tiled, rows, chunk, want_vn):
    r = bu.shape[0]
    row = pl.BlockSpec((rows, W_GROUP), lambda i: (i, 0))
    out_shape = [jax.ShapeDtypeStruct((r, W_GROUP), F32)]
    out_specs = [row]
    if want_vn:
        out_shape.append(jax.ShapeDtypeStruct((r, W_GROUP), F32))
        out_specs.append(row)
    outs = pl.pallas_call(
        functools.partial(_gmlp_kernel, chunk=chunk),
        out_shape=out_shape,
        grid=(r // rows,),
        in_specs=[row, row, pl.BlockSpec((N_H, rows, rows), lambda i: (0, 0, 0)),
                  pl.BlockSpec((rows, N_H), lambda i: (0, 0))],
        out_specs=out_specs,
        compiler_params=_cparams(("parallel",)),
        name="gmlp",
    )(bu, bv, w_tiled, bt_tiled)
    return outs if want_vn else (outs[0], None)


def _sb_block(qk, pv, vis, carry, acc, upper):
    new_carry, new_acc = [], []
    for h in range(N_H):
        z = qk(h)
        l_raw = -_softplus(z)
        l_vis = l_raw if vis is None else jnp.where(vis, l_raw, 0.0)
        tail = _dot_precise_lhs(l_vis, upper) + carry[h]
        w = jnp.exp(z + l_raw + tail)
        if vis is not None:
            w = jnp.where(vis, w, 0.0)
        new_acc.append(acc[h] + pv(h, w))
        new_carry.append(carry[h] + jnp.sum(l_vis, axis=1, keepdims=True))
    return new_carry, new_acc


def _sb_live(carry):
    m = carry[0]
    for c in carry[1:]:
        m = jnp.maximum(m, c)
    return (jnp.max(m) > SB_ZERO_TAIL).astype(I32)


def _upper_ones():
    return (_iota((BLK, BLK), 0) > _iota((BLK, BLK), 1)).astype(BF16)


def _natural_kv(q_heads, k, v):
    kh, vh = _heads(k), _heads(v)
    return (lambda h: _dot_nt(q_heads[h], kh[h])), (lambda h, w: _dot(w, vh[h]))


def _sbp_kernel(q_ref, k_ref, v_ref, o_ref):
    qb = pl.program_id(1)
    q_heads = _heads((q_ref[...] * (HEAD_DIM ** -0.5)).astype(BF16))
    upper = _upper_ones()
    ri, ci = _iota((BLK, BLK), 0), _iota((BLK, BLK), 1)
    zero_c = [jnp.zeros((BLK, 1), F32)] * N_H
    zero_a = [jnp.zeros((BLK, HEAD_DIM), F32)] * N_H
    off = pl.multiple_of(qb * BLK, BLK)
    qk, pv = _natural_kv(q_heads, k_ref[pl.ds(off, BLK), :], v_ref[pl.ds(off, BLK), :])
    carry, acc = _sb_block(qk, pv, ci < ri, zero_c, zero_a, upper)

    def cond(st):
        return jnp.logical_and(st[0] >= 0, st[1] > 0)

    def body(st):
        kb = st[0]
        o = pl.multiple_of(kb * BLK, BLK)
        qk, pv = _natural_kv(q_heads, k_ref[pl.ds(o, BLK), :], v_ref[pl.ds(o, BLK), :])
        c, a = _sb_block(qk, pv, None, list(st[2:2 + N_H]), list(st[2 + N_H:]), upper)
        return (kb - 1, _sb_live(c), *c, *a)

    st = lax.while_loop(cond, body, (qb - 1, _sb_live(carry), *carry, *acc))
    for h in range(N_H):
        o_ref[:, h * HEAD_DIM:(h + 1) * HEAD_DIM] = st[2 + N_H + h]


def _sb_prompt(cq, ck_b, cv_b, batch, seq):
    nq = seq // BLK
    full = pl.BlockSpec((seq, W_GROUP), lambda b, i: (b, 0), pipeline_mode=pl.Buffered(1))
    return pl.pallas_call(
        _sbp_kernel,
        out_shape=jax.ShapeDtypeStruct(cq.shape, F32),
        grid=(batch, nq),
        in_specs=[pl.BlockSpec((BLK, W_GROUP), lambda b, i: (b * nq + i, 0)), full, full],
        out_specs=pl.BlockSpec((BLK, W_GROUP), lambda b, i: (b * nq + i, 0)),
        compiler_params=_cparams(("parallel", "arbitrary")),
        name="sb_prompt",
    )(cq, ck_b, cv_b)


def _sbs_kernel(pt_ref, q_ref, kn_ref, vn_ref, kc_ref, vc_ref, o_ref, kbuf, vbuf, sem, *, layer, n_pages):
    b = pl.program_id(0)
    rows = q_ref.shape[0]
    q_heads = _heads((q_ref[...] * (HEAD_DIM ** -0.5)).astype(BF16))
    upper = _upper_ones()
    ri, ci = _iota((rows, BLK), 0), _iota((rows, BLK), 1)
    zero_c = [jnp.zeros((rows, 1), F32)] * N_H
    zero_a = [jnp.zeros((rows, HEAD_DIM), F32)] * N_H
    def copies(p):
        page, slot = pt_ref[b, p], p % SB_RING
        return (pltpu.make_async_copy(kc_ref.at[layer, page], kbuf.at[slot], sem.at[0, slot]),
                pltpu.make_async_copy(vc_ref.at[layer, page], vbuf.at[slot], sem.at[1, slot]))

    def start(p):
        for c in copies(p):
            c.start()

    def wait(p):
        for c in copies(p):
            c.wait()

    for j in range(1, min(SB_RING, n_pages + 1)):
        start(n_pages - j)
    qk, pv = _natural_kv(q_heads, kn_ref[0], vn_ref[0])
    carry, acc = _sb_block(qk, pv, ci < ri, zero_c, zero_a, upper)

    def cond(st):
        return jnp.logical_and(st[0] >= 0, st[1] > 0)

    def body(st):
        p = st[0]
        slot = p % SB_RING
        wait(p)

        @pl.when(p - (SB_RING - 1) >= 0)
        def _():
            start(p - (SB_RING - 1))

        c, a = _sb_block(lambda h: _dot(q_heads[h], kbuf[slot, h]), lambda h, w: _dot_nt(w, vbuf[slot, h]), None,
                         list(st[2:2 + N_H]), list(st[2 + N_H:]), upper)
        return (p - 1, _sb_live(c), *c, *a)

    st = lax.while_loop(cond, body, (jnp.int32(n_pages - 1), _sb_live(carry), *carry, *acc))
    for j in range(SB_RING - 1):
        @pl.when(st[0] - j >= 0)
        def _():
            wait(st[0] - j)

    for h in range(N_H):
        o_ref[:, h * HEAD_DIM:(h + 1) * HEAD_DIM] = st[2 + N_H + h]


def _sb_sample(page_table, cq, ck_new_pad, cv_new_pad, cache_kt, cache_vt, layer, bs, s):
    n_pages = page_table.shape[1]
    page = cache_kt.shape[-1]
    return pl.pallas_call(
        functools.partial(_sbs_kernel, layer=layer, n_pages=n_pages),
        out_shape=jax.ShapeDtypeStruct(cq.shape, F32),
        grid_spec=pltpu.PrefetchScalarGridSpec(
            num_scalar_prefetch=1,
            grid=(bs,),
            in_specs=[pl.BlockSpec((s, W_GROUP), lambda b, pt: (b, 0)),
                      pl.BlockSpec((1, BLK, W_GROUP), lambda b, pt: (b, 0, 0)),
                      pl.BlockSpec((1, BLK, W_GROUP), lambda b, pt: (b, 0, 0)),
                      pl.BlockSpec(memory_space=pl.ANY), pl.BlockSpec(memory_space=pl.ANY)],
            out_specs=pl.BlockSpec((s, W_GROUP), lambda b, pt: (b, 0)),
            scratch_shapes=[pltpu.VMEM((SB_RING, N_H, HEAD_DIM, page), F32),
                            pltpu.VMEM((SB_RING, N_H, HEAD_DIM, page), F32),
                            pltpu.SemaphoreType.DMA((2, SB_RING))]),
        compiler_params=_cparams(("arbitrary",)),
        name="sb_sample",
    )(page_table, cq, ck_new_pad, cv_new_pad, cache_kt, cache_vt)


def _kth_largest_key(count_ge, shape, k):
    kf = jnp.float32(k)
    t = jnp.where(count_ge(jnp.zeros(shape, I32)) >= kf, jnp.int32(0), jnp.int32(INT_MIN))

    def body(i, t):
        cand = t + (jnp.int32(1) << (30 - i))
        return jnp.where(count_ge(cand) >= kf, cand, t)

    return lax.fori_loop(0, 31, body, t)


def _dsap_kernel(aq_ref, iq_ref, sm_ref, ik_ref, k_ref, vt_ref, o_ref, key_ref, s_scr, *, topk):
    qb = pl.program_id(1)
    nkb = qb + 1
    ri, ci = _iota((BLK, BLK), 0), _iota((BLK, BLK), 1)
    diag_ok = ri <= ci

    iq_t = (iq_ref[...] * (IDX_DIM ** -0.5)).T
    iq_all = jnp.concatenate([iq_t[h * IDX_DIM:(h + 1) * IDX_DIM] for h in range(IDX_HEADS)],
                             axis=1).astype(BF16)
    sm_t = sm_ref[...].T
    iw = [sm_t[SMALL_IW + h:SMALL_IW + h + 1, :] * (IDX_HEADS ** -0.5) for h in range(IDX_HEADS)]

    def score_block(kb, diagonal):
        o = pl.multiple_of(kb * BLK, BLK)
        kidx = ik_ref[pl.ds(o, BLK), :][:, :IDX_DIM]
        s_all = jnp.dot(kidx, iq_all, preferred_element_type=F32)
        score = iw[0] * jnp.maximum(s_all[:, 0:BLK], 0.0)
        for h in range(1, IDX_HEADS):
            score = score + iw[h] * jnp.maximum(s_all[:, h * BLK:(h + 1) * BLK], 0.0)
        if diagonal:
            score = jnp.where(diag_ok, score, -jnp.inf)
        key = _sort_key(score)
        key_ref[kb] = key

    def score_group(g, _):
        for u in range(UNROLL):
            score_block(jnp.minimum(g * UNROLL + u, qb), False)
        return 0

    trips = (qb + UNROLL - 1) // UNROLL
    lax.fori_loop(0, trips, score_group, 0)
    score_block(qb, True)

    def count(pred):
        def group(g, cnt):
            for u in range(COUNT_UNROLL):
                idx = g * COUNT_UNROLL + u
                weight = (idx <= qb).astype(I32)
                cnt = cnt + jnp.where(pred(key_ref[jnp.minimum(idx, qb)]), weight, 0)
            return cnt

        cnt = lax.fori_loop(0, (nkb + COUNT_UNROLL - 1) // COUNT_UNROLL, group, jnp.zeros((BLK, BLK), I32))
        return jnp.sum(cnt.astype(F32), axis=0, keepdims=True)

    thr = _kth_largest_key(lambda cand: count(lambda key: key >= cand), (1, BLK), topk)
    n_gt = count(lambda key: key > thr)
    n_eq = count(lambda key: key == thr)
    room = jnp.float32(topk) - n_gt
    tie_break = jnp.max(n_eq - room) > 0.0

    aq_t = (aq_ref[...] * (HEAD_DIM ** -0.5 * LOG2E)).T
    bias = [(_alibi_slope(h) * LOG2E) * ri.astype(F32) for h in range(N_H)]
    zeros = jnp.zeros((HEAD_DIM, BLK), F32)
    q_bd = jnp.concatenate(
        [jnp.concatenate([aq_t[g * HEAD_DIM:(g + 1) * HEAD_DIM] if g == h else zeros for g in range(N_H)], axis=0)
         for h in range(N_H)], axis=1).astype(BF16)
    strict_lower = (ci < ri).astype(BF16)

    half = W_GROUP // 2

    def logits_to(slot, kb):
        kblk = k_ref[pl.ds(pl.multiple_of(kb * BLK, BLK), BLK), :]
        for g in range(2):
            s_scr[slot, :, g * 2 * BLK:(g + 1) * 2 * BLK] = jnp.dot(
                kblk[:, g * half:(g + 1) * half], q_bd[g * half:(g + 1) * half, g * 2 * BLK:(g + 1) * 2 * BLK],
                preferred_element_type=F32)

    def attend(ties, idx, slot, diagonal, st):
        kb = jnp.minimum(idx, qb)
        if diagonal:
            thr_b = thr
        else:
            logits_to(1 - slot, jnp.minimum(idx + 1, qb))
            thr_b = jnp.where(idx < qb, thr, jnp.int32(INT_MAX))
        seen = st[0]
        key = key_ref[kb]
        if ties:
            eq = key == thr_b
            rank = seen + jnp.dot(strict_lower, eq.astype(BF16), preferred_element_type=F32)
            sel = jnp.logical_or(key > thr_b, jnp.logical_and(eq, rank < room))
            seen = seen + jnp.sum(eq.astype(F32), axis=0, keepdims=True)
        else:
            sel = key >= thr_b
        if diagonal:
            sel = jnp.logical_and(sel, diag_ok)
        blk_off = ((kb - qb) * BLK).astype(F32)
        vt = vt_ref[kb]
        out = [seen]
        for h in range(N_H):
            m_old, l_old, a_old = st[1 + 3 * h], st[2 + 3 * h], st[3 + 3 * h]
            c_blk = (_alibi_slope(h) * LOG2E) * blk_off
            s = jnp.where(sel, s_scr[slot, :, h * BLK:(h + 1) * BLK] + bias[h], MASKED)
            m_new = jnp.maximum(m_old, jnp.max(s, axis=0, keepdims=True) + c_blk)
            p = jnp.exp2(s - (m_new - c_blk))
            alpha = jnp.exp2(m_old - m_new)
            pv = jnp.dot(vt[h * HEAD_DIM:(h + 1) * HEAD_DIM, :], p.astype(BF16), preferred_element_type=F32)
            out += [m_new, alpha * l_old + jnp.sum(p, axis=0, keepdims=True), alpha * a_old + pv]
        return tuple(out)

    init = [jnp.zeros((1, BLK), F32)]
    for h in range(N_H):
        init += [jnp.full((1, BLK), M_INIT, F32), jnp.zeros((1, BLK), F32), jnp.zeros((HEAD_DIM, BLK), F32)]
    def sweep(ties):
        def group(g, st):
            for u in range(UNROLL):
                st = attend(ties, g * UNROLL + u, u % 2, False, st)
            return st

        logits_to(0, 0)
        st = lax.fori_loop(0, trips, group, tuple(init))
        return attend(ties, qb, 0, True, st)

    st = lax.cond(tie_break, lambda: sweep(True), lambda: sweep(False))
    out_t = jnp.concatenate([st[3 + 3 * h] / st[2 + 3 * h] for h in range(N_H)], axis=0)
    o_ref[...] = out_t.T


def _dsa_prompt(aq, iq, small, ik_b, ak_b, av_t, batch, seq):
    nq = seq // BLK
    topk = min(TOPK_MAX, seq // 4)
    blk = lambda w: pl.BlockSpec((BLK, w), lambda b, i: (b * nq + i, 0))
    full = lambda w: pl.BlockSpec((seq, w), lambda b, i: (b, 0), pipeline_mode=pl.Buffered(1))
    return pl.pallas_call(
        functools.partial(_dsap_kernel, topk=topk),
        out_shape=jax.ShapeDtypeStruct(aq.shape, F32),
        grid=(batch, nq),
        in_specs=[blk(W_GROUP), blk(W_GROUP), blk(LANES), full(LANES), full(W_GROUP),
                  pl.BlockSpec((nq, W_GROUP, BLK), lambda b, i: (b, 0, 0), pipeline_mode=pl.Buffered(1))],
        out_specs=blk(W_GROUP),
        scratch_shapes=[pltpu.VMEM((nq, BLK, BLK), I32), pltpu.VMEM((2, BLK, N_H * BLK), F32)],
        compiler_params=_cparams(("parallel", "arbitrary")),
        name="dsa_prompt",
    )(aq, iq, small, ik_b, ak_b, av_t)


def _dsas_kernel(pt_ref, aq_ref, iq_ref, sm_ref, ikn_ref, akn_ref, avn_ref, ic_ref, kc_ref, vc_ref, o_ref,
                 key_ref, s_ref, ibuf, kbuf, vbuf, sem, *, layer, n_pages, topk):
    b = pl.program_id(0)
    s_rows = aq_ref.shape[0]
    past = n_pages * BLK
    ri, ci = _iota((s_rows, BLK), 0), _iota((s_rows, BLK), 1)
    new_valid = jnp.logical_and(ci <= ri, ci < s_rows)

    iq = iq_ref[...] * (IDX_DIM ** -0.5)
    iq_all = jnp.concatenate(_heads(iq), axis=0).astype(BF16)
    iw = sm_ref[...][:, SMALL_IW:SMALL_IW + IDX_HEADS] * (IDX_HEADS ** -0.5)

    def idx_copy(p, slot):
        return pltpu.make_async_copy(ic_ref.at[layer, pt_ref[b, p]], ibuf.at[slot], sem.at[0, slot])

    def k_copy(p, slot):
        return pltpu.make_async_copy(kc_ref.at[layer, pt_ref[b, p]], kbuf.at[slot], sem.at[1, slot])

    def v_copy(p, slot):
        return pltpu.make_async_copy(vc_ref.at[layer, pt_ref[b, p]], vbuf.at[slot], sem.at[2, slot])

    def score_keys(s_all, valid):
        score = iw[:, 0:1] * jnp.maximum(s_all[0:s_rows], 0.0)
        for h in range(1, IDX_HEADS):
            score = score + iw[:, h:h + 1] * jnp.maximum(s_all[h * s_rows:(h + 1) * s_rows], 0.0)
        if valid is not None:
            score = jnp.where(valid, score, -jnp.inf)
        return _sort_key(score)

    u_n = PAGE_UNROLL
    trips = n_pages // u_n

    def prefetch(copy):
        for g in range(min(PAGE_RING - 1, trips)):
            for u in range(u_n):
                copy(g * u_n + u, g * u_n + u).start()

    def stream(copy, body, carry):
        def trip(g, carry):
            ahead = g + PAGE_RING - 1

            @pl.when(ahead < trips)
            def _():
                for u in range(u_n):
                    copy(ahead * u_n + u, (ahead % PAGE_RING) * u_n + u).start()

            base = (g % PAGE_RING) * u_n
            for u in range(u_n):
                copy(g * u_n + u, base + u).wait()
            for u in range(u_n):
                carry = body(g * u_n + u, base + u, carry)
            return carry

        return lax.fori_loop(0, trips, trip, carry)

    prefetch(idx_copy)
    prefetch(k_copy)
    prefetch(v_copy)

    def p1(p, slot, _):
        key_ref[p] = score_keys(_dot(iq_all, ibuf[slot]), None)
        return 0

    stream(idx_copy, p1, 0)
    key_ref[n_pages] = score_keys(_dot_nt(iq_all, ikn_ref[0]), new_valid)

    def count(pred):
        per_trip = math.gcd(n_pages, 16)

        def body(g, cnt):
            for u in range(per_trip):
                cnt = cnt + jnp.where(pred(key_ref[g * per_trip + u]), 1, 0)
            return cnt
        cnt = lax.fori_loop(0, n_pages // per_trip, body, jnp.zeros((s_rows, BLK), I32))
        cnt = cnt + jnp.where(pred(key_ref[n_pages]), 1, 0)
        return jnp.sum(cnt.astype(F32), axis=1, keepdims=True)

    thr = _kth_largest_key(lambda cand: count(lambda key: key >= cand), (s_rows, 1), topk)
    n_gt = count(lambda key: key > thr)
    n_eq = count(lambda key: key == thr)
    room = jnp.float32(topk) - n_gt
    tie_break = jnp.max(n_eq - room) > 0.0

    q_heads = _heads((aq_ref[...] * (HEAD_DIM ** -0.5)).astype(BF16))
    lower = (_iota((BLK, BLK), 0) < _iota((BLK, BLK), 1)).astype(BF16)

    def logits_block(ties, p, qk, valid, kpos0, st):
        seen = st[0]
        key = key_ref[p]
        if ties:
            eq = key == thr
            rank = seen + jnp.dot(eq.astype(BF16), lower, preferred_element_type=F32)
            sel = jnp.logical_or(key > thr, jnp.logical_and(eq, rank < room))
            seen = seen + jnp.sum(eq.astype(F32), axis=1, keepdims=True)
        else:
            sel = key >= thr
        if valid is not None:
            sel = jnp.logical_and(sel, valid)
        dist = ((past - kpos0) + ri - ci).astype(F32)
        out = [seen]
        for h in range(N_H):
            s = jnp.where(sel, qk(h) - _alibi_slope(h) * dist, MASKED)
            s_ref[p, h] = s
            out.append(jnp.maximum(st[1 + h], s))
        return tuple(out)

    qk_new, pv_new = _natural_kv(q_heads, akn_ref[0], avn_ref[0])

    def pass_a(ties):
        st = stream(k_copy,
                    lambda p, slot, st: logits_block(ties, p, lambda h: _dot(q_heads[h], kbuf[slot, h]), None,
                                                     p * BLK, st),
                    (jnp.zeros((s_rows, 1), F32),) + (jnp.full((s_rows, BLK), MASKED, F32),) * N_H)
        return logits_block(ties, n_pages, qk_new, new_valid, past, st)[1:]

    run_max = lax.cond(tie_break, lambda: pass_a(True), lambda: pass_a(False))
    m_fin = [jnp.max(run_max[h], axis=1, keepdims=True) for h in range(N_H)]

    def values_block(p, pv, st):
        out_l, out_a = [], []
        for h in range(N_H):
            pr = jnp.exp(s_ref[p, h] - m_fin[h])
            out_l.append(st[h] + pr)
            out_a.append(st[N_H + h] + pv(h, pr))
        return tuple(out_l + out_a)

    st = stream(v_copy,
                lambda p, slot, st: values_block(p, lambda h, pr: _dot_nt(pr, vbuf[slot, h]), st),
                (jnp.zeros((s_rows, BLK), F32),) * N_H + (jnp.zeros((s_rows, HEAD_DIM), F32),) * N_H)
    st = values_block(n_pages, pv_new, st)
    for h in range(N_H):
        o_ref[:, h * HEAD_DIM:(h + 1) * HEAD_DIM] = st[N_H + h] / jnp.sum(st[h], axis=1, keepdims=True)


def _dsa_sample(page_table, aq, iq, small, ik_new_pad, ak_new_pad, av_new_pad, cache_it, cache_kt, cache_vt,
                layer, bs, s):
    n_pages = page_table.shape[1]
    page = cache_kt.shape[-1]
    topk = min(TOPK_MAX, (n_pages * page + s) // 4)
    row = lambda w: pl.BlockSpec((s, w), lambda b, pt: (b, 0))
    new = lambda w: pl.BlockSpec((1, BLK, w), lambda b, pt: (b, 0, 0))
    hbm = pl.BlockSpec(memory_space=pl.ANY)
    return pl.pallas_call(
        functools.partial(_dsas_kernel, layer=layer, n_pages=n_pages, topk=topk),
        out_shape=jax.ShapeDtypeStruct(aq.shape, F32),
        grid_spec=pltpu.PrefetchScalarGridSpec(
            num_scalar_prefetch=1,
            grid=(bs,),
            in_specs=[row(W_GROUP), row(W_GROUP), row(LANES), new(IDX_DIM), new(W_GROUP), new(W_GROUP),
                      hbm, hbm, hbm],
            out_specs=row(W_GROUP),
            scratch_shapes=[pltpu.VMEM((n_pages + 1, s, BLK), I32), pltpu.VMEM((n_pages + 1, N_H, s, BLK), F32),
                            pltpu.VMEM((PAGE_RING * PAGE_UNROLL, IDX_DIM, page), F32),
                            pltpu.VMEM((PAGE_RING * PAGE_UNROLL, N_H, HEAD_DIM, page), F32),
                            pltpu.VMEM((PAGE_RING * PAGE_UNROLL, N_H, HEAD_DIM, page), F32),
                            pltpu.SemaphoreType.DMA((3, PAGE_RING * PAGE_UNROLL))]),
        compiler_params=_cparams(("arbitrary",)),
        name="dsa_sample",
    )(page_table, aq, iq, small, ik_new_pad, ak_new_pad, av_new_pad, cache_it, cache_kt, cache_vt)


def _gdn_kernel(x_ref, sm_ref, gate_ref, tail0_ref, s0_ref, cw_ref, alog_ref, dtb_ref, ng_ref,
                o_ref, sout_ref, s_scr, tail_scr, *, chunk):
    step = pl.program_id(1)
    rb = x_ref.shape[0]
    n_chunks = rb // chunk

    @pl.when(step == 0)
    def _():
        s_scr[...] = s0_ref[0]
        tail_scr[...] = tail0_ref[0]

    x = x_ref[...]
    xfull = jnp.concatenate([tail_scr[...], x], axis=0)
    y = xfull[8:8 + rb] * cw_ref[CONV_W - 1:CONV_W, :]
    for j in range(CONV_W - 1):
        y = y + pltpu.roll(xfull, CONV_W - 1 - j, 0)[8:8 + rb] * cw_ref[j:j + 1, :]
    tail_scr[...] = xfull[rb:rb + 8]
    c = _silu(y)

    sm = sm_ref[...]
    beta_all = _sigmoid(sm)
    g_all = -jnp.exp(alog_ref[...]) * _softplus(sm + dtb_ref[...])
    rbp = ((rb + LANES - 1) // LANES) * LANES
    g_pad = g_all if rbp == rb else jnp.concatenate([g_all, jnp.zeros((rbp - rb, LANES), F32)], axis=0)
    g_t = g_pad.T[:, :rb]

    ri, ci = _iota((rb, rb), 0), _iota((rb, rb), 1)
    incl, strict, upper = ci <= ri, ci < ri, ri <= ci
    if n_chunks > 1:
        sh = int(math.log2(chunk))
        same = (ri >> sh) == (ci >> sh)
        incl, strict, upper = incl & same, strict & same, upper & same
    eye = (ri == ci).astype(F32)

    def same_blk(size):
        sh_b = int(math.log2(size))
        return (ri >> sh_b) == (ci >> sh_b)

    gcol_all = _dot_precise_rhs(incl.astype(BF16), g_all)
    grow_all = _dot_precise_lhs(g_t, upper.astype(BF16))
    gate = gate_ref[...]
    n_sq = int(math.log2(chunk))
    scale = HEAD_DIM ** -0.5

    for h in range(N_H):
        hs = slice(h * HEAD_DIM, (h + 1) * HEAD_DIM)
        qh = c[:, h * HEAD_DIM:(h + 1) * HEAD_DIM]
        kh = c[:, W_GROUP + h * HEAD_DIM:W_GROUP + (h + 1) * HEAD_DIM]
        vh = c[:, 2 * W_GROUP + h * HEAD_DIM:2 * W_GROUP + (h + 1) * HEAD_DIM]
        qh = qh * lax.rsqrt(jnp.sum(qh * qh, axis=1, keepdims=True) + NORM_EPS) * scale
        kh = kh * lax.rsqrt(jnp.sum(kh * kh, axis=1, keepdims=True) + NORM_EPS)
        bh = beta_all[:, SMALL_BETA + h:SMALL_BETA + h + 1]
        gcol = gcol_all[:, SMALL_A + h:SMALL_A + h + 1]
        grow = grow_all[SMALL_A + h:SMALL_A + h + 1, :]
        decay = jnp.where(incl, jnp.exp(jnp.where(incl, gcol - grow, 0.0)), 0.0)
        kb = kh * bh
        m = jnp.where(strict, _dot_nt(kb, kh) * decay, 0.0)
        size = min(chunk, GDN_BASE)
        pw = -jnp.where(same_blk(size), m, 0.0)
        t = eye + pw
        for _ in range(int(math.log2(size)) - 1):
            pw = _dot(pw, pw)
            t = t + _dot(t, pw)
        while size < chunk:
            off = jnp.where(jnp.logical_and(same_blk(2 * size), jnp.logical_not(same_blk(size))), m, 0.0)
            t = t - _dot(_dot(t, off), t)
            size *= 2
        u = _dot(t, vh * bh)
        w = _dot(t, kb * jnp.exp(gcol))
        attn = jnp.where(incl, _dot_nt(qh, kh) * decay, 0.0)
        qg = qh * jnp.exp(gcol)
        s_h = s_scr[h]
        vnews, inters = [], []
        for cidx in range(n_chunks):
            rows = slice(cidx * chunk, (cidx + 1) * chunk)
            vnew = u[rows] - _dot(w[rows], s_h)
            inters.append(_dot(qg[rows], s_h))
            glast = gcol[(cidx + 1) * chunk - 1:(cidx + 1) * chunk, :]
            s_h = s_h * jnp.exp(glast) + _dot_tn(kh[rows] * jnp.exp(glast - gcol[rows]), vnew)
            vnews.append(vnew)
        s_scr[h] = s_h
        vnew_all = vnews[0] if n_chunks == 1 else jnp.concatenate(vnews, axis=0)
        inter = inters[0] if n_chunks == 1 else jnp.concatenate(inters, axis=0)
        o = inter + _dot(attn, vnew_all)
        o = o * lax.rsqrt(jnp.mean(o * o, axis=1, keepdims=True) + NORM_EPS) * ng_ref[:, hs]
        o_ref[:, hs] = o * _silu(gate[:, hs])

    @pl.when(step == pl.num_programs(1) - 1)
    def _():
        sout_ref[0] = s_scr[...]


def _gdn(dqkv, small, dgate, tail0, s0, conv_w, alog_row, dtb_row, ng_row, batch, seq, chunk, rb):
    steps = seq // rb
    row = lambda w: pl.BlockSpec((rb, w), lambda b, i: (b * steps + i, 0))
    const = lambda shp: pl.BlockSpec(shp, lambda b, i: (0,) * len(shp))
    return pl.pallas_call(
        functools.partial(_gdn_kernel, chunk=chunk),
        out_shape=[jax.ShapeDtypeStruct((dqkv.shape[0], W_GROUP), F32),
                   jax.ShapeDtypeStruct((batch, N_H, HEAD_DIM, HEAD_DIM), F32)],
        grid=(batch, steps),
        in_specs=[row(3 * W_GROUP), row(LANES), row(W_GROUP),
                  pl.BlockSpec((1, 8, 3 * W_GROUP), lambda b, i: (b, 0, 0)),
                  pl.BlockSpec((1, N_H, HEAD_DIM, HEAD_DIM), lambda b, i: (b, 0, 0, 0)),
                  const((CONV_W, 3 * W_GROUP)), const((1, LANES)), const((1, LANES)), const((1, W_GROUP))],
        out_specs=[row(W_GROUP), pl.BlockSpec((1, N_H, HEAD_DIM, HEAD_DIM), lambda b, i: (b, 0, 0, 0))],
        scratch_shapes=[pltpu.VMEM((N_H, HEAD_DIM, HEAD_DIM), F32), pltpu.VMEM((8, 3 * W_GROUP), F32)],
        compiler_params=_cparams(("parallel", "arbitrary")),
        name="gdn",
    )(dqkv, small, dgate, tail0, s0, conv_w, alog_row, dtb_row, ng_row)


def _lane_row(vec, offset):
    return jnp.zeros((1, LANES), F32).at[0, offset:offset + vec.shape[0]].set(vec)


def _pad_rows(x, bs, s, rows):
    w = x.shape[-1]
    return jnp.concatenate([x.reshape(bs, s, w), jnp.zeros((bs, rows - s, w), x.dtype)], axis=1)


def kernel(x_prompt, x_sample, cache_dsa_k, cache_dsa_v, cache_dsa_kidx, cache_sb_k, cache_sb_v, state_gdn_S,
           state_gdn_conv, page_table, c_prompt, c_sample, w_cond, b_cond, w_in, w_out, ln_g, ln_b, conv_w, a_log,
           dt_bias, gdn_norm_g, gmlp_w_s, gmlp_b_s, w_ff1, w_ff2):
    bp, seq, d = x_prompt.shape
    bs, s, _ = x_sample.shape
    depth = w_in.shape[0]
    page = cache_dsa_k.shape[2]
    alpha = (2 * depth) ** 0.25
    assert seq % BLK == 0 and page == BLK and s >= CONV_W - 1 and s % 8 == 0 and s <= BLK

    c_all = jnp.concatenate([c_prompt, c_sample], axis=0)
    rc = ((c_all.shape[0] + 7) // 8) * 8
    c_all = jnp.concatenate([c_all, jnp.zeros((rc - c_all.shape[0], d), F32)], axis=0)
    mod = _cond(c_all, w_cond, b_cond)

    cache_it = jnp.transpose(cache_dsa_kidx, (0, 1, 3, 2))
    cache_akt, cache_avt, cache_ckt, cache_cvt = (jnp.transpose(t, (0, 1, 3, 4, 2))
                                                  for t in (cache_dsa_k, cache_dsa_v, cache_sb_k, cache_sb_v))

    xp = x_prompt.reshape(bp * seq, d)
    xs = x_sample.reshape(bs * s, d)
    c_gmlp = min(seq, GMLP_CHUNK)
    cs_gmlp = min(s, GMLP_CHUNK)
    st_p, st_s = [], []
    for l in range(depth):
        w_pad = _pad_w_in(w_in[l])
        wout_b, w1_b, w2_b = w_out[l].astype(BF16), w_ff1[l].astype(BF16), w_ff2[l].astype(BF16)
        alog_row, dtb_row = _lane_row(a_log[l], SMALL_A), _lane_row(dt_bias[l], SMALL_A)
        ng_row = jnp.tile(gdn_norm_g[l], N_H).reshape(1, W_GROUP)
        pre_ln = l == 0

        assert bp == 1
        mods = [mod[l, 0:1, i * d:(i + 1) * d] for i in range(6)]
        pr = _proj(xp, mods[0], mods[1], w_pad, pre_ln, 512 if (bp * seq) % 512 == 0 else BLK, True)
        out_a = _dsa_prompt(pr["aq"], pr["iq"], pr["small"], pr["ik_b"], pr["ak_b"], pr["av_t"], bp, seq)
        out_b, _ = _gmlp(pr["bu"], pr["bv"], gmlp_w_s[l][:, :c_gmlp, :c_gmlp], gmlp_b_s[l][:, :c_gmlp].T,
                         c_gmlp, c_gmlp, False)
        out_c = _sb_prompt(pr["cq"], pr["ck_b"], pr["cv_b"], bp, seq)
        gchunk = math.gcd(seq, GDN_CHUNK)
        out_d, s_new = _gdn(pr["dqkv"], pr["small"], pr["dgate"], jnp.zeros((bp, 8, 3 * W_GROUP), F32),
                            jnp.zeros((bp, N_H, HEAD_DIM, HEAD_DIM), F32), conv_w[l], alog_row, dtb_row, ng_row,
                            bp, seq, gchunk, 256 if seq % 256 == 0 else gchunk)
        xp = _outffn(xp, (out_a, out_b, out_c, out_d), mods[2], mods[3], mods[4], mods[5], wout_b, ln_g[l], ln_b[l],
                     w1_b, w2_b, pre_ln, alpha, tm=256 if (bp * seq) % 256 == 0 else BLK)
        buf_new = pr["dqkv"].reshape(bp, seq, 3 * W_GROUP)[:, seq - (CONV_W - 1):]
        st_p.append((pr["ak"].reshape(bp, seq, N_H, HEAD_DIM), pr["av"].reshape(bp, seq, N_H, HEAD_DIM),
                     pr["ik"].reshape(bp, seq, IDX_DIM), pr["ck"].reshape(bp, seq, N_H, HEAD_DIM),
                     pr["cv"].reshape(bp, seq, N_H, HEAD_DIM), s_new, buf_new))

        mods = [jnp.repeat(mod[l, bp:bp + bs, i * d:(i + 1) * d], s, axis=0) for i in range(6)]
        ps = _proj(xs, mods[0], mods[1], w_pad, pre_ln, bs * s, False)
        out_a = _dsa_sample(page_table, ps["aq"], ps["iq"], ps["small"], _pad_rows(ps["ik"], bs, s, BLK),
                            _pad_rows(ps["ak"], bs, s, BLK), _pad_rows(ps["av"], bs, s, BLK),
                            cache_it, cache_akt, cache_avt, l, bs, s)
        out_b, v_rows = _gmlp(ps["bu"], ps["bv"], jnp.tile(gmlp_w_s[l][:, :cs_gmlp, :cs_gmlp], (1, bs, bs)),
                              jnp.tile(gmlp_b_s[l][:, :cs_gmlp].T, (bs, 1)), bs * s, cs_gmlp, True)
        out_c = _sb_sample(page_table, ps["cq"], _pad_rows(ps["ck"], bs, s, BLK), _pad_rows(ps["cv"], bs, s, BLK),
                           cache_ckt, cache_cvt, l, bs, s)
        tail0 = jnp.concatenate([jnp.zeros((bs, 8 - (CONV_W - 1), 3 * W_GROUP), F32), state_gdn_conv[l]], axis=1)
        out_d, s_new = _gdn(ps["dqkv"], ps["small"], ps["dgate"], tail0, state_gdn_S[l], conv_w[l], alog_row,
                            dtb_row, ng_row, bs, s, math.gcd(s, GDN_CHUNK), s)
        xs = _outffn(xs, (out_a, out_b, out_c, out_d), mods[2], mods[3], mods[4], mods[5], wout_b, ln_g[l], ln_b[l],
                     w1_b, w2_b, pre_ln, alpha, tm=bs * s)
        buf_new = ps["dqkv"].reshape(bs, s, 3 * W_GROUP)[:, s - (CONV_W - 1):]
        st_s.append((ps["ak"].reshape(bs, s, N_H, HEAD_DIM), ps["av"].reshape(bs, s, N_H, HEAD_DIM),
                     ps["ik"].reshape(bs, s, IDX_DIM), ps["ck"].reshape(bs, s, N_H, HEAD_DIM),
                     ps["cv"].reshape(bs, s, N_H, HEAD_DIM), s_new, buf_new, v_rows.reshape(bs, s, W_GROUP)))

    outs_p = tuple(jnp.stack(t) for t in zip(*st_p))
    outs_s = tuple(jnp.stack(t) for t in zip(*st_s))
    return (xp.reshape(bp, seq, d), xs.reshape(bs, s, d)) + outs_p + outs_s
```

```python
import functools
import math

import jax
import jax.numpy as jnp
from jax import lax
from jax.experimental import pallas as pl
from jax.experimental.pallas import tpu as pltpu

F32 = jnp.float32
BF16 = jnp.bfloat16
I32 = jnp.int32

LN_EPS = 1e-5
NORM_EPS = 1e-6
N_H = 4
HEAD_DIM = 64
W_GROUP = N_H * HEAD_DIM
IDX_HEADS = 4
IDX_DIM = 64
TOPK_MAX = 256
GMLP_CHUNK = 128
GDN_CHUNK = 64
CONV_W = 4
BLK = 128
LANES = 128
VMEM_LIMIT = 56 * 1024 * 1024
SB_ZERO_TAIL = -110.0
INT_MIN = -2 ** 31
INT_MAX = 2 ** 31 - 1
LOG2E = 1.4426950408889634
MASKED = -1e30
M_INIT = -1e20
UNROLL = 8
PAGE_UNROLL = 8
SB_RING = 3
PAGE_RING = 4
COUNT_UNROLL = 16
GDN_BASE = 8

PIECES = (("dqkv", 768, 768), ("aq", 256, 256), ("ak", 256, 256), ("av", 256, 256), ("iq", 256, 256),
          ("bu", 256, 256), ("bv", 256, 256), ("cq", 256, 256), ("ck", 256, 256), ("cv", 256, 256),
          ("dgate", 256, 256), ("ik", 64, 128), ("small", 12, 128))
N_PAD = sum(p[2] for p in PIECES)
SMALL_IW, SMALL_BETA, SMALL_A = 0, 4, 8
BF16_COPIES = ("ak", "ck", "cv", "ik")


def _cparams(sem, vmem=VMEM_LIMIT):
    return pltpu.CompilerParams(dimension_semantics=sem, vmem_limit_bytes=vmem)


def _ln_plain(x):
    mu = jnp.mean(x, axis=-1, keepdims=True)
    xc = x - mu
    var = jnp.mean(xc * xc, axis=-1, keepdims=True)
    return xc * lax.rsqrt(var + LN_EPS)


def _sigmoid(x):
    return 1.0 / (1.0 + jnp.exp(-x))


def _silu(x):
    return x * _sigmoid(x)


def _softplus(x):
    return jnp.maximum(x, 0.0) + jnp.log1p(jnp.exp(-jnp.abs(x)))


def _gelu_tanh(x):
    c = math.sqrt(2.0 / math.pi)
    return x * (0.5 * (1.0 + jnp.tanh(c * (x + 0.044715 * (x * x * x)))))


def _dot(a, b):
    return jnp.dot(a.astype(BF16), b.astype(BF16), preferred_element_type=F32)


def _dot_nt(a, b):
    return lax.dot_general(a.astype(BF16), b.astype(BF16), (((1,), (1,)), ((), ())),
                           preferred_element_type=F32)


def _dot_tn(a, b):
    return lax.dot_general(a.astype(BF16), b.astype(BF16), (((0,), (0,)), ((), ())),
                           preferred_element_type=F32)


def _split(x):
    hi = x.astype(BF16)
    lo = (x - hi.astype(F32)).astype(BF16)
    return hi, lo


def _dot_precise_lhs(a, b01):
    hi, lo = _split(a)
    return jnp.dot(hi, b01, preferred_element_type=F32) + jnp.dot(lo, b01, preferred_element_type=F32)


def _dot_precise_rhs(a01, b):
    hi, lo = _split(b)
    return jnp.dot(a01, hi, preferred_element_type=F32) + jnp.dot(a01, lo, preferred_element_type=F32)


def _dot3(a, b):
    ah, al = _split(a)
    bh, bl = _split(b)
    return (jnp.dot(ah, bh, preferred_element_type=F32) + jnp.dot(al, bh, preferred_element_type=F32)
            + jnp.dot(ah, bl, preferred_element_type=F32))


def _iota(shape, dim):
    return lax.broadcasted_iota(I32, shape, dim)


def _sort_key(score):
    bits = pltpu.bitcast(score, I32)
    return bits ^ ((bits >> 31) & jnp.int32(0x7FFFFFFF))


def _alibi_slope(h):
    return 2.0 ** (-8.0 * (h + 1) / N_H)


def _heads(x):
    return [x[:, h * HEAD_DIM:(h + 1) * HEAD_DIM] for h in range(N_H)]


def _cond_kernel(c_ref, w_ref, b_ref, o_ref):
    s = _silu(c_ref[...])
    o_ref[0] = _dot3(s, w_ref[0]) + b_ref[0]


def _cond(c_all, w_cond, b_cond):
    depth, d, n6 = w_cond.shape
    rc = c_all.shape[0]
    tn = 1536 if n6 % 1536 == 0 else n6
    return pl.pallas_call(
        _cond_kernel,
        out_shape=jax.ShapeDtypeStruct((depth, rc, n6), F32),
        grid=(depth, n6 // tn),
        in_specs=[pl.BlockSpec((rc, d), lambda l, j: (0, 0)),
                  pl.BlockSpec((1, d, tn), lambda l, j: (l, 0, j)),
                  pl.BlockSpec((1, 1, tn), lambda l, j: (l, 0, j))],
        out_specs=pl.BlockSpec((1, rc, tn), lambda l, j: (l, 0, j)),
        compiler_params=_cparams(("arbitrary", "arbitrary")),
        name="cond",
    )(c_all, w_cond, b_cond.reshape(depth, 1, n6))


def _proj_kernel(x_ref, sh_ref, sc_ref, w_ref, *o_refs, pre_ln, extras):
    x = x_ref[...]
    if pre_ln:
        x = _ln_plain(x)
    hb = (x * (1.0 + sc_ref[...]) + sh_ref[...]).astype(BF16)
    col = 0
    for (name, width, padded), o_ref in zip(PIECES, o_refs[:len(PIECES)]):
        res = jnp.dot(hb, w_ref[:, col:col + padded], preferred_element_type=F32)
        o_ref[...] = res if o_ref.shape[-1] == padded else res[:, :o_ref.shape[-1]]
        if extras and name in BF16_COPIES:
            o_refs[len(PIECES) + BF16_COPIES.index(name)][...] = res.astype(BF16)
        if extras and name == "av":
            vt_ref = o_refs[len(PIECES) + len(BF16_COPIES)]
            for j in range(vt_ref.shape[0]):
                vt_ref[j] = res[j * BLK:(j + 1) * BLK, :].T.astype(BF16)
        col += padded


def _mod_spec(m, tm, d):
    if m.shape[0] == 1:
        return pl.BlockSpec((1, d), lambda i: (0, 0))
    return pl.BlockSpec((tm, d), lambda i: (i, 0))


def _proj(x, shift, scale, w_pad, pre_ln, tm, extras):
    r, d = x.shape
    out_shapes, out_specs = [], []
    for name, width, padded in PIECES:
        w_out = width if name == "ik" else padded
        out_shapes.append(jax.ShapeDtypeStruct((r, w_out), F32))
        out_specs.append(pl.BlockSpec((tm, w_out), lambda i: (i, 0)))
    if extras:
        for name in BF16_COPIES:
            w_out = LANES if name == "ik" else W_GROUP
            out_shapes.append(jax.ShapeDtypeStruct((r, w_out), BF16))
            out_specs.append(pl.BlockSpec((tm, w_out), lambda i: (i, 0)))
        out_shapes.append(jax.ShapeDtypeStruct((r // BLK, W_GROUP, BLK), BF16))
        out_specs.append(pl.BlockSpec((tm // BLK, W_GROUP, BLK), lambda i: (i, 0, 0)))
    outs = pl.pallas_call(
        functools.partial(_proj_kernel, pre_ln=pre_ln, extras=extras),
        out_shape=out_shapes,
        grid=(r // tm,),
        in_specs=[pl.BlockSpec((tm, d), lambda i: (i, 0)), _mod_spec(shift, tm, d), _mod_spec(scale, tm, d),
                  pl.BlockSpec((d, N_PAD), lambda i: (0, 0))],
        out_specs=out_specs,
        compiler_params=_cparams(("parallel",)),
        name="proj",
    )(x, shift, scale, w_pad)
    res = {name: o for (name, _, _), o in zip(PIECES, outs)}
    if extras:
        for i, name in enumerate(BF16_COPIES):
            res[name + "_b"] = outs[len(PIECES) + i]
        res["av_t"] = outs[len(PIECES) + len(BF16_COPIES)]
    return res


def _pad_w_in(w_in_l):
    off = {}
    start = 0
    for name, n in (("aq", 256), ("ak", 256), ("av", 256), ("iq", 256), ("ik", 64), ("iw", 4), ("bu", 256),
                    ("bv", 256), ("cq", 256), ("ck", 256), ("cv", 256), ("dqkv", 768), ("dbeta", 4), ("da", 4),
                    ("dgate", 256)):
        off[name] = (start, n)
        start += n
    d = w_in_l.shape[0]
    cols = []
    for name, width, padded in PIECES:
        if name == "small":
            parts = [w_in_l[:, off[k][0]:off[k][0] + off[k][1]] for k in ("iw", "dbeta", "da")]
            piece = jnp.concatenate(parts, axis=1)
        else:
            piece = w_in_l[:, off[name][0]:off[name][0] + off[name][1]]
        if padded > piece.shape[1]:
            piece = jnp.concatenate([piece, jnp.zeros((d, padded - piece.shape[1]), piece.dtype)], axis=1)
        cols.append(piece)
    return jnp.concatenate(cols, axis=1).astype(BF16)


def _outffn_kernel(x_ref, oa_ref, ob_ref, oc_ref, od_ref, gm_ref, shf_ref, scf_ref, gf_ref,
                   wout_ref, lng_ref, lnb_ref, w1_ref, w2_ref, o_ref, *, pre_ln, alpha):
    x = x_ref[...]
    if pre_ln:
        x = _ln_plain(x)
    mo = None
    for g, r in enumerate((oa_ref, ob_ref, oc_ref, od_ref)):
        part = _dot(r[...], wout_ref[g * W_GROUP:(g + 1) * W_GROUP, :])
        mo = part if mo is None else mo + part
    x1 = _ln_plain(alpha * x + gm_ref[...] * mo) * lng_ref[0:1, :] + lnb_ref[0:1, :]
    hf = x1 * (1.0 + scf_ref[...]) + shf_ref[...]
    a = jnp.maximum(_dot(hf, w1_ref[...]), 0.0)
    ff = _dot(a * a, w2_ref[...])
    o_ref[...] = _ln_plain(alpha * x1 + gf_ref[...] * ff) * lng_ref[1:2, :] + lnb_ref[1:2, :]


def _outffn(x, branches, gm, shf, scf, gf, wout_b, ln_g, ln_b, w1_b, w2_b, pre_ln, alpha, tm):
    r, d = x.shape
    dff = w1_b.shape[1]
    row = lambda w: pl.BlockSpec((tm, w), lambda i: (i, 0))
    const = lambda shp: pl.BlockSpec(shp, lambda i: (0, 0), pipeline_mode=pl.Buffered(1))
    return pl.pallas_call(
        functools.partial(_outffn_kernel, pre_ln=pre_ln, alpha=alpha),
        out_shape=jax.ShapeDtypeStruct((r, d), F32),
        grid=(r // tm,),
        in_specs=[row(d)] + [row(W_GROUP)] * 4 + [_mod_spec(m, tm, d) for m in (gm, shf, scf, gf)]
        + [const((d, d)), const((2, d)), const((2, d)), const((d, dff)), const((dff, d))],
        out_specs=row(d),
        compiler_params=_cparams(("parallel",)),
        name="outffn",
    )(x, *branches, gm, shf, scf, gf, wout_b, ln_g, ln_b, w1_b, w2_b)


def _gmlp_kernel(u_ref, v_ref, w_ref, bt_ref, ob_ref, *vn_refs, chunk):
    u = _gelu_tanh(u_ref[...])
    vn = _ln_plain(_gelu_tanh(v_ref[...]))
    if vn_refs:
        vn_refs[0][...] = vn
    rows = u.shape[0]
    ri, ci = _iota((rows, rows), 0), _iota((rows, rows), 1)
    mask = ci <= ri
    if rows != chunk:
        sh = int(math.log2(chunk))
        mask = mask & ((ri >> sh) == (ci >> sh))
    vb = vn.astype(BF16)
    for g in range(N_H):
        w = jnp.where(mask, w_ref[g], 0.0).astype(BF16)
        mixed = jnp.dot(w, vb[:, g * HEAD_DIM:(g + 1) * HEAD_DIM], preferred_element_type=F32)
        mixed = mixed + bt_ref[:, g:g + 1]
        ob_ref[:, g * HEAD_DIM:(g + 1) * HEAD_DIM] = u[:, g * HEAD_DIM:(g + 1) * HEAD_DIM] * mixed


def _gmlp(bu, bv, w_tiled, bt_tiled, rows, chunk, want_vn):
    r = bu.shape[0]
    row = pl.BlockSpec((rows, W_GROUP), lambda i: (i, 0))
    out_shape = [jax.ShapeDtypeStruct((r, W_GROUP), F32)]
    out_specs = [row]
    if want_vn:
        out_shape.append(jax.ShapeDtypeStruct((r, W_GROUP), F32))
        out_specs.append(row)
    outs = pl.pallas_call(
        functools.partial(_gmlp_kernel, chunk=chunk),
        out_shape=out_shape,
        grid=(r // rows,),
        in_specs=[row, row, pl.BlockSpec((N_H, rows, rows), lambda i: (0, 0, 0)),
                  pl.BlockSpec((rows, N_H), lambda i: (0, 0))],
        out_specs=out_specs,
        compiler_params=_cparams(("parallel",)),
        name="gmlp",
    )(bu, bv, w_tiled, bt_tiled)
    return outs if want_vn else (outs[0], None)


def _sb_block(qk, pv, vis, carry, acc, upper):
    new_carry, new_acc = [], []
    for h in range(N_H):
        z = qk(h)
        l_raw = -_softplus(z)
        l_vis = l_raw if vis is None else jnp.where(vis, l_raw, 0.0)
        tail = _dot_precise_lhs(l_vis, upper) + carry[h]
        w = jnp.exp(z + l_raw + tail)
        if vis is not None:
            w = jnp.where(vis, w, 0.0)
        new_acc.append(acc[h] + pv(h, w))
        new_carry.append(carry[h] + jnp.sum(l_vis, axis=1, keepdims=True))
    return new_carry, new_acc


def _sb_live(carry):
    m = carry[0]
    for c in carry[1:]:
        m = jnp.maximum(m, c)
    return (jnp.max(m) > SB_ZERO_TAIL).astype(I32)


def _upper_ones():
    return (_iota((BLK, BLK), 0) > _iota((BLK, BLK), 1)).astype(BF16)


def _natural_kv(q_heads, k, v):
    kh, vh = _heads(k), _heads(v)
    return (lambda h: _dot_nt(q_heads[h], kh[h])), (lambda h, w: _dot(w, vh[h]))


def _sbp_kernel(q_ref, k_ref, v_ref, o_ref):
    qb = pl.program_id(1)
    q_heads = _heads((q_ref[...] * (HEAD_DIM ** -0.5)).astype(BF16))
    upper = _upper_ones()
    ri, ci = _iota((BLK, BLK), 0), _iota((BLK, BLK), 1)
    zero_c = [jnp.zeros((BLK, 1), F32)] * N_H
    zero_a = [jnp.zeros((BLK, HEAD_DIM), F32)] * N_H
    off = pl.multiple_of(qb * BLK, BLK)
    qk, pv = _natural_kv(q_heads, k_ref[pl.ds(off, BLK), :], v_ref[pl.ds(off, BLK), :])
    carry, acc = _sb_block(qk, pv, ci < ri, zero_c, zero_a, upper)

    def cond(st):
        return jnp.logical_and(st[0] >= 0, st[1] > 0)

    def body(st):
        kb = st[0]
        o = pl.multiple_of(kb * BLK, BLK)
        qk, pv = _natural_kv(q_heads, k_ref[pl.ds(o, BLK), :], v_ref[pl.ds(o, BLK), :])
        c, a = _sb_block(qk, pv, None, list(st[2:2 + N_H]), list(st[2 + N_H:]), upper)
        return (kb - 1, _sb_live(c), *c, *a)

    st = lax.while_loop(cond, body, (qb - 1, _sb_live(carry), *carry, *acc))
    for h in range(N_H):
        o_ref[:, h * HEAD_DIM:(h + 1) * HEAD_DIM] = st[2 + N_H + h]


def _sb_prompt(cq, ck_b, cv_b, batch, seq):
    nq = seq // BLK
    full = pl.BlockSpec((seq, W_GROUP), lambda b, i: (b, 0), pipeline_mode=pl.Buffered(1))
    return pl.pallas_call(
        _sbp_kernel,
        out_shape=jax.ShapeDtypeStruct(cq.shape, F32),
        grid=(batch, nq),
        in_specs=[pl.BlockSpec((BLK, W_GROUP), lambda b, i: (b * nq + i, 0)), full, full],
        out_specs=pl.BlockSpec((BLK, W_GROUP), lambda b, i: (b * nq + i, 0)),
        compiler_params=_cparams(("parallel", "arbitrary")),
        name="sb_prompt",
    )(cq, ck_b, cv_b)


def _sbs_kernel(pt_ref, q_ref, kn_ref, vn_ref, kc_ref, vc_ref, o_ref, kbuf, vbuf, sem, *, layer, n_pages):
    b = pl.program_id(0)
    rows = q_ref.shape[0]
    q_heads = _heads((q_ref[...] * (HEAD_DIM ** -0.5)).astype(BF16))
    upper = _upper_ones()
    ri, ci = _iota((rows, BLK), 0), _iota((rows, BLK), 1)
    zero_c = [jnp.zeros((rows, 1), F32)] * N_H
    zero_a = [jnp.zeros((rows, HEAD_DIM), F32)] * N_H
    def copies(p):
        page, slot = pt_ref[b, p], p % SB_RING
        return (pltpu.make_async_copy(kc_ref.at[layer, page], kbuf.at[slot], sem.at[0, slot]),
                pltpu.make_async_copy(vc_ref.at[layer, page], vbuf.at[slot], sem.at[1, slot]))

    def start(p):
        for c in copies(p):
            c.start()

    def wait(p):
        for c in copies(p):
            c.wait()

    for j in range(1, min(SB_RING, n_pages + 1)):
        start(n_pages - j)
    qk, pv = _natural_kv(q_heads, kn_ref[0], vn_ref[0])
    carry, acc = _sb_block(qk, pv, ci < ri, zero_c, zero_a, upper)

    def cond(st):
        return jnp.logical_and(st[0] >= 0, st[1] > 0)

    def body(st):
        p = st[0]
        slot = p % SB_RING
        wait(p)

        @pl.when(p - (SB_RING - 1) >= 0)
        def _():
            start(p - (SB_RING - 1))

        c, a = _sb_block(lambda h: _dot(q_heads[h], kbuf[slot, h]), lambda h, w: _dot_nt(w, vbuf[slot, h]), None,
                         list(st[2:2 + N_H]), list(st[2 + N_H:]), upper)
        return (p - 1, _sb_live(c), *c, *a)

    st = lax.while_loop(cond, body, (jnp.int32(n_pages - 1), _sb_live(carry), *carry, *acc))
    for j in range(SB_RING - 1):
        @pl.when(st[0] - j >= 0)
        def _():
            wait(st[0] - j)

    for h in range(N_H):
        o_ref[:, h * HEAD_DIM:(h + 1) * HEAD_DIM] = st[2 + N_H + h]


def _sb_sample(page_table, cq, ck_new_pad, cv_new_pad, cache_kt, cache_vt, layer, bs, s):
    n_pages = page_table.shape[1]
    page = cache_kt.shape[-1]
    return pl.pallas_call(
        functools.partial(_sbs_kernel, layer=layer, n_pages=n_pages),
        out_shape=jax.ShapeDtypeStruct(cq.shape, F32),
        grid_spec=pltpu.PrefetchScalarGridSpec(
            num_scalar_prefetch=1,
            grid=(bs,),
            in_specs=[pl.BlockSpec((s, W_GROUP), lambda b, pt: (b, 0)),
                      pl.BlockSpec((1, BLK, W_GROUP), lambda b, pt: (b, 0, 0)),
                      pl.BlockSpec((1, BLK, W_GROUP), lambda b, pt: (b, 0, 0)),
                      pl.BlockSpec(memory_space=pl.ANY), pl.BlockSpec(memory_space=pl.ANY)],
            out_specs=pl.BlockSpec((s, W_GROUP), lambda b, pt: (b, 0)),
            scratch_shapes=[pltpu.VMEM((SB_RING, N_H, HEAD_DIM, page), F32),
                            pltpu.VMEM((SB_RING, N_H, HEAD_DIM, page), F32),
                            pltpu.SemaphoreType.DMA((2, SB_RING))]),
        compiler_params=_cparams(("arbitrary",)),
        name="sb_sample",
    )(page_table, cq, ck_new_pad, cv_new_pad, cache_kt, cache_vt)


def _kth_largest_key(count_ge, shape, k):
    kf = jnp.float32(k)
    t = jnp.where(count_ge(jnp.zeros(shape, I32)) >= kf, jnp.int32(0), jnp.int32(INT_MIN))

    def body(i, t):
        cand = t + (jnp.int32(1) << (30 - i))
        return jnp.where(count_ge(cand) >= kf, cand, t)

    return lax.fori_loop(0, 31, body, t)


def _dsap_kernel(aq_ref, iq_ref, sm_ref, ik_ref, k_ref, vt_ref, o_ref, key_ref, s_scr, *, topk):
    qb = pl.program_id(1)
    nkb = qb + 1
    ri, ci = _iota((BLK, BLK), 0), _iota((BLK, BLK), 1)
    diag_ok = ri <= ci

    iq_t = (iq_ref[...] * (IDX_DIM ** -0.5)).T
    iq_all = jnp.concatenate([iq_t[h * IDX_DIM:(h + 1) * IDX_DIM] for h in range(IDX_HEADS)],
                             axis=1).astype(BF16)
    sm_t = sm_ref[...].T
    iw = [sm_t[SMALL_IW + h:SMALL_IW + h + 1, :] * (IDX_HEADS ** -0.5) for h in range(IDX_HEADS)]

    def score_block(kb, diagonal):
        o = pl.multiple_of(kb * BLK, BLK)
        kidx = ik_ref[pl.ds(o, BLK), :][:, :IDX_DIM]
        s_all = jnp.dot(kidx, iq_all, preferred_element_type=F32)
        score = iw[0] * jnp.maximum(s_all[:, 0:BLK], 0.0)
        for h in range(1, IDX_HEADS):
            score = score + iw[h] * jnp.maximum(s_all[:, h * BLK:(h + 1) * BLK], 0.0)
        if diagonal:
            score = jnp.where(diag_ok, score, -jnp.inf)
        key = _sort_key(score)
        key_ref[kb] = key

    def score_group(g, _):
        for u in range(UNROLL):
            score_block(jnp.minimum(g * UNROLL + u, qb), False)
        return 0

    trips = (qb + UNROLL - 1) // UNROLL
    lax.fori_loop(0, trips, score_group, 0)
    score_block(qb, True)

    def count(pred):
        def group(g, cnt):
            for u in range(COUNT_UNROLL):
                idx = g * COUNT_UNROLL + u
                weight = (idx <= qb).astype(I32)
                cnt = cnt + jnp.where(pred(key_ref[jnp.minimum(idx, qb)]), weight, 0)
            return cnt

        cnt = lax.fori_loop(0, (nkb + COUNT_UNROLL - 1) // COUNT_UNROLL, group, jnp.zeros((BLK, BLK), I32))
        return jnp.sum(cnt.astype(F32), axis=0, keepdims=True)

    def count3(c1, c2, c3):
        def group(g, accs):
            accs = list(accs)
            for u in range(COUNT_UNROLL):
                idx = g * COUNT_UNROLL + u
                weight = (idx <= qb).astype(I32)
                key = key_ref[jnp.minimum(idx, qb)]
                for j in range(4):
                    part = key[j * 32:(j + 1) * 32]
                    for t, c in enumerate((c1, c2, c3)):
                        accs[t] = accs[t] + jnp.where(part >= c, weight, 0)
            return tuple(accs)

        zero = jnp.zeros((32, BLK), I32)
        accs = lax.fori_loop(0, (nkb + COUNT_UNROLL - 1) // COUNT_UNROLL, group, (zero, zero, zero))
        return [jnp.sum(a.astype(F32), axis=0, keepdims=True) for a in accs]

    kf = jnp.float32(topk)
    thr = jnp.where(count(lambda key: key >= 0) >= kf, jnp.int32(0), jnp.int32(INT_MIN))

    def two_bits(i, t):
        step = jnp.int32(1) << (29 - 2 * i)
        hits = count3(t + step, t + 2 * step, t + 3 * step)
        inc = sum((n >= kf).astype(I32) for n in hits)
        return t + inc * step

    thr = lax.fori_loop(0, 15, two_bits, thr)
    thr = jnp.where(count(lambda key: key >= thr + 1) >= kf, thr + 1, thr)
    n_gt = count(lambda key: key > thr)
    n_eq = count(lambda key: key == thr)
    room = jnp.float32(topk) - n_gt
    tie_break = jnp.max(n_eq - room) > 0.0

    aq_t = (aq_ref[...] * (HEAD_DIM ** -0.5 * LOG2E)).T
    bias = [(_alibi_slope(h) * LOG2E) * ri.astype(F32) for h in range(N_H)]
    zeros = jnp.zeros((HEAD_DIM, BLK), F32)
    q_bd = jnp.concatenate(
        [jnp.concatenate([aq_t[g * HEAD_DIM:(g + 1) * HEAD_DIM] if g == h else zeros for g in range(N_H)], axis=0)
         for h in range(N_H)], axis=1).astype(BF16)
    strict_lower = (ci < ri).astype(BF16)

    half = W_GROUP // 2

    def logits_to(slot, kb):
        kblk = k_ref[pl.ds(pl.multiple_of(kb * BLK, BLK), BLK), :]
        for g in range(2):
            s_scr[slot, :, g * 2 * BLK:(g + 1) * 2 * BLK] = jnp.dot(
                kblk[:, g * half:(g + 1) * half], q_bd[g * half:(g + 1) * half, g * 2 * BLK:(g + 1) * 2 * BLK],
                preferred_element_type=F32)

    def attend(ties, idx, slot, diagonal, st):
        kb = jnp.minimum(idx, qb)
        if diagonal:
            thr_b = thr
        else:
            logits_to(1 - slot, jnp.minimum(idx + 1, qb))
            thr_b = jnp.where(idx < qb, thr, jnp.int32(INT_MAX))
        seen = st[0]
        key = key_ref[kb]
        if ties:
            eq = key == thr_b
            rank = seen + jnp.dot(strict_lower, eq.astype(BF16), preferred_element_type=F32)
            sel = jnp.logical_or(key > thr_b, jnp.logical_and(eq, rank < room))
            seen = seen + jnp.sum(eq.astype(F32), axis=0, keepdims=True)
        else:
            sel = key >= thr_b
        if diagonal:
            sel = jnp.logical_and(sel, diag_ok)
        blk_off = ((kb - qb) * BLK).astype(F32)
        vt = vt_ref[kb]
        out = [seen]
        for h in range(N_H):
            m_old, l_old, a_old = st[1 + 3 * h], st[2 + 3 * h], st[3 + 3 * h]
            c_blk = (_alibi_slope(h) * LOG2E) * blk_off
            s = jnp.where(sel, s_scr[slot, :, h * BLK:(h + 1) * BLK] + bias[h], MASKED)
            m_new = jnp.maximum(m_old, jnp.max(s, axis=0, keepdims=True) + c_blk)
            p = jnp.exp2(s - (m_new - c_blk))
            alpha = jnp.exp2(m_old - m_new)
            pv = jnp.dot(vt[h * HEAD_DIM:(h + 1) * HEAD_DIM, :], p.astype(BF16), preferred_element_type=F32)
            out += [m_new, alpha * l_old + jnp.sum(p, axis=0, keepdims=True), alpha * a_old + pv]
        return tuple(out)

    init = [jnp.zeros((1, BLK), F32)]
    for h in range(N_H):
        init += [jnp.full((1, BLK), M_INIT, F32), jnp.zeros((1, BLK), F32), jnp.zeros((HEAD_DIM, BLK), F32)]
    def sweep(ties):
        def group(g, st):
            for u in range(UNROLL):
                st = attend(ties, g * UNROLL + u, u % 2, False, st)
            return st

        logits_to(0, 0)
        st = lax.fori_loop(0, trips, group, tuple(init))
        return attend(ties, qb, 0, True, st)

    st = lax.cond(tie_break, lambda: sweep(True), lambda: sweep(False))
    out_t = jnp.concatenate([st[3 + 3 * h] / st[2 + 3 * h] for h in range(N_H)], axis=0)
    o_ref[...] = out_t.T


def _dsa_prompt(aq, iq, small, ik_b, ak_b, av_t, batch, seq):
    nq = seq // BLK
    topk = min(TOPK_MAX, seq // 4)
    blk = lambda w: pl.BlockSpec((BLK, w), lambda b, i: (b * nq + i, 0))
    full = lambda w: pl.BlockSpec((seq, w), lambda b, i: (b, 0), pipeline_mode=pl.Buffered(1))
    return pl.pallas_call(
        functools.partial(_dsap_kernel, topk=topk),
        out_shape=jax.ShapeDtypeStruct(aq.shape, F32),
        grid=(batch, nq),
        in_specs=[blk(W_GROUP), blk(W_GROUP), blk(LANES), full(LANES), full(W_GROUP),
                  pl.BlockSpec((nq, W_GROUP, BLK), lambda b, i: (b, 0, 0), pipeline_mode=pl.Buffered(1))],
        out_specs=blk(W_GROUP),
        scratch_shapes=[pltpu.VMEM((nq, BLK, BLK), I32), pltpu.VMEM((2, BLK, N_H * BLK), F32)],
        compiler_params=_cparams(("parallel", "arbitrary")),
        name="dsa_prompt",
    )(aq, iq, small, ik_b, ak_b, av_t)


def _dsas_kernel(pt_ref, aq_ref, iq_ref, sm_ref, ikn_ref, akn_ref, avn_ref, ic_ref, kc_ref, vc_ref, o_ref,
                 key_ref, s_ref, ibuf, kbuf, vbuf, sem, *, layer, n_pages, topk):
    b = pl.program_id(0)
    s_rows = aq_ref.shape[0]
    past = n_pages * BLK
    ri, ci = _iota((s_rows, BLK), 0), _iota((s_rows, BLK), 1)
    new_valid = jnp.logical_and(ci <= ri, ci < s_rows)

    iq = iq_ref[...] * (IDX_DIM ** -0.5)
    iq_all = jnp.concatenate(_heads(iq), axis=0).astype(BF16)
    iw = sm_ref[...][:, SMALL_IW:SMALL_IW + IDX_HEADS] * (IDX_HEADS ** -0.5)

    def idx_copy(p, slot):
        return pltpu.make_async_copy(ic_ref.at[layer, pt_ref[b, p]], ibuf.at[slot], sem.at[0, slot])

    def k_copy(p, slot):
        return pltpu.make_async_copy(kc_ref.at[layer, pt_ref[b, p]], kbuf.at[slot], sem.at[1, slot])

    def v_copy(p, slot):
        return pltpu.make_async_copy(vc_ref.at[layer, pt_ref[b, p]], vbuf.at[slot], sem.at[2, slot])

    def score_keys(s_all, valid):
        score = iw[:, 0:1] * jnp.maximum(s_all[0:s_rows], 0.0)
        for h in range(1, IDX_HEADS):
            score = score + iw[:, h:h + 1] * jnp.maximum(s_all[h * s_rows:(h + 1) * s_rows], 0.0)
        if valid is not None:
            score = jnp.where(valid, score, -jnp.inf)
        return _sort_key(score)

    u_n = PAGE_UNROLL
    trips = n_pages // u_n

    def prefetch(copy):
        for g in range(min(PAGE_RING - 1, trips)):
            for u in range(u_n):
                copy(g * u_n + u, g * u_n + u).start()

    def stream(copy, body, carry):
        def trip(g, carry):
            ahead = g + PAGE_RING - 1

            @pl.when(ahead < trips)
            def _():
                for u in range(u_n):
                    copy(ahead * u_n + u, (ahead % PAGE_RING) * u_n + u).start()

            base = (g % PAGE_RING) * u_n
            for u in range(u_n):
                copy(g * u_n + u, base + u).wait()
            for u in range(u_n):
                carry = body(g * u_n + u, base + u, carry)
            return carry

        return lax.fori_loop(0, trips, trip, carry)

    prefetch(idx_copy)
    prefetch(k_copy)
    prefetch(v_copy)

    def p1(p, slot, _):
        key_ref[p] = score_keys(_dot(iq_all, ibuf[slot]), None)
        return 0

    stream(idx_copy, p1, 0)
    key_ref[n_pages] = score_keys(_dot_nt(iq_all, ikn_ref[0]), new_valid)

    def count(pred):
        per_trip = math.gcd(n_pages, 16)

        def body(g, cnt):
            for u in range(per_trip):
                cnt = cnt + jnp.where(pred(key_ref[g * per_trip + u]), 1, 0)
            return cnt
        cnt = lax.fori_loop(0, n_pages // per_trip, body, jnp.zeros((s_rows, BLK), I32))
        cnt = cnt + jnp.where(pred(key_ref[n_pages]), 1, 0)
        return jnp.sum(cnt.astype(F32), axis=1, keepdims=True)

    thr = _kth_largest_key(lambda cand: count(lambda key: key >= cand), (s_rows, 1), topk)
    n_gt = count(lambda key: key > thr)
    n_eq = count(lambda key: key == thr)
    room = jnp.float32(topk) - n_gt
    tie_break = jnp.max(n_eq - room) > 0.0

    q_heads = _heads((aq_ref[...] * (HEAD_DIM ** -0.5)).astype(BF16))
    lower = (_iota((BLK, BLK), 0) < _iota((BLK, BLK), 1)).astype(BF16)

    def logits_block(ties, p, qk, valid, kpos0, st):
        seen = st[0]
        key = key_ref[p]
        if ties:
            eq = key == thr
            rank = seen + jnp.dot(eq.astype(BF16), lower, preferred_element_type=F32)
            sel = jnp.logical_or(key > thr, jnp.logical_and(eq, rank < room))
            seen = seen + jnp.sum(eq.astype(F32), axis=1, keepdims=True)
        else:
            sel = key >= thr
        if valid is not None:
            sel = jnp.logical_and(sel, valid)
        dist = ((past - kpos0) + ri - ci).astype(F32)
        out = [seen]
        for h in range(N_H):
            s = jnp.where(sel, qk(h) - _alibi_slope(h) * dist, MASKED)
            s_ref[p, h] = s
            out.append(jnp.maximum(st[1 + h], s))
        return tuple(out)

    qk_new, pv_new = _natural_kv(q_heads, akn_ref[0], avn_ref[0])

    def pass_a(ties):
        st = stream(k_copy,
                    lambda p, slot, st: logits_block(ties, p, lambda h: _dot(q_heads[h], kbuf[slot, h]), None,
                                                     p * BLK, st),
                    (jnp.zeros((s_rows, 1), F32),) + (jnp.full((s_rows, BLK), MASKED, F32),) * N_H)
        return logits_block(ties, n_pages, qk_new, new_valid, past, st)[1:]

    run_max = lax.cond(tie_break, lambda: pass_a(True), lambda: pass_a(False))
    m_fin = [jnp.max(run_max[h], axis=1, keepdims=True) for h in range(N_H)]

    def values_block(p, pv, st):
        out_l, out_a = [], []
        for h in range(N_H):
            pr = jnp.exp(s_ref[p, h] - m_fin[h])
            out_l.append(st[h] + pr)
            out_a.append(st[N_H + h] + pv(h, pr))
        return tuple(out_l + out_a)

    st = stream(v_copy,
                lambda p, slot, st: values_block(p, lambda h, pr: _dot_nt(pr, vbuf[slot, h]), st),
                (jnp.zeros((s_rows, BLK), F32),) * N_H + (jnp.zeros((s_rows, HEAD_DIM), F32),) * N_H)
    st = values_block(n_pages, pv_new, st)
    for h in range(N_H):
        o_ref[:, h * HEAD_DIM:(h + 1) * HEAD_DIM] = st[N_H + h] / jnp.sum(st[h], axis=1, keepdims=True)


def _dsa_sample(page_table, aq, iq, small, ik_new_pad, ak_new_pad, av_new_pad, cache_it, cache_kt, cache_vt,
                layer, bs, s):
    n_pages = page_table.shape[1]
    page = cache_kt.shape[-1]
    topk = min(TOPK_MAX, (n_pages * page + s) // 4)
    row = lambda w: pl.BlockSpec((s, w), lambda b, pt: (b, 0))
    new = lambda w: pl.BlockSpec((1, BLK, w), lambda b, pt: (b, 0, 0))
    hbm = pl.BlockSpec(memory_space=pl.ANY)
    return pl.pallas_call(
        functools.partial(_dsas_kernel, layer=layer, n_pages=n_pages, topk=topk),
        out_shape=jax.ShapeDtypeStruct(aq.shape, F32),
        grid_spec=pltpu.PrefetchScalarGridSpec(
            num_scalar_prefetch=1,
            grid=(bs,),
            in_specs=[row(W_GROUP), row(W_GROUP), row(LANES), new(IDX_DIM), new(W_GROUP), new(W_GROUP),
                      hbm, hbm, hbm],
            out_specs=row(W_GROUP),
            scratch_shapes=[pltpu.VMEM((n_pages + 1, s, BLK), I32), pltpu.VMEM((n_pages + 1, N_H, s, BLK), F32),
                            pltpu.VMEM((PAGE_RING * PAGE_UNROLL, IDX_DIM, page), F32),
                            pltpu.VMEM((PAGE_RING * PAGE_UNROLL, N_H, HEAD_DIM, page), F32),
                            pltpu.VMEM((PAGE_RING * PAGE_UNROLL, N_H, HEAD_DIM, page), F32),
                            pltpu.SemaphoreType.DMA((3, PAGE_RING * PAGE_UNROLL))]),
        compiler_params=_cparams(("arbitrary",)),
        name="dsa_sample",
    )(page_table, aq, iq, small, ik_new_pad, ak_new_pad, av_new_pad, cache_it, cache_kt, cache_vt)


def _gdn_kernel(x_ref, sm_ref, gate_ref, tail0_ref, s0_ref, cw_ref, alog_ref, dtb_ref, ng_ref,
                o_ref, sout_ref, s_scr, tail_scr, *, chunk):
    step = pl.program_id(1)
    rb = x_ref.shape[0]
    n_chunks = rb // chunk

    @pl.when(step == 0)
    def _():
        s_scr[...] = s0_ref[0]
        tail_scr[...] = tail0_ref[0]

    x = x_ref[...]
    xfull = jnp.concatenate([tail_scr[...], x], axis=0)
    y = xfull[8:8 + rb] * cw_ref[CONV_W - 1:CONV_W, :]
    for j in range(CONV_W - 1):
        y = y + pltpu.roll(xfull, CONV_W - 1 - j, 0)[8:8 + rb] * cw_ref[j:j + 1, :]
    tail_scr[...] = xfull[rb:rb + 8]
    c = _silu(y)

    sm = sm_ref[...]
    beta_all = _sigmoid(sm)
    g_all = -jnp.exp(alog_ref[...]) * _softplus(sm + dtb_ref[...])
    rbp = ((rb + LANES - 1) // LANES) * LANES
    g_pad = g_all if rbp == rb else jnp.concatenate([g_all, jnp.zeros((rbp - rb, LANES), F32)], axis=0)
    g_t = g_pad.T[:, :rb]

    ri, ci = _iota((rb, rb), 0), _iota((rb, rb), 1)
    incl, strict, upper = ci <= ri, ci < ri, ri <= ci
    if n_chunks > 1:
        sh = int(math.log2(chunk))
        same = (ri >> sh) == (ci >> sh)
        incl, strict, upper = incl & same, strict & same, upper & same
    eye = (ri == ci).astype(F32)

    def same_blk(size):
        sh_b = int(math.log2(size))
        return (ri >> sh_b) == (ci >> sh_b)

    gcol_all = _dot_precise_rhs(incl.astype(BF16), g_all)
    grow_all = _dot_precise_lhs(g_t, upper.astype(BF16))
    gate = gate_ref[...]
    n_sq = int(math.log2(chunk))
    scale = HEAD_DIM ** -0.5

    for h in range(N_H):
        hs = slice(h * HEAD_DIM, (h + 1) * HEAD_DIM)
        qh = c[:, h * HEAD_DIM:(h + 1) * HEAD_DIM]
        kh = c[:, W_GROUP + h * HEAD_DIM:W_GROUP + (h + 1) * HEAD_DIM]
        vh = c[:, 2 * W_GROUP + h * HEAD_DIM:2 * W_GROUP + (h + 1) * HEAD_DIM]
        qh = qh * lax.rsqrt(jnp.sum(qh * qh, axis=1, keepdims=True) + NORM_EPS) * scale
        kh = kh * lax.rsqrt(jnp.sum(kh * kh, axis=1, keepdims=True) + NORM_EPS)
        bh = beta_all[:, SMALL_BETA + h:SMALL_BETA + h + 1]
        gcol = gcol_all[:, SMALL_A + h:SMALL_A + h + 1]
        grow = grow_all[SMALL_A + h:SMALL_A + h + 1, :]
        decay = jnp.where(incl, jnp.exp(jnp.where(incl, gcol - grow, 0.0)), 0.0)
        kb = kh * bh
        m = jnp.where(strict, _dot_nt(kb, kh) * decay, 0.0)
        size = min(chunk, GDN_BASE)
        pw = -jnp.where(same_blk(size), m, 0.0)
        t = eye + pw
        for _ in range(int(math.log2(size)) - 1):
            pw = _dot(pw, pw)
            t = t + _dot(t, pw)
        while size < chunk:
            off = jnp.where(jnp.logical_and(same_blk(2 * size), jnp.logical_not(same_blk(size))), m, 0.0)
            t = t - _dot(_dot(t, off), t)
            size *= 2
        u = _dot(t, vh * bh)
        w = _dot(t, kb * jnp.exp(gcol))
        attn = jnp.where(incl, _dot_nt(qh, kh) * decay, 0.0)
        qg = qh * jnp.exp(gcol)
        s_h = s_scr[h]
        vnews, inters = [], []
        for cidx in range(n_chunks):
            rows = slice(cidx * chunk, (cidx + 1) * chunk)
            vnew = u[rows] - _dot(w[rows], s_h)
            inters.append(_dot(qg[rows], s_h))
            glast = gcol[(cidx + 1) * chunk - 1:(cidx + 1) * chunk, :]
            s_h = s_h * jnp.exp(glast) + _dot_tn(kh[rows] * jnp.exp(glast - gcol[rows]), vnew)
            vnews.append(vnew)
        s_scr[h] = s_h
        vnew_all = vnews[0] if n_chunks == 1 else jnp.concatenate(vnews, axis=0)
        inter = inters[0] if n_chunks == 1 else jnp.concatenate(inters, axis=0)
        o = inter + _dot(attn, vnew_all)
        o = o * lax.rsqrt(jnp.mean(o * o, axis=1, keepdims=True) + NORM_EPS) * ng_ref[:, hs]
        o_ref[:, hs] = o * _silu(gate[:, hs])

    @pl.when(step == pl.num_programs(1) - 1)
    def _():
        sout_ref[0] = s_scr[...]


def _gdn(dqkv, small, dgate, tail0, s0, conv_w, alog_row, dtb_row, ng_row, batch, seq, chunk, rb):
    steps = seq // rb
    row = lambda w: pl.BlockSpec((rb, w), lambda b, i: (b * steps + i, 0))
    const = lambda shp: pl.BlockSpec(shp, lambda b, i: (0,) * len(shp))
    return pl.pallas_call(
        functools.partial(_gdn_kernel, chunk=chunk),
        out_shape=[jax.ShapeDtypeStruct((dqkv.shape[0], W_GROUP), F32),
                   jax.ShapeDtypeStruct((batch, N_H, HEAD_DIM, HEAD_DIM), F32)],
        grid=(batch, steps),
        in_specs=[row(3 * W_GROUP), row(LANES), row(W_GROUP),
                  pl.BlockSpec((1, 8, 3 * W_GROUP), lambda b, i: (b, 0, 0)),
                  pl.BlockSpec((1, N_H, HEAD_DIM, HEAD_DIM), lambda b, i: (b, 0, 0, 0)),
                  const((CONV_W, 3 * W_GROUP)), const((1, LANES)), const((1, LANES)), const((1, W_GROUP))],
        out_specs=[row(W_GROUP), pl.BlockSpec((1, N_H, HEAD_DIM, HEAD_DIM), lambda b, i: (b, 0, 0, 0))],
        scratch_shapes=[pltpu.VMEM((N_H, HEAD_DIM, HEAD_DIM), F32), pltpu.VMEM((8, 3 * W_GROUP), F32)],
        compiler_params=_cparams(("parallel", "arbitrary")),
        name="gdn",
    )(dqkv, small, dgate, tail0, s0, conv_w, alog_row, dtb_row, ng_row)


def _lane_row(vec, offset):
    return jnp.zeros((1, LANES), F32).at[0, offset:offset + vec.shape[0]].set(vec)


def _pad_rows(x, bs, s, rows):
    w = x.shape[-1]
    return jnp.concatenate([x.reshape(bs, s, w), jnp.zeros((bs, rows - s, w), x.dtype)], axis=1)


def kernel(x_prompt, x_sample, cache_dsa_k, cache_dsa_v, cache_dsa_kidx, cache_sb_k, cache_sb_v, state_gdn_S,
           state_gdn_conv, page_table, c_prompt, c_sample, w_cond, b_cond, w_in, w_out, ln_g, ln_b, conv_w, a_log,
           dt_bias, gdn_norm_g, gmlp_w_s, gmlp_b_s, w_ff1, w_ff2):
    bp, seq, d = x_prompt.shape
    bs, s, _ = x_sample.shape
    depth = w_in.shape[0]
    page = cache_dsa_k.shape[2]
    alpha = (2 * depth) ** 0.25
    assert seq % BLK == 0 and page == BLK and s >= CONV_W - 1 and s % 8 == 0 and s <= BLK

    c_all = jnp.concatenate([c_prompt, c_sample], axis=0)
    rc = ((c_all.shape[0] + 7) // 8) * 8
    c_all = jnp.concatenate([c_all, jnp.zeros((rc - c_all.shape[0], d), F32)], axis=0)
    mod = _cond(c_all, w_cond, b_cond)

    cache_it = jnp.transpose(cache_dsa_kidx, (0, 1, 3, 2))
    cache_akt, cache_avt, cache_ckt, cache_cvt = (jnp.transpose(t, (0, 1, 3, 4, 2))
                                                  for t in (cache_dsa_k, cache_dsa_v, cache_sb_k, cache_sb_v))

    xp = x_prompt.reshape(bp * seq, d)
    xs = x_sample.reshape(bs * s, d)
    c_gmlp = min(seq, GMLP_CHUNK)
    cs_gmlp = min(s, GMLP_CHUNK)
    st_p, st_s = [], []
    for l in range(depth):
        w_pad = _pad_w_in(w_in[l])
        wout_b, w1_b, w2_b = w_out[l].astype(BF16), w_ff1[l].astype(BF16), w_ff2[l].astype(BF16)
        alog_row, dtb_row = _lane_row(a_log[l], SMALL_A), _lane_row(dt_bias[l], SMALL_A)
        ng_row = jnp.tile(gdn_norm_g[l], N_H).reshape(1, W_GROUP)
        pre_ln = l == 0

        assert bp == 1
        mods = [mod[l, 0:1, i * d:(i + 1) * d] for i in range(6)]
        pr = _proj(xp, mods[0], mods[1], w_pad, pre_ln, 512 if (bp * seq) % 512 == 0 else BLK, True)
        out_a = _dsa_prompt(pr["aq"], pr["iq"], pr["small"], pr["ik_b"], pr["ak_b"], pr["av_t"], bp, seq)
        out_b, _ = _gmlp(pr["bu"], pr["bv"], gmlp_w_s[l][:, :c_gmlp, :c_gmlp], gmlp_b_s[l][:, :c_gmlp].T,
                         c_gmlp, c_gmlp, False)
        out_c = _sb_prompt(pr["cq"], pr["ck_b"], pr["cv_b"], bp, seq)
        gchunk = math.gcd(seq, GDN_CHUNK)
        out_d, s_new = _gdn(pr["dqkv"], pr["small"], pr["dgate"], jnp.zeros((bp, 8, 3 * W_GROUP), F32),
                            jnp.zeros((bp, N_H, HEAD_DIM, HEAD_DIM), F32), conv_w[l], alog_row, dtb_row, ng_row,
                            bp, seq, gchunk, 256 if seq % 256 == 0 else gchunk)
        xp = _outffn(xp, (out_a, out_b, out_c, out_d), mods[2], mods[3], mods[4], mods[5], wout_b, ln_g[l], ln_b[l],
                     w1_b, w2_b, pre_ln, alpha, tm=256 if (bp * seq) % 256 == 0 else BLK)
        buf_new = pr["dqkv"].reshape(bp, seq, 3 * W_GROUP)[:, seq - (CONV_W - 1):]
        st_p.append((pr["ak"].reshape(bp, seq, N_H, HEAD_DIM), pr["av"].reshape(bp, seq, N_H, HEAD_DIM),
                     pr["ik"].reshape(bp, seq, IDX_DIM), pr["ck"].reshape(bp, seq, N_H, HEAD_DIM),
                     pr["cv"].reshape(bp, seq, N_H, HEAD_DIM), s_new, buf_new))

        mods = [jnp.repeat(mod[l, bp:bp + bs, i * d:(i + 1) * d], s, axis=0) for i in range(6)]
        ps = _proj(xs, mods[0], mods[1], w_pad, pre_ln, bs * s, False)
        out_a = _dsa_sample(page_table, ps["aq"], ps["iq"], ps["small"], _pad_rows(ps["ik"], bs, s, BLK),
                            _pad_rows(ps["ak"], bs, s, BLK), _pad_rows(ps["av"], bs, s, BLK),
                            cache_it, cache_akt, cache_avt, l, bs, s)
        out_b, v_rows = _gmlp(ps["bu"], ps["bv"], jnp.tile(gmlp_w_s[l][:, :cs_gmlp, :cs_gmlp], (1, bs, bs)),
                              jnp.tile(gmlp_b_s[l][:, :cs_gmlp].T, (bs, 1)), bs * s, cs_gmlp, True)
        out_c = _sb_sample(page_table, ps["cq"], _pad_rows(ps["ck"], bs, s, BLK), _pad_rows(ps["cv"], bs, s, BLK),
                           cache_ckt, cache_cvt, l, bs, s)
        tail0 = jnp.concatenate([jnp.zeros((bs, 8 - (CONV_W - 1), 3 * W_GROUP), F32), state_gdn_conv[l]], axis=1)
        out_d, s_new = _gdn(ps["dqkv"], ps["small"], ps["dgate"], tail0, state_gdn_S[l], conv_w[l], alog_row,
                            dtb_row, ng_row, bs, s, math.gcd(s, GDN_CHUNK), s)
        xs = _outffn(xs, (out_a, out_b, out_c, out_d), mods[2], mods[3], mods[4], mods[5], wout_b, ln_g[l], ln_b[l],
                     w1_b, w2_b, pre_ln, alpha, tm=bs * s)
        buf_new = ps["dqkv"].reshape(bs, s, 3 * W_GROUP)[:, s - (CONV_W - 1):]
        st_s.append((ps["ak"].reshape(bs, s, N_H, HEAD_DIM), ps["av"].reshape(bs, s, N_H, HEAD_DIM),
                     ps["ik"].reshape(bs, s, IDX_DIM), ps["ck"].reshape(bs, s, N_H, HEAD_DIM),
                     ps["cv"].reshape(bs, s, N_H, HEAD_DIM), s_new, buf_new, v_rows.reshape(bs, s, W_GROUP)))

    outs_p = tuple(jnp.stack(t) for t in zip(*st_p))
    outs_s = tuple(jnp.stack(t) for t in zip(*st_s))
    return (xp.reshape(bp, seq, d), xs.reshape(bs, s, d)) + outs_p + outs_s
```
